```python
import jax, jax.numpy as jnp
from jax import lax
import numpy as np

D_MODEL = 1024
BATCH = 32
SEQ = 256
DEPTH = 2
DEC_BATCH = 2
DEC_SEQ = 2048
PAST_LEN = 256

GRID_W = 64
N_MOD = 6
EPS = 1e-6
ROPE_BASE = 10000.0
H_RET = 8
RET_DK = 64
RET_DV = 64
RET_W = H_RET * RET_DK
RET_CHUNK = 128
POOL_GROUPS = 4
POOL_GW = 128
POOL_W = POOL_GROUPS * POOL_GW
POOL_WINDOWS = (2, 4, 8, 16)
H_NA = 8
NA_HD = 64
NA_W = H_NA * NA_HD
NA_KH = 8
NA_KW = 16
ATTN_QBLOCK = 128
N_BRANCH = 3
IN_COLS = 4 * RET_W + POOL_W + 3 * NA_W + N_BRANCH * D_MODEL
N_EXPERTS = 16
EXPERT_FF = 2048
EC_CAPACITY_FACTOR = 2

kernel_name = "hybrid_retention_pool_natten_ec_diffusion_step"

F32 = jnp.float32


def rms_norm(x, g):
    xf = x.astype(F32)
    y = xf * lax.rsqrt(jnp.mean(xf * xf, axis=-1, keepdims=True) + EPS) * g.astype(F32)
    return y.astype(x.dtype)


def adaln(cvec, w, b):
    m = (jax.nn.silu(cvec) @ w + b).reshape(cvec.shape[:-1] + (N_MOD, D_MODEL))
    return [m[..., i, None, :] for i in range(N_MOD)]


def axial_rope(x):
    L, d = x.shape[1], x.shape[-1]
    t = jnp.arange(L)
    rows = (t // GRID_W).astype(F32)
    cols = (t % GRID_W).astype(F32)
    half = d // 2
    nf = half // 2
    freqs = 1.0 / (ROPE_BASE ** (jnp.arange(nf, dtype=F32) / nf))

    def rot(xh, pos):
        ang = pos[:, None] * freqs[None, :]
        cos = jnp.cos(ang)[None, :, None, :]
        sin = jnp.sin(ang)[None, :, None, :]
        x1, x2 = xh[..., :nf], xh[..., nf:]
        return jnp.concatenate([x1 * cos - x2 * sin, x2 * cos + x1 * sin], axis=-1)

    return jnp.concatenate([rot(x[..., :half], rows), rot(x[..., half:], cols)], axis=-1)


def retention_scan(q, k, v, log_g, s0):
    B, L, H, _ = q.shape
    dv = v.shape[-1]
    n = L // RET_CHUNK

    def chunks(a):
        return a.reshape(B, n, RET_CHUNK, H, a.shape[-1]).transpose(1, 0, 3, 2, 4)

    pos = jnp.arange(RET_CHUNK, dtype=F32)
    rel = pos[:, None] - pos[None, :]
    lower = rel >= 0
    intra = jnp.where(lower[None], jnp.exp(jnp.where(lower, rel, 0.0)[None] * log_g[:, None, None]), 0.0)
    q_dec = jnp.exp((pos + 1.0)[None, :] * log_g[:, None])[None, :, :, None]
    k_dec = jnp.exp((RET_CHUNK - 1.0 - pos)[None, :] * log_g[:, None])[None, :, :, None]
    s_dec = jnp.exp(RET_CHUNK * log_g)[None, :, None, None]

    def step(s, qkv):
        qi, ki, vi = qkv
        a = jnp.einsum('bhqd,bhkd->bhqk', qi, ki) * intra
        o = jnp.einsum('bhqk,bhkv->bhqv', a, vi) + jnp.einsum('bhqd,bhdv->bhqv', qi, s) * q_dec
        s = s * s_dec + jnp.einsum('bhkd,bhkv->bhdv', ki * k_dec, vi)
        return s, o

    s, o = lax.scan(step, s0, (chunks(q), chunks(k), chunks(v)))
    return o.transpose(1, 0, 3, 2, 4).reshape(B, L, H, dv), s


def retention_bidir(q, k, v, g, decay_logit, s0_f, s0_b, rotate):
    B, L, _ = q.shape
    qf = q.astype(F32).reshape(B, L, H_RET, RET_DK)
    kf = k.astype(F32).reshape(B, L, H_RET, RET_DK)
    vf = v.astype(F32).reshape(B, L, H_RET, RET_DV)
    if rotate:
        qf = axial_rope(qf)
        kf = axial_rope(kf)
    kf = kf * RET_DK ** -0.5
    log_g = jax.nn.log_sigmoid(decay_logit.astype(F32))
    o_f, s_f = retention_scan(qf, kf, vf, log_g[0], s0_f)
    o_b, s_b = retention_scan(jnp.flip(qf, 1), jnp.flip(kf, 1), jnp.flip(vf, 1), log_g[1], s0_b)
    o = o_f + jnp.flip(o_b, 1)
    o = o * lax.rsqrt(jnp.mean(o * o, axis=-1, keepdims=True) + EPS)
    o = jax.nn.silu(g.astype(F32)) * o.reshape(B, L, RET_W)
    return o.astype(q.dtype), s_f, s_b


def multiscale_pool(u, w_groups, scale):
    B, L, _ = u.shape
    uf = u.astype(F32)
    cs = jnp.concatenate([jnp.zeros((B, 1, POOL_W), F32), jnp.cumsum(uf, axis=1)], axis=1)
    t = jnp.arange(L)
    outs = []
    for gi, w in enumerate(POOL_WINDOWS):
        sl = slice(gi * POOL_GW, (gi + 1) * POOL_GW)
        lo = jnp.clip(t - w // 2, 0, L)
        hi = jnp.clip(t - w // 2 + w, 0, L)
        cnt = (hi - lo).astype(F32)[None, :, None]
        outs.append((cs[:, hi, sl] - cs[:, lo, sl]) / cnt - uf[..., sl])
    pooled = jnp.stack(outs, axis=2)
    mixed = jnp.einsum('blgc,gcd->blgd', pooled, w_groups.astype(F32)).reshape(B, L, POOL_W)
    return (mixed * scale.astype(F32)).astype(u.dtype)


def context_attention(q, k, v):
    B, Lc, H, d = q.shape
    nb = Lc // ATTN_QBLOCK
    qb = q.reshape(B, nb, ATTN_QBLOCK, H, d).transpose(1, 0, 2, 3, 4)

    def blk(qi):
        s = jnp.einsum('bqhd,bkhd->bhqk', qi, k).astype(F32) * d ** -0.5
        p = jax.nn.softmax(s, axis=-1).astype(v.dtype)
        return jnp.einsum('bhqk,bkhd->bqhd', p, v)

    o = lax.map(blk, qb)
    return o.transpose(1, 0, 2, 3, 4).reshape(B, Lc, H * d)


def neighborhood_attention(q, k, v, rpb, ctx_k, ctx_v):
    B, L, H, d = q.shape
    rows = L // GRID_W
    kh = min(NA_KH, rows)
    scale = d ** -0.5
    qg = q.reshape(B, rows, GRID_W, H, d)
    kg = k.reshape(B, rows, GRID_W, H, d)
    vg = v.reshape(B, rows, GRID_W, H, d)
    r = jnp.arange(rows)
    row_idx = jnp.clip(r - kh // 2, 0, rows - kh)[:, None] + jnp.arange(kh)[None, :]
    kb = kg[:, row_idx]
    vb = vg[:, row_idx]
    cq = jnp.arange(GRID_W)
    c_start = jnp.clip(cq - NA_KW // 2, 0, GRID_W - NA_KW)
    in_win = (cq[None, :] >= c_start[:, None]) & (cq[None, :] < c_start[:, None] + NA_KW)
    dy_idx = row_idx - r[:, None] + NA_KH - 1
    dx_idx = jnp.clip(cq[None, :] - cq[:, None] + NA_KW - 1, 0, 2 * NA_KW - 2)
    bias = rpb.astype(F32)[:, dy_idx[:, :, None, None], dx_idx[None, None]]
    bias = jnp.where(in_win[None, None, None], bias, -jnp.inf).transpose(1, 3, 0, 2, 4)
    s_loc = jnp.einsum('brqhd,brkwhd->brqhkw', qg, kb).astype(F32) * scale + bias
    s_ctx = jnp.einsum('brqhd,bchd->brqhc', qg, ctx_k).astype(F32) * scale
    nloc = kh * GRID_W
    p = jax.nn.softmax(jnp.concatenate([s_loc.reshape(B, rows, GRID_W, H, nloc), s_ctx], axis=-1), axis=-1)
    p = p.astype(v.dtype)
    p_loc = p[..., :nloc].reshape(B, rows, GRID_W, H, kh, GRID_W)
    out = (jnp.einsum('brqhkw,brkwhd->brqhd', p_loc, vb)
           + jnp.einsum('brqhc,bchd->brqhd', p[..., nloc:], ctx_v))
    return out.reshape(B, L, H * d)


def expert_choice_ffn(h, w_router, w_gate, w_up, w_down):
    B, L, D = h.shape
    n = B * L
    cap = EC_CAPACITY_FACTOR * n // N_EXPERTS
    xt = h.reshape(n, D)
    affinity = jax.nn.softmax((xt @ w_router).astype(F32), axis=-1)
    gate_vals, tok_idx = lax.top_k(affinity.T, cap)
    xe = xt[tok_idx]
    hid = jax.nn.silu(jnp.einsum('ecd,edf->ecf', xe, w_gate)) * jnp.einsum('ecd,edf->ecf', xe, w_up)
    ye = jnp.einsum('ecf,efd->ecd', hid, w_down) * gate_vals[..., None].astype(h.dtype)
    y = jnp.zeros((n, D), h.dtype).at[tok_idx.reshape(-1)].add(ye.reshape(-1, D))
    return y.reshape(B, L, D)


def token_mixer(h, p, ctx):
    B, L, _ = h.shape
    z = h @ p['w_in']
    splits = np.cumsum([RET_W] * 4 + [POOL_W] + [NA_W] * 3).tolist()
    rq, rk, rv, rg, pu, nq, nk, nv, gl = jnp.split(z, splits, axis=-1)
    nq = nq.reshape(B, L, H_NA, NA_HD)
    nk = nk.reshape(B, L, H_NA, NA_HD)
    nv = nv.reshape(B, L, H_NA, NA_HD)
    if ctx is None:
        zero = jnp.zeros((B, H_RET, RET_DK, RET_DV), F32)
        ret_o, s_f, s_b = retention_bidir(rq, rk, rv, rg, p['ret_decay'], zero, zero, False)
        na_o = context_attention(nq, nk, nv)
        cache = (nk, nv, jnp.stack([s_f, s_b], axis=1).astype(h.dtype))
    else:
        ck, cv, s0f, s0b = ctx
        ret_o, _, _ = retention_bidir(rq, rk, rv, rg, p['ret_decay'], s0f, s0b, True)
        na_o = neighborhood_attention(nq, nk, nv, p['na_rpb'], ck, cv)
        cache = None
    pool_o = multiscale_pool(pu, p['pool_w'], p['pool_scale'])
    gates = jax.nn.sigmoid(gl.astype(F32)).astype(h.dtype).reshape(B, L, N_BRANCH, D_MODEL)
    merged = (gates[:, :, 0] * (ret_o @ p['w_ret_o'])
              + gates[:, :, 1] * (pool_o @ p['w_pool_o'])
              + gates[:, :, 2] * (na_o @ p['w_na_o']))
    return merged @ p['w_o'], cache


def trunk_layer(x, cvec, ctx, p):
    sh1, sc1, g1, sh2, sc2, g2 = adaln(cvec, p['w_mod'], p['b_mod'])
    h = rms_norm(x, p['ln_pre_mix']) * (1 + sc1) + sh1
    mix, cache = token_mixer(h, p, ctx)
    x = x + g1 * rms_norm(mix, p['ln_post_mix'])
    h = rms_norm(x, p['ln_pre_ffn']) * (1 + sc2) + sh2
    ffn = expert_choice_ffn(h, p['w_router'], p['w_gate'], p['w_up'], p['w_down'])
    x = x + g2 * rms_norm(ffn, p['ln_post_ffn'])
    return x, cache


def setup_inputs(seed: int = 0) -> dict:
    key = jax.random.key(seed)
    ks = jax.random.split(key, 26)
    D = D_MODEL

    def nrm(k, shape, s):
        return jax.random.normal(k, shape, F32) * s

    return {
        "x_prompt": nrm(ks[0], (BATCH, SEQ, D), 1.0),
        "x_sample": nrm(ks[1], (DEC_BATCH, DEC_SEQ, D), 1.0),
        "cache_k": nrm(ks[2], (DEC_BATCH, DEPTH, PAST_LEN, H_NA, NA_HD), 1.0),
        "cache_v": nrm(ks[3], (DEC_BATCH, DEPTH, PAST_LEN, H_NA, NA_HD), 1.0),
        "state_ret": nrm(ks[4], (DEC_BATCH, DEPTH, 2, H_RET, RET_DK, RET_DV), 1.0),
        "c": nrm(ks[5], (DEC_BATCH, D), 1.0),
        "c_ctx": nrm(ks[6], (D,), 1.0),
        "w_mod": nrm(ks[7], (DEPTH, D, N_MOD * D), 0.5 * D ** -0.5),
        "b_mod": nrm(ks[8], (DEPTH, N_MOD * D), 0.02),
        "ln_pre_mix": 1.0 + nrm(ks[9], (DEPTH, D), 0.1),
        "ln_post_mix": 1.0 + nrm(ks[10], (DEPTH, D), 0.1),
        "ln_pre_ffn": 1.0 + nrm(ks[11], (DEPTH, D), 0.1),
        "ln_post_ffn": 1.0 + nrm(ks[12], (DEPTH, D), 0.1),
        "w_in": nrm(ks[13], (DEPTH, D, IN_COLS), D ** -0.5),
        "ret_decay": jnp.log(2.0 ** (5.0 + jnp.arange(H_RET, dtype=F32)) - 1.0)[None, None, :]
                     + nrm(ks[14], (DEPTH, 2, H_RET), 0.1),
        "pool_w": nrm(ks[15], (DEPTH, POOL_GROUPS, POOL_GW, POOL_GW), POOL_GW ** -0.5),
        "pool_scale": 1.0 + nrm(ks[16], (DEPTH, POOL_W), 0.1),
        "na_rpb": nrm(ks[17], (DEPTH, H_NA, 2 * NA_KH - 1, 2 * NA_KW - 1), 0.1),
        "w_ret_o": nrm(ks[18], (DEPTH, RET_W, D), RET_W ** -0.5),
        "w_pool_o": nrm(ks[19], (DEPTH, POOL_W, D), POOL_W ** -0.5),
        "w_na_o": nrm(ks[20], (DEPTH, NA_W, D), NA_W ** -0.5),
        "w_o": nrm(ks[21], (DEPTH, D, D), D ** -0.5),
        "w_router": nrm(ks[22], (DEPTH, D, N_EXPERTS), D ** -0.5),
        "w_gate": nrm(ks[23], (DEPTH, N_EXPERTS, D, EXPERT_FF), D ** -0.5),
        "w_up": nrm(ks[24], (DEPTH, N_EXPERTS, D, EXPERT_FF), D ** -0.5),
        "w_down": nrm(ks[25], (DEPTH, N_EXPERTS, EXPERT_FF, D), EXPERT_FF ** -0.5),
    }


def reference(x_prompt, x_sample, cache_k, cache_v, state_ret, c, c_ctx, w_mod, b_mod,
              ln_pre_mix, ln_post_mix, ln_pre_ffn, ln_post_ffn, w_in, ret_decay, pool_w, pool_scale,
              na_rpb, w_ret_o, w_pool_o, w_na_o, w_o, w_router, w_gate, w_up, w_down):
    params = [dict(w_mod=w_mod[l], b_mod=b_mod[l], ln_pre_mix=ln_pre_mix[l], ln_post_mix=ln_post_mix[l],
                   ln_pre_ffn=ln_pre_ffn[l], ln_post_ffn=ln_post_ffn[l], w_in=w_in[l], ret_decay=ret_decay[l],
                   pool_w=pool_w[l], pool_scale=pool_scale[l], na_rpb=na_rpb[l], w_ret_o=w_ret_o[l],
                   w_pool_o=w_pool_o[l], w_na_o=w_na_o[l], w_o=w_o[l], w_router=w_router[l],
                   w_gate=w_gate[l], w_up=w_up[l], w_down=w_down[l]) for l in range(DEPTH)]

    y_prompt = x_prompt
    ks_, vs_, ss_ = [], [], []
    for l in range(DEPTH):
        y_prompt, (nk, nv, st) = trunk_layer(y_prompt, c_ctx, None, params[l])
        ks_.append(nk)
        vs_.append(nv)
        ss_.append(st)
    new_cache_k = jnp.stack(ks_, axis=1)
    new_cache_v = jnp.stack(vs_, axis=1)
    new_state_ret = jnp.stack(ss_, axis=1)

    y_sample = x_sample
    for l in range(DEPTH):
        ctx = (cache_k[:, l], cache_v[:, l], state_ret[:, l, 0].astype(F32), state_ret[:, l, 1].astype(F32))
        y_sample, _ = trunk_layer(y_sample, c, ctx, params[l])

    return (y_prompt, y_sample, new_cache_k, new_cache_v, new_state_ret)
```

```python
import functools

import numpy as np
import jax
import jax.numpy as jnp
from jax import lax
from jax.experimental import pallas as pl
from jax.experimental.pallas import tpu as pltpu

F32 = jnp.float32
BF16 = jnp.bfloat16
I32 = jnp.int32

D = 1024
BATCH, SEQ = 32, 256
DEC_BATCH, DEC_SEQ = 2, 2048
DEPTH = 2
NP = BATCH * SEQ
NS = DEC_BATCH * DEC_SEQ
NT = NP + NS
GRID_W = 64
N_MOD = 6
EPS = 1e-6
ROPE_BASE = 10000.0
HEAD_DIM = 64
N_PAIRS = 4
CHUNK = 128
POOL_WINDOWS = (2, 4, 8, 16)
POOL_PAD = 16
NA_KH, NA_KW = 8, 16
N_EXPERTS = 16
EXPERT_FF = 2048
IN_COLS = 7168
LANES = 128
SLOT_CHUNK = 64
VMEM_LIMIT = 56 * 1024 * 1024

NT_DIMS = (((1,), (1,)), ((), ()))


def _params(sem, vmem=None):
    return pltpu.CompilerParams(dimension_semantics=sem, vmem_limit_bytes=vmem)


def _mod_row(row_start):
    return jnp.where(row_start < NP, 0, 1 + (row_start - NP) // DEC_SEQ)


def _silu(x):
    return x * jax.nn.sigmoid(x)


def _rms(x):
    return x * lax.rsqrt(jnp.mean(x * x, axis=-1, keepdims=True) + EPS)


def _mod_kernel(c_ref, w_ref, b_ref, o_ref):
    a = _silu(c_ref[...]).astype(BF16)
    o_ref[0] = jnp.dot(a, w_ref[0].astype(BF16), preferred_element_type=F32) + b_ref[0]


def _modulation(cvecs, w_mod, b_mod):
    out = pl.pallas_call(
        _mod_kernel,
        grid=(DEPTH, N_MOD),
        in_specs=[pl.BlockSpec((8, D), lambda l, j: (0, 0)),
                  pl.BlockSpec((1, D, D), lambda l, j: (l, 0, j)),
                  pl.BlockSpec((1, 1, D), lambda l, j: (l, 0, j))],
        out_specs=pl.BlockSpec((1, 8, D), lambda l, j: (l, 0, j)),
        out_shape=jax.ShapeDtypeStruct((DEPTH, 8, N_MOD * D), F32),
        compiler_params=_params(("arbitrary", "arbitrary")),
    )(cvecs, w_mod, b_mod.reshape(DEPTH, 1, N_MOD * D))
    return out.reshape(DEPTH, 8, N_MOD, D)


def _prenorm_kernel(x_ref, m_ref, ln_ref, h_ref):
    y = _rms(x_ref[...]) * ln_ref[...]
    h_ref[...] = (y * (1.0 + m_ref[0, 1:2, :]) + m_ref[0, 0:1, :]).astype(BF16)


def _prenorm(x_all, mod_l, ln):
    tm = 512
    return pl.pallas_call(
        _prenorm_kernel,
        grid=(NT // tm,),
        in_specs=[pl.BlockSpec((tm, D), lambda i: (i, 0)),
                  pl.BlockSpec((1, N_MOD, D), lambda i: (_mod_row(i * tm), 0, 0)),
                  pl.BlockSpec((1, D), lambda i: (0, 0))],
        out_specs=pl.BlockSpec((tm, D), lambda i: (i, 0)),
        out_shape=jax.ShapeDtypeStruct((NT, D), BF16),
        compiler_params=_params(("arbitrary",)),
    )(x_all, mod_l, ln.reshape(1, D))


def _mm_kernel(a_ref, w_ref, o_ref, wb_ref):
    @pl.when(pl.program_id(1) == 0)
    def _():
        wb_ref[...] = w_ref[...].astype(BF16)

    o_ref[...] = jnp.dot(a_ref[...], wb_ref[...], preferred_element_type=F32)


def _in_proj(h_all, w_in):
    tm, tn = 1024, 512
    return pl.pallas_call(
        _mm_kernel,
        grid=(IN_COLS // tn, NT // tm),
        in_specs=[pl.BlockSpec((tm, D), lambda j, i: (i, 0)),
                  pl.BlockSpec((D, tn), lambda j, i: (0, j))],
        out_specs=pl.BlockSpec((tm, tn), lambda j, i: (i, j)),
        out_shape=jax.ShapeDtypeStruct((NT, IN_COLS), F32),
        scratch_shapes=[pltpu.VMEM((D, tn), BF16)],
        compiler_params=_params(("arbitrary", "arbitrary"), VMEM_LIMIT),
    )(h_all, w_in)


def _swap16(x):
    lane = lax.broadcasted_iota(I32, x.shape, 1)
    return jnp.where((lane // 16) % 2 == 0, pltpu.roll(x, LANES - 16, 1), pltpu.roll(x, 16, 1))


def _block_diag(top, bottom):
    z = jnp.zeros((HEAD_DIM, HEAD_DIM), F32)
    return jnp.concatenate([jnp.concatenate([top, z], axis=1),
                            jnp.concatenate([z, bottom], axis=1)], axis=0)


def _retention_kernel(*refs, n_chunks, rotate, has_s0, emit_state):
    refs = list(refs)
    lg_ref, q_ref, k_ref, v_ref, g_ref = refs[:5]
    pos = 5
    if rotate:
        cos_ref, sin_ref = refs[pos:pos + 2]
        pos += 2
    if has_s0:
        s0_ref = refs[pos]
        pos += 1
    o_ref = refs[pos]
    pos += 1
    if emit_state:
        st_ref = refs[pos]
        pos += 1
    sf_scr, sb_scr = refs[pos:pos + 2]

    pair = pl.program_id(1)
    lane1 = lax.broadcasted_iota(I32, (1, LANES), 1)
    lo1 = lane1 < HEAD_DIM
    lgf = jnp.where(lo1, lg_ref[0, 2 * pair], lg_ref[0, 2 * pair + 1])
    lgb = jnp.where(lo1, lg_ref[1, 2 * pair], lg_ref[1, 2 * pair + 1])
    lg_heads = [(lg_ref[0, 2 * pair], lg_ref[1, 2 * pair]),
                (lg_ref[0, 2 * pair + 1], lg_ref[1, 2 * pair + 1])]

    row = lax.broadcasted_iota(I32, (CHUNK, CHUNK), 0)
    col = lax.broadcasted_iota(I32, (CHUNK, CHUNK), 1)
    rel = (row - col).astype(F32)
    lo_mask = col < HEAD_DIM
    blockdiag = (row < HEAD_DIM) == (col < HEAD_DIM)
    posf = row.astype(F32)
    dmat = []
    for hf, hb in lg_heads:
        dmat.append(jnp.where(rel >= 0, jnp.exp(jnp.where(rel >= 0, rel, 0.0) * hf), 0.0)
                    + jnp.where(rel <= 0, jnp.exp(jnp.where(rel <= 0, -rel, 0.0) * hb), 0.0))
    qdec_f = jnp.exp((posf + 1.0) * lgf)
    kdec_f = jnp.exp((CHUNK - 1.0 - posf) * lgf)
    qdec_b = jnp.exp((CHUNK - posf) * lgb)
    kdec_b = jnp.exp(posf * lgb)
    sdec_f = jnp.exp(CHUNK * lgf)
    sdec_b = jnp.exp(CHUNK * lgb)

    def load(c):
        rows = pl.ds(c * CHUNK, CHUNK)
        q = q_ref[rows, :]
        k = k_ref[rows, :]
        if rotate:
            cs, sn = cos_ref[rows, :], sin_ref[rows, :]
            q = q * cs + _swap16(q) * sn
            k = k * cs + _swap16(k) * sn
        return q, k * (HEAD_DIM ** -0.5), v_ref[rows, :]

    def state_update(s, k, v, kdec, sdec):
        kd = (k * kdec).T.astype(BF16)
        u = jnp.dot(kd, v.astype(BF16), preferred_element_type=F32)
        return s * sdec + jnp.where(blockdiag, u, 0.0)

    if has_s0:
        s_f = _block_diag(s0_ref[0, 0, 0, 0], s0_ref[0, 0, 0, 1])
        s_b = _block_diag(s0_ref[0, 0, 1, 0], s0_ref[0, 0, 1, 1])
    else:
        s_f = jnp.zeros((CHUNK, CHUNK), F32)
        s_b = jnp.zeros((CHUNK, CHUNK), F32)

    for c in range(n_chunks):
        sf_scr[c] = s_f
        _, k, v = load(c)
        s_f = state_update(s_f, k, v, kdec_f, sdec_f)
    for c in reversed(range(n_chunks)):
        sb_scr[c] = s_b
        _, k, v = load(c)
        s_b = state_update(s_b, k, v, kdec_b, sdec_b)

    if emit_state:
        st_ref[0, 0, 0] = s_f[:HEAD_DIM, :HEAD_DIM]
        st_ref[0, 0, 1] = s_f[HEAD_DIM:, HEAD_DIM:]
        st_ref[0, 1, 0] = s_b[:HEAD_DIM, :HEAD_DIM]
        st_ref[0, 1, 1] = s_b[HEAD_DIM:, HEAD_DIM:]

    for c in range(n_chunks):
        q, k, v = load(c)
        qb, kb, vb = q.astype(BF16), k.astype(BF16), v.astype(BF16)
        outs = []
        for h in range(2):
            qh = jnp.where(lo_mask if h == 0 else ~lo_mask, qb, jnp.zeros_like(qb))
            a = lax.dot_general(qh, kb, NT_DIMS, preferred_element_type=F32) * dmat[h]
            outs.append(jnp.dot(a.astype(BF16), vb, preferred_element_type=F32))
        o = jnp.where(lo_mask, outs[0], outs[1])
        o = o + jnp.dot(qb, sf_scr[c].astype(BF16), preferred_element_type=F32) * qdec_f
        o = o + jnp.dot(qb, sb_scr[c].astype(BF16), preferred_element_type=F32) * qdec_b
        o2 = o * o
        ms0 = jnp.sum(jnp.where(lo_mask, o2, 0.0), axis=1, keepdims=True) * (1.0 / HEAD_DIM)
        ms1 = jnp.sum(jnp.where(lo_mask, 0.0, o2), axis=1, keepdims=True) * (1.0 / HEAD_DIM)
        inv = jnp.where(lo_mask, lax.rsqrt(ms0 + EPS), lax.rsqrt(ms1 + EPS))
        g = g_ref[pl.ds(c * CHUNK, CHUNK), :]
        o_ref[pl.ds(c * CHUNK, CHUNK), :] = (_silu(g) * (o * inv)).astype(BF16)


def _retention(z, log_g, *, nb, seq, row_block0, rope=None, s0=None, layer=0, emit_state=False):
    n_chunks = seq // CHUNK
    rotate = rope is not None

    def zspec(cb):
        return pl.BlockSpec((seq, LANES), lambda b, p: (row_block0 + b, cb + p))

    in_specs = [pl.BlockSpec(memory_space=pltpu.SMEM), zspec(0), zspec(4), zspec(8), zspec(12)]
    args = [log_g, z, z, z, z]
    if rotate:
        in_specs += [pl.BlockSpec((seq, LANES), lambda b, p: (0, 0))] * 2
        args += list(rope)
    if s0 is not None:
        in_specs.append(pl.BlockSpec((1, 1, 2, 2, HEAD_DIM, HEAD_DIM), lambda b, p: (b, layer, 0, p, 0, 0)))
        args.append(s0)
    out_specs = [pl.BlockSpec((seq, LANES), lambda b, p: (b, p))]
    out_shape = [jax.ShapeDtypeStruct((nb * seq, N_PAIRS * LANES), BF16)]
    if emit_state:
        out_specs.append(pl.BlockSpec((1, 2, 2, HEAD_DIM, HEAD_DIM), lambda b, p: (b, 0, p, 0, 0)))
        out_shape.append(jax.ShapeDtypeStruct((nb, 2, 2 * N_PAIRS, HEAD_DIM, HEAD_DIM), F32))
    return pl.pallas_call(
        functools.partial(_retention_kernel, n_chunks=n_chunks, rotate=rotate,
                          has_s0=s0 is not None, emit_state=emit_state),
        grid=(nb, N_PAIRS),
        in_specs=in_specs,
        out_specs=out_specs,
        out_shape=out_shape,
        scratch_shapes=[pltpu.VMEM((n_chunks, CHUNK, CHUNK), F32),
                        pltpu.VMEM((n_chunks, CHUNK, CHUNK), F32)],
        compiler_params=_params(("arbitrary", "arbitrary")),
    )(*args)


def _rope_tables():
    t = np.arange(DEC_SEQ)
    posn = [(t // GRID_W).astype(np.float32), (t % GRID_W).astype(np.float32)]
    nf = HEAD_DIM // 4
    freqs = (1.0 / (np.float32(ROPE_BASE) ** (np.arange(nf, dtype=np.float32) / np.float32(nf)))).astype(np.float32)
    cos = np.zeros((DEC_SEQ, HEAD_DIM), np.float32)
    sin = np.zeros((DEC_SEQ, HEAD_DIM), np.float32)
    for half in range(2):
        ang = (posn[half][:, None] * freqs[None, :]).astype(np.float32)
        for grp in range(2):
            lo = half * 32 + grp * nf
            cos[:, lo:lo + nf] = np.cos(ang)
            sin[:, lo:lo + nf] = np.sin(ang) * (-1.0 if grp == 0 else 1.0)
    return jnp.asarray(np.tile(cos, (1, 2))), jnp.asarray(np.tile(sin, (1, 2)))


def _pool_kernel(u_ref, w_ref, sc_ref, o_ref, *, seq):
    padded = seq + 2 * POOL_PAD
    t = lax.broadcasted_iota(I32, (seq, 1), 0)
    zpad = jnp.zeros((POOL_PAD, LANES), F32)
    for gi, w in enumerate(POOL_WINDOWS):
        x = u_ref[:, gi * LANES:(gi + 1) * LANES]
        run = jnp.concatenate([zpad, x, zpad], axis=0)
        span = 1
        while span < w:
            run = run + pltpu.roll(run, padded - span, 0)
            span *= 2
        win = pltpu.roll(run, padded - (POOL_PAD - w // 2), 0)[:seq]
        cnt = (jnp.minimum(t + w // 2, seq) - jnp.maximum(t - w // 2, 0)).astype(F32)
        pooled = win / cnt - x
        mixed = jnp.dot(pooled.astype(BF16), w_ref[gi].astype(BF16), preferred_element_type=F32)
        o_ref[:, gi * LANES:(gi + 1) * LANES] = (mixed * sc_ref[:, gi * LANES:(gi + 1) * LANES]).astype(BF16)


def _pool(z, pool_w, pool_scale, *, nb, seq, row_block0):
    width = len(POOL_WINDOWS) * LANES
    return pl.pallas_call(
        functools.partial(_pool_kernel, seq=seq),
        grid=(nb,),
        in_specs=[pl.BlockSpec((seq, width), lambda b: (row_block0 + b, 2048 // width)),
                  pl.BlockSpec((len(POOL_WINDOWS), LANES, LANES), lambda b: (0, 0, 0)),
                  pl.BlockSpec((1, width), lambda b: (0, 0))],
        out_specs=pl.BlockSpec((seq, width), lambda b: (b, 0)),
        out_shape=jax.ShapeDtypeStruct((nb * seq, width), BF16),
        compiler_params=_params(("arbitrary",), VMEM_LIMIT),
    )(z, pool_w, pool_scale.reshape(1, width))


def _head_select(h, shape):
    lane = lax.broadcasted_iota(I32, shape, 1)
    return (lane < HEAD_DIM) if h == 0 else (lane >= HEAD_DIM)


def _ctx_attn_kernel(q_ref, k_ref, v_ref, o_ref):
    qb = (q_ref[...] * (HEAD_DIM ** -0.5)).astype(BF16)
    kb = k_ref[...].astype(BF16)
    vb = v_ref[...].astype(BF16)
    outs = []
    for h in range(2):
        qh = jnp.where(_head_select(h, qb.shape), qb, jnp.zeros_like(qb))
        s = lax.dot_general(qh, kb, NT_DIMS, preferred_element_type=F32)
        p = jnp.exp(s - jnp.max(s, axis=1, keepdims=True))
        denom = jnp.sum(p, axis=1, keepdims=True)
        outs.append(jnp.dot(p.astype(BF16), vb, preferred_element_type=F32) / denom)
    o_ref[...] = jnp.where(_head_select(0, outs[0].shape), outs[0], outs[1]).astype(BF16)


def _ctx_attention(z):
    def zspec(cb):
        return pl.BlockSpec((SEQ, LANES), lambda b, p: (b, cb + p))

    return pl.pallas_call(
        _ctx_attn_kernel,
        grid=(BATCH, N_PAIRS),
        in_specs=[zspec(20), zspec(24), zspec(28)],
        out_specs=pl.BlockSpec((SEQ, LANES), lambda b, p: (b, p)),
        out_shape=jax.ShapeDtypeStruct((NP, N_PAIRS * LANES), BF16),
        compiler_params=_params(("arbitrary", "arbitrary")),
    )(z, z, z)


NA_QROWS = 4
NA_QBLK = NA_QROWS * GRID_W
NA_KROWS = 12
NA_NBLK = DEC_SEQ // NA_QBLK


def _na_key_block(i):
    return jnp.clip(i - 1, 0, NA_NBLK - 3)


def _na_bias_tables(rpb):
    rows = DEC_SEQ // GRID_W
    tabs = []
    for r0, ks in ((0, 0), (NA_QROWS, 0), (rows - NA_QROWS, rows - NA_KROWS)):
        r = r0 + np.arange(NA_QROWS)[:, None, None, None]
        cq = np.arange(GRID_W)[None, :, None, None]
        kr = ks + np.arange(NA_KROWS)[None, None, :, None]
        ck = np.arange(GRID_W)[None, None, None, :]
        start = np.clip(r - NA_KH // 2, 0, rows - NA_KH)
        c_start = np.clip(cq - NA_KW // 2, 0, GRID_W - NA_KW)
        valid = (kr >= start) & (kr < start + NA_KH) & (ck >= c_start) & (ck < c_start + NA_KW)
        dy = np.clip(kr - r + NA_KH - 1, 0, 2 * NA_KH - 2)
        dx = np.clip(ck - cq + NA_KW - 1, 0, 2 * NA_KW - 2)
        shape = (NA_QROWS, GRID_W, NA_KROWS, GRID_W)
        dy, dx, valid = (np.broadcast_to(a, shape).reshape(NA_QBLK, NA_KROWS * GRID_W) for a in (dy, dx, valid))
        tabs.append(jnp.where(jnp.asarray(valid)[None], rpb.astype(F32)[:, dy, dx], -jnp.inf))
    return jnp.stack(tabs, axis=0)


def _na_kernel(q_ref, k0_ref, k1_ref, k2_ref, v0_ref, v1_ref, v2_ref, ck_ref, cv_ref, bias_ref, o_ref):
    qb = (q_ref[...] * (HEAD_DIM ** -0.5)).astype(BF16)
    ks = [r[...].astype(BF16) for r in (k0_ref, k1_ref, k2_ref)] + [ck_ref[0, 0].astype(BF16)]
    vs = [r[...].astype(BF16) for r in (v0_ref, v1_ref, v2_ref)] + [cv_ref[0, 0].astype(BF16)]
    outs = []
    for h in range(2):
        qh = jnp.where(_head_select(h, qb.shape), qb, jnp.zeros_like(qb))
        ss = []
        for j in range(4):
            s = lax.dot_general(qh, ks[j], NT_DIMS, preferred_element_type=F32)
            if j < 3:
                s = s + bias_ref[0, h, :, j * NA_QBLK:(j + 1) * NA_QBLK]
            ss.append(s)
        m = functools.reduce(jnp.maximum, [jnp.max(s, axis=1, keepdims=True) for s in ss])
        ps = [jnp.exp(s - m) for s in ss]
        denom = functools.reduce(jnp.add, [jnp.sum(p, axis=1, keepdims=True) for p in ps])
        acc = functools.reduce(jnp.add, [jnp.dot(p.astype(BF16), v, preferred_element_type=F32)
                                         for p, v in zip(ps, vs)])
        outs.append(acc / denom)
    o_ref[...] = jnp.where(_head_select(0, outs[0].shape), outs[0], outs[1]).astype(BF16)


def _neighbourhood_attention(z, bias, cache_k, cache_v, layer):
    base = NP // NA_QBLK

    def qspec():
        return pl.BlockSpec((NA_QBLK, LANES), lambda b, i, p: (base + b * NA_NBLK + i, 20 + p))

    def kvspec(cb, j):
        return pl.BlockSpec((NA_QBLK, LANES),
                            lambda b, i, p: (base + b * NA_NBLK + _na_key_block(i) + j, cb + p))

    def cspec():
        return pl.BlockSpec((1, 1, SEQ, LANES), lambda b, i, p: (b, layer, 0, p))

    pattern = lambda i: jnp.where(i == 0, 0, jnp.where(i == NA_NBLK - 1, 2, 1))
    ck = cache_k.reshape(DEC_BATCH, DEPTH, SEQ, N_PAIRS * LANES)
    cv = cache_v.reshape(DEC_BATCH, DEPTH, SEQ, N_PAIRS * LANES)
    return pl.pallas_call(
        _na_kernel,
        grid=(DEC_BATCH, NA_NBLK, N_PAIRS),
        in_specs=[qspec()] + [kvspec(24, j) for j in range(3)] + [kvspec(28, j) for j in range(3)]
                 + [cspec(), cspec(),
                    pl.BlockSpec((1, 2, NA_QBLK, NA_KROWS * GRID_W), lambda b, i, p: (pattern(i), p, 0, 0))],
        out_specs=pl.BlockSpec((NA_QBLK, LANES), lambda b, i, p: (b * NA_NBLK + i, p)),
        out_shape=jax.ShapeDtypeStruct((NS, N_PAIRS * LANES), BF16),
        compiler_params=_params(("arbitrary", "arbitrary", "arbitrary")),
    )(z, z, z, z, z, z, z, ck, cv, bias)


def _merge_kernel(r_ref, p_ref, a_ref, g0_ref, g1_ref, g2_ref, wr_ref, wp_ref, wa_ref, wo_ref,
                  x_ref, m_ref, ln_ref, o_ref, wrb, wpb, wab, wob):
    @pl.when(pl.program_id(0) == 0)
    def _():
        wrb[...] = wr_ref[...].astype(BF16)
        wpb[...] = wp_ref[...].astype(BF16)
        wab[...] = wa_ref[...].astype(BF16)
        wob[...] = wo_ref[...].astype(BF16)

    merged = (jax.nn.sigmoid(g0_ref[...]) * jnp.dot(r_ref[...], wrb[...], preferred_element_type=F32)
              + jax.nn.sigmoid(g1_ref[...]) * jnp.dot(p_ref[...], wpb[...], preferred_element_type=F32)
              + jax.nn.sigmoid(g2_ref[...]) * jnp.dot(a_ref[...], wab[...], preferred_element_type=F32))
    mix = jnp.dot(merged.astype(BF16), wob[...], preferred_element_type=F32)
    o_ref[...] = x_ref[...] + m_ref[0, 2:3, :] * (_rms(mix) * ln_ref[...])


def _merge(ret_o, pool_o, na_o, z, w_ret_o, w_pool_o, w_na_o, w_o, x_all, mod_l, ln):
    tm = 512
    half = N_PAIRS * LANES
    row = lambda i: (i, 0)
    const = lambda i: (0, 0)
    return pl.pallas_call(
        _merge_kernel,
        grid=(NT // tm,),
        in_specs=[pl.BlockSpec((tm, half), row)] * 3
                 + [pl.BlockSpec((tm, D), lambda i, c=c: (i, 4 + c)) for c in range(3)]
                 + [pl.BlockSpec((half, D), const)] * 3
                 + [pl.BlockSpec((D, D), const),
                    pl.BlockSpec((tm, D), row),
                    pl.BlockSpec((1, N_MOD, D), lambda i: (_mod_row(i * tm), 0, 0)),
                    pl.BlockSpec((1, D), const)],
        out_specs=pl.BlockSpec((tm, D), row),
        out_shape=jax.ShapeDtypeStruct((NT, D), F32),
        scratch_shapes=[pltpu.VMEM((half, D), BF16)] * 3 + [pltpu.VMEM((D, D), BF16)],
        compiler_params=_params(("arbitrary",), VMEM_LIMIT),
    )(ret_o, pool_o, na_o, z, z, z, w_ret_o, w_pool_o, w_na_o, w_o, x_all, mod_l, ln.reshape(1, D))


def _ffnprep_kernel(x_ref, m_ref, ln_ref, wr_ref, h_ref, aff_ref):
    y = _rms(x_ref[...]) * ln_ref[...]
    h = y * (1.0 + m_ref[0, 4:5, :]) + m_ref[0, 3:4, :]
    hb = h.astype(BF16)
    h_ref[...] = hb
    hl = (h - hb.astype(F32)).astype(BF16)
    w = wr_ref[...]
    wb = w.astype(BF16)
    wl = (w - wb.astype(F32)).astype(BF16)
    logits = (lax.dot_general(wb, hb, NT_DIMS, preferred_element_type=F32)
              + lax.dot_general(wb, hl, NT_DIMS, preferred_element_type=F32)
              + lax.dot_general(wl, hb, NT_DIMS, preferred_element_type=F32))
    e = jnp.exp(logits - jnp.max(logits, axis=0, keepdims=True))
    aff = e / jnp.sum(e, axis=0, keepdims=True)
    for j in range(aff_ref.shape[0]):
        aff_ref[j] = aff[:, j * LANES:(j + 1) * LANES]


def _ffnprep(x_all, mod_l, ln, w_router_t):
    tm = 512
    return pl.pallas_call(
        _ffnprep_kernel,
        grid=(NT // tm,),
        in_specs=[pl.BlockSpec((tm, D), lambda i: (i, 0)),
                  pl.BlockSpec((1, N_MOD, D), lambda i: (_mod_row(i * tm), 0, 0)),
                  pl.BlockSpec((1, D), lambda i: (0, 0)),
                  pl.BlockSpec((N_EXPERTS, D), lambda i: (0, 0))],
        out_specs=[pl.BlockSpec((tm, D), lambda i: (i, 0)),
                   pl.BlockSpec((tm // LANES, N_EXPERTS, LANES), lambda i: (i, 0, 0))],
        out_shape=[jax.ShapeDtypeStruct((NT, D), BF16),
                   jax.ShapeDtypeStruct((NT // LANES, N_EXPERTS, LANES), F32)],
        compiler_params=_params(("arbitrary",)),
    )(x_all, mod_l, ln.reshape(1, D), w_router_t)


def _route_kernel(aff_ref, slot_ref, offs_ref, ceq_ref, csel_ref, *, cap, nblk):
    def count_ge(v):
        bits = lax.bitcast_convert_type(aff_ref[...], I32)
        c = jnp.sum(jnp.where(bits >= v[None], 1.0, 0.0), axis=0)
        return jnp.sum(c, axis=1, keepdims=True)

    def search(_, lohi):
        lo, hi = lohi
        mid = lo + ((hi - lo + 1) >> 1)
        ok = count_ge(mid) >= cap
        return jnp.where(ok, mid, lo), jnp.where(ok, hi, mid - 1)

    lo0 = jnp.zeros((N_EXPERTS, 1), I32)
    hi0 = jnp.full((N_EXPERTS, 1), 0x3F800000, I32)
    thr, _ = lax.fori_loop(0, 31, search, (lo0, hi0))
    need = cap - (count_ge(thr + 1))

    upper = (lax.broadcasted_iota(I32, (LANES, LANES), 0)
             < lax.broadcasted_iota(I32, (LANES, LANES), 1)).astype(BF16)
    lane = lax.broadcasted_iota(I32, (N_EXPERTS, LANES), 1)

    ceq_ref[...] = jnp.zeros_like(ceq_ref)
    csel_ref[...] = jnp.zeros_like(csel_ref)
    offs_ref[...] = jnp.zeros_like(offs_ref)

    def block(b, carry):
        c_eq = ceq_ref[...]
        c_sel = csel_ref[...]
        bits = lax.bitcast_convert_type(aff_ref[b], I32)
        eq = bits == thr
        eqf = jnp.where(eq, 1.0, 0.0)
        eq_rank = jnp.dot(eqf.astype(BF16), upper, preferred_element_type=F32) + c_eq
        sel = (bits > thr) | (eq & (eq_rank < need))
        self_ = jnp.where(sel, 1.0, 0.0)
        rank = jnp.dot(self_.astype(BF16), upper, preferred_element_type=F32) + c_sel
        slot_ref[b] = jnp.where(sel, rank, -1.0)
        offs_ref[...] = jnp.where(lane == b, c_sel.astype(I32), offs_ref[...])
        ceq_ref[...] = c_eq + jnp.sum(eqf, axis=1, keepdims=True)
        csel_ref[...] = c_sel + jnp.sum(self_, axis=1, keepdims=True)
        return carry

    lax.fori_loop(0, nblk, block, 0)
    offs_ref[...] = jnp.where(lane >= nblk, csel_ref[...].astype(I32), offs_ref[...])


def _route(aff_blocks, *, blk0, nblk, cap):
    return pl.pallas_call(
        functools.partial(_route_kernel, cap=cap, nblk=nblk),
        grid=(1,),
        in_specs=[pl.BlockSpec((nblk, N_EXPERTS, LANES), lambda i: (blk0 // nblk, 0, 0))],
        out_specs=[pl.BlockSpec((nblk, N_EXPERTS, LANES), lambda i: (0, 0, 0)),
                   pl.BlockSpec((N_EXPERTS, LANES), lambda i: (0, 0))],
        out_shape=[jax.ShapeDtypeStruct((nblk, N_EXPERTS, LANES), F32),
                   jax.ShapeDtypeStruct((N_EXPERTS, LANES), I32)],
        scratch_shapes=[pltpu.VMEM((N_EXPERTS, LANES), F32)] * 2,
        compiler_params=_params(("arbitrary",)),
    )(aff_blocks)


GATHER_TOKENS = 256


def _gather_kernel(offs_ref, slot_ref, h_ref, xe_ref, acc_ref, *, n_tiles):
    e = pl.program_id(0)
    t = pl.program_id(1)

    @pl.when(t == 0)
    def _():
        acc_ref[...] = jnp.zeros_like(acc_ref)

    per = GATHER_TOKENS // LANES
    k0 = offs_ref[e, per * t] >> 6
    k1 = (offs_ref[e, per * t + per] + (SLOT_CHUNK - 1)) >> 6
    srow = jnp.concatenate([slot_ref[j, pl.ds(e, 1), :] for j in range(per)], axis=1)
    hb = h_ref[...]
    jj = lax.broadcasted_iota(I32, (SLOT_CHUNK, GATHER_TOKENS), 0).astype(F32)

    def body(k, carry):
        base = pl.multiple_of(k * SLOT_CHUNK, SLOT_CHUNK)
        onehot = jnp.where(srow - base.astype(F32) == jj, 1.0, 0.0).astype(BF16)
        acc_ref[pl.ds(base, SLOT_CHUNK), :] += jnp.dot(onehot, hb, preferred_element_type=F32)
        return carry

    lax.fori_loop(k0, k1, body, 0)

    @pl.when(t == n_tiles - 1)
    def _():
        xe_ref[0] = acc_ref[...].astype(BF16)


def _gather(offs, slot, h_all, *, row0, n, cap):
    n_tiles = n // GATHER_TOKENS
    per = GATHER_TOKENS // LANES
    tile0 = row0 // GATHER_TOKENS
    return pl.pallas_call(
        functools.partial(_gather_kernel, n_tiles=n_tiles),
        grid_spec=pltpu.PrefetchScalarGridSpec(
            num_scalar_prefetch=1,
            grid=(N_EXPERTS, n_tiles),
            in_specs=[pl.BlockSpec((per, N_EXPERTS, LANES), lambda e, t, o: (t, 0, 0)),
                      pl.BlockSpec((GATHER_TOKENS, D), lambda e, t, o: (tile0 + t, 0))],
            out_specs=pl.BlockSpec((1, cap, D), lambda e, t, o: (e, 0, 0)),
            scratch_shapes=[pltpu.VMEM((cap, D), F32)]),
        out_shape=jax.ShapeDtypeStruct((N_EXPERTS, cap, D), BF16),
        compiler_params=_params(("arbitrary", "arbitrary")),
    )(offs, slot, h_all)


FF_CHUNK = 512


def _experts_kernel(xp_ref, xs_ref, wg_ref, wu_ref, wd_ref, yp_ref, ys_ref, accp_ref, accs_ref, *, n_f):
    f = pl.program_id(1)
    wg = wg_ref[0].astype(BF16)
    wu = wu_ref[0].astype(BF16)
    wd = wd_ref[0].astype(BF16)

    def ffn(x):
        a = jnp.dot(x, wg, preferred_element_type=F32)
        b = jnp.dot(x, wu, preferred_element_type=F32)
        return jnp.dot((_silu(a) * b).astype(BF16), wd, preferred_element_type=F32)

    yp = ffn(xp_ref[0])
    ys = ffn(xs_ref[0])

    @pl.when(f == 0)
    def _():
        accp_ref[...] = yp
        accs_ref[...] = ys

    @pl.when(f > 0)
    def _():
        accp_ref[...] += yp
        accs_ref[...] += ys

    @pl.when(f == n_f - 1)
    def _():
        yp_ref[0] = accp_ref[...].astype(BF16)
        ys_ref[0] = accs_ref[...].astype(BF16)


def _experts(xe_p, xe_s, w_gate, w_up, w_down, layer):
    n_f = EXPERT_FF // FF_CHUNK
    cap_p, cap_s = xe_p.shape[1], xe_s.shape[1]
    xspec = lambda cap: pl.BlockSpec((1, cap, D), lambda e, f: (e, 0, 0))
    return pl.pallas_call(
        functools.partial(_experts_kernel, n_f=n_f),
        grid=(N_EXPERTS, n_f),
        in_specs=[xspec(cap_p), xspec(cap_s),
                  pl.BlockSpec((None, 1, D, FF_CHUNK), lambda e, f: (layer, e, 0, f)),
                  pl.BlockSpec((None, 1, D, FF_CHUNK), lambda e, f: (layer, e, 0, f)),
                  pl.BlockSpec((None, 1, FF_CHUNK, D), lambda e, f: (layer, e, f, 0))],
        out_specs=[xspec(cap_p), xspec(cap_s)],
        out_shape=[jax.ShapeDtypeStruct(xe_p.shape, BF16), jax.ShapeDtypeStruct(xe_s.shape, BF16)],
        scratch_shapes=[pltpu.VMEM((cap_p, D), F32), pltpu.VMEM((cap_s, D), F32)],
        compiler_params=_params(("arbitrary", "arbitrary"), VMEM_LIMIT),
    )(xe_p, xe_s, w_gate, w_up, w_down)


COMBINE_ROWS = 1024


def _combine_kernel(offs_ref, ye_ref, slot_ref, aff_ref, x_ref, m_ref, ln_ref, o_ref):
    st = pl.program_id(0)
    e = pl.program_id(1)

    @pl.when(e == 0)
    def _():
        o_ref[...] = jnp.zeros_like(o_ref)

    mine = lax.broadcasted_iota(I32, (COMBINE_ROWS, N_EXPERTS), 1) == e
    scol = jnp.sum(jnp.where(mine, slot_ref[...], 0.0), axis=1, keepdims=True)
    gcol = jnp.sum(jnp.where(mine, aff_ref[...], 0.0), axis=1, keepdims=True)
    lane = lax.broadcasted_iota(I32, (GATHER_TOKENS, SLOT_CHUNK), 1).astype(F32)
    per = GATHER_TOKENS // LANES
    for s in range(COMBINE_ROWS // GATHER_TOKENS):
        blk = st * (COMBINE_ROWS // LANES) + s * per
        k0 = offs_ref[e, blk] >> 6
        k1 = (offs_ref[e, blk + per] + (SLOT_CHUNK - 1)) >> 6
        rows = slice(s * GATHER_TOKENS, (s + 1) * GATHER_TOKENS)
        sc = scol[rows]
        gc = gcol[rows]

        def body(k, carry, sc=sc, gc=gc, rows=rows):
            base = pl.multiple_of(k * SLOT_CHUNK, SLOT_CHUNK)
            onehot = jnp.where(sc - base.astype(F32) == lane, 1.0, 0.0).astype(BF16)
            y = jnp.dot(onehot, ye_ref[0, pl.ds(base, SLOT_CHUNK), :], preferred_element_type=F32)
            o_ref[rows, :] += gc * y
            return carry

        lax.fori_loop(k0, k1, body, 0)

    @pl.when(e == N_EXPERTS - 1)
    def _():
        o_ref[...] = x_ref[...] + m_ref[0, 5:6, :] * (_rms(o_ref[...]) * ln_ref[...])


def _combine(offs, ye, slot_cols, aff_cols, x_all, mod_l, ln, *, row0, n):
    tile0 = row0 // COMBINE_ROWS
    cap = ye.shape[1]
    return pl.pallas_call(
        _combine_kernel,
        grid_spec=pltpu.PrefetchScalarGridSpec(
            num_scalar_prefetch=1,
            grid=(n // COMBINE_ROWS, N_EXPERTS),
            in_specs=[pl.BlockSpec((1, cap, D), lambda s, e, o: (e, 0, 0)),
                      pl.BlockSpec((COMBINE_ROWS, N_EXPERTS), lambda s, e, o: (s, 0)),
                      pl.BlockSpec((COMBINE_ROWS, N_EXPERTS), lambda s, e, o: (s, 0)),
                      pl.BlockSpec((COMBINE_ROWS, D), lambda s, e, o: (tile0 + s, 0)),
                      pl.BlockSpec((1, N_MOD, D), lambda s, e, o: (_mod_row(row0 + s * COMBINE_ROWS), 0, 0)),
                      pl.BlockSpec((1, D), lambda s, e, o: (0, 0))],
            out_specs=pl.BlockSpec((COMBINE_ROWS, D), lambda s, e, o: (s, 0))),
        out_shape=jax.ShapeDtypeStruct((n, D), F32),
        compiler_params=_params(("arbitrary", "arbitrary"), VMEM_LIMIT),
    )(offs, ye, slot_cols, aff_cols, x_all, mod_l, ln.reshape(1, D))


def _token_major(blocks):
    return blocks.transpose(0, 2, 1).reshape(-1, N_EXPERTS)


def kernel(x_prompt, x_sample, cache_k, cache_v, state_ret, c, c_ctx, w_mod, b_mod, ln_pre_mix, ln_post_mix,
           ln_pre_ffn, ln_post_ffn, w_in, ret_decay, pool_w, pool_scale, na_rpb, w_ret_o, w_pool_o, w_na_o, w_o,
           w_router, w_gate, w_up, w_down):
    cvecs = jnp.zeros((8, D), F32).at[0].set(c_ctx).at[1:1 + DEC_BATCH].set(c)
    mod = _modulation(cvecs, w_mod, b_mod)
    rope = _rope_tables()
    x_all = jnp.concatenate([x_prompt.reshape(NP, D), x_sample.reshape(NS, D)], axis=0)
    new_k, new_v, new_s = [], [], []
    groups = ((0, NP, NP // N_EXPERTS * 2), (NP, NS, NS // N_EXPERTS * 2))

    for l in range(DEPTH):
        mod_l = mod[l]
        h_all = _prenorm(x_all, mod_l, ln_pre_mix[l])
        z = _in_proj(h_all, w_in[l])
        log_g = jax.nn.log_sigmoid(ret_decay[l].astype(F32))

        ret_p, st = _retention(z, log_g, nb=BATCH, seq=SEQ, row_block0=0, emit_state=True)
        (ret_s,) = _retention(z, log_g, nb=DEC_BATCH, seq=DEC_SEQ, row_block0=NP // DEC_SEQ,
                              rope=rope, s0=state_ret, layer=l)
        pool_p = _pool(z, pool_w[l], pool_scale[l], nb=BATCH, seq=SEQ, row_block0=0)
        pool_s = _pool(z, pool_w[l], pool_scale[l], nb=DEC_BATCH, seq=DEC_SEQ, row_block0=NP // DEC_SEQ)
        na_p = _ctx_attention(z)
        na_s = _neighbourhood_attention(z, _na_bias_tables(na_rpb[l]), cache_k, cache_v, l)
        x_all = _merge(jnp.concatenate([ret_p, ret_s]), jnp.concatenate([pool_p, pool_s]),
                       jnp.concatenate([na_p, na_s]), z, w_ret_o[l], w_pool_o[l], w_na_o[l], w_o[l],
                       x_all, mod_l, ln_post_mix[l])
        new_k.append(z[:NP, 3072:3584].reshape(BATCH, SEQ, 8, HEAD_DIM))
        new_v.append(z[:NP, 3584:4096].reshape(BATCH, SEQ, 8, HEAD_DIM))
        new_s.append(st)

        h2, aff = _ffnprep(x_all, mod_l, ln_pre_ffn[l], w_router[l].T)
        routed = []
        for row0, n, cap in groups:
            slot, offs = _route(aff, blk0=row0 // LANES, nblk=n // LANES, cap=cap)
            routed.append((slot, offs, _gather(offs, slot, h2, row0=row0, n=n, cap=cap)))
        ye = _experts(routed[0][2], routed[1][2], w_gate, w_up, w_down, l)
        outs = []
        for (row0, n, cap), (slot, offs, _), y in zip(groups, routed, ye):
            aff_cols = _token_major(aff[row0 // LANES:(row0 + n) // LANES])
            outs.append(_combine(offs, y, _token_major(slot), aff_cols, x_all, mod_l, ln_post_ffn[l],
                                 row0=row0, n=n))
        if l + 1 < DEPTH:
            x_all = jnp.concatenate(outs, axis=0)

    y_prompt = outs[0].reshape(BATCH, SEQ, D)
    y_sample = outs[1].reshape(DEC_BATCH, DEC_SEQ, D)
    return (y_prompt, y_sample, jnp.stack(new_k, axis=1), jnp.stack(new_v, axis=1), jnp.stack(new_s, axis=1))
```

```python
import functools

import numpy as np
import jax
import jax.numpy as jnp
from jax import lax
from jax.experimental import pallas as pl
from jax.experimental.pallas import tpu as pltpu

F32 = jnp.float32
BF16 = jnp.bfloat16
I32 = jnp.int32

D = 1024
BATCH, SEQ = 32, 256
DEC_BATCH, DEC_SEQ = 2, 2048
DEPTH = 2
NP = BATCH * SEQ
NS = DEC_BATCH * DEC_SEQ
NT = NP + NS
GRID_W = 64
N_MOD = 6
EPS = 1e-6
ROPE_BASE = 10000.0
HEAD_DIM = 64
N_PAIRS = 4
CHUNK = 128
POOL_WINDOWS = (2, 4, 8, 16)
POOL_PAD = 16
NA_KH, NA_KW = 8, 16
N_EXPERTS = 16
EXPERT_FF = 2048
IN_COLS = 7168
LANES = 128
SLOT_CHUNK = 64
SLOT_SHIFT = SLOT_CHUNK.bit_length() - 1
VMEM_LIMIT = 56 * 1024 * 1024

NT_DIMS = (((1,), (1,)), ((), ()))


def _params(sem, vmem=None):
    return pltpu.CompilerParams(dimension_semantics=sem, vmem_limit_bytes=vmem)


def _mod_row(row_start):
    return jnp.where(row_start < NP, 0, 1 + (row_start - NP) // DEC_SEQ)


def _silu(x):
    return x * jax.nn.sigmoid(x)


def _rms(x):
    return x * lax.rsqrt(jnp.mean(x * x, axis=-1, keepdims=True) + EPS)


def _pair_specs(tm, width):
    n_p = NP // tm
    return [pl.BlockSpec((tm, width), lambda i: (jnp.minimum(i, n_p - 1), 0)),
            pl.BlockSpec((tm, width), lambda i: (jnp.maximum(i - n_p, 0), 0))]


def _pick(p_ref, s_ref):
    return jnp.where(pl.program_id(0) < NP // p_ref.shape[0], p_ref[...], s_ref[...])


def _mod_kernel(c_ref, w_ref, b_ref, o_ref):
    a = _silu(c_ref[...]).astype(BF16)
    o_ref[0] = jnp.dot(a, w_ref[0].astype(BF16), preferred_element_type=F32) + b_ref[0]


def _modulation(cvecs, w_mod, b_mod):
    out = pl.pallas_call(
        _mod_kernel,
        name="modulation",
        grid=(DEPTH, N_MOD),
        in_specs=[pl.BlockSpec((8, D), lambda l, j: (0, 0)),
                  pl.BlockSpec((1, D, D), lambda l, j: (l, 0, j)),
                  pl.BlockSpec((1, 1, D), lambda l, j: (l, 0, j))],
        out_specs=pl.BlockSpec((1, 8, D), lambda l, j: (l, 0, j)),
        out_shape=jax.ShapeDtypeStruct((DEPTH, 8, N_MOD * D), F32),
        compiler_params=_params(("arbitrary", "arbitrary")),
    )(cvecs, w_mod, b_mod.reshape(DEPTH, 1, N_MOD * D))
    return out.reshape(DEPTH, 8, N_MOD, D)


def _prenorm_kernel(xp_ref, xs_ref, m_ref, ln_ref, h_ref):
    y = _rms(_pick(xp_ref, xs_ref)) * ln_ref[...]
    h_ref[...] = (y * (1.0 + m_ref[0, 1:2, :]) + m_ref[0, 0:1, :]).astype(BF16)


def _prenorm(x_pair, mod_l, ln):
    tm = 512
    return pl.pallas_call(
        _prenorm_kernel,
        name="prenorm",
        grid=(NT // tm,),
        in_specs=_pair_specs(tm, D)
                 + [pl.BlockSpec((1, N_MOD, D), lambda i: (_mod_row(i * tm), 0, 0)),
                  pl.BlockSpec((1, D), lambda i: (0, 0))],
        out_specs=pl.BlockSpec((tm, D), lambda i: (i, 0)),
        out_shape=jax.ShapeDtypeStruct((NT, D), BF16),
        compiler_params=_params(("arbitrary",)),
    )(*x_pair, mod_l, ln.reshape(1, D))


def _mm_kernel(a_ref, w_ref, o_ref, wb_ref):
    @pl.when(pl.program_id(1) == 0)
    def _():
        wb_ref[...] = w_ref[...].astype(BF16)

    o_ref[...] = jnp.dot(a_ref[...], wb_ref[...], preferred_element_type=F32)


def _in_proj(h_all, w_in):
    tm, tn = 1024, 512
    return pl.pallas_call(
        _mm_kernel,
        name="in_proj",
        grid=(IN_COLS // tn, NT // tm),
        in_specs=[pl.BlockSpec((tm, D), lambda j, i: (i, 0)),
                  pl.BlockSpec((D, tn), lambda j, i: (0, j))],
        out_specs=pl.BlockSpec((tm, tn), lambda j, i: (i, j)),
        out_shape=jax.ShapeDtypeStruct((NT, IN_COLS), F32),
        scratch_shapes=[pltpu.VMEM((D, tn), BF16)],
        compiler_params=_params(("arbitrary", "arbitrary"), VMEM_LIMIT),
    )(h_all, w_in)


def _swap16(x):
    lane = lax.broadcasted_iota(I32, x.shape, 1)
    return jnp.where((lane // 16) % 2 == 0, pltpu.roll(x, LANES - 16, 1), pltpu.roll(x, 16, 1))


def _block_diag(top, bottom):
    z = jnp.zeros((HEAD_DIM, HEAD_DIM), F32)
    return jnp.concatenate([jnp.concatenate([top, z], axis=1),
                            jnp.concatenate([z, bottom], axis=1)], axis=0)


def _retention_kernel(*refs, n_chunks, rotate, has_s0, emit_state):
    refs = list(refs)
    lg_ref, q_ref, k_ref, v_ref, g_ref = refs[:5]
    pos = 5
    if rotate:
        cos_ref, sin_ref = refs[pos:pos + 2]
        pos += 2
    if has_s0:
        s0_ref = refs[pos]
        pos += 1
    o_ref = refs[pos]
    pos += 1
    if emit_state:
        st_ref = refs[pos]
        pos += 1
    sf_scr, sb_scr = refs[pos:pos + 2]

    pair = pl.program_id(1)
    lane1 = lax.broadcasted_iota(I32, (1, LANES), 1)
    lo1 = lane1 < HEAD_DIM
    lgf = jnp.where(lo1, lg_ref[0, 2 * pair], lg_ref[0, 2 * pair + 1])
    lgb = jnp.where(lo1, lg_ref[1, 2 * pair], lg_ref[1, 2 * pair + 1])
    lg_heads = [(lg_ref[0, 2 * pair], lg_ref[1, 2 * pair]),
                (lg_ref[0, 2 * pair + 1], lg_ref[1, 2 * pair + 1])]

    row = lax.broadcasted_iota(I32, (CHUNK, CHUNK), 0)
    col = lax.broadcasted_iota(I32, (CHUNK, CHUNK), 1)
    rel = (row - col).astype(F32)
    lo_mask = col < HEAD_DIM
    blockdiag = (row < HEAD_DIM) == (col < HEAD_DIM)
    posf = row.astype(F32)
    dmat = []
    for hf, hb in lg_heads:
        dmat.append(jnp.where(rel >= 0, jnp.exp(jnp.where(rel >= 0, rel, 0.0) * hf), 0.0)
                    + jnp.where(rel <= 0, jnp.exp(jnp.where(rel <= 0, -rel, 0.0) * hb), 0.0))
    qdec_f = jnp.exp((posf + 1.0) * lgf)
    kdec_f = jnp.exp((CHUNK - 1.0 - posf) * lgf)
    qdec_b = jnp.exp((CHUNK - posf) * lgb)
    kdec_b = jnp.exp(posf * lgb)
    sdec_f = jnp.exp(CHUNK * lgf)
    sdec_b = jnp.exp(CHUNK * lgb)

    def load(c):
        rows = pl.ds(c * CHUNK, CHUNK)
        q = q_ref[rows, :]
        k = k_ref[rows, :]
        if rotate:
            cs, sn = cos_ref[rows, :], sin_ref[rows, :]
            q = q * cs + _swap16(q) * sn
            k = k * cs + _swap16(k) * sn
        return q, k * (HEAD_DIM ** -0.5), v_ref[rows, :]

    def state_update(s, k, v, kdec, sdec):
        kd = (k * kdec).T.astype(BF16)
        u = jnp.dot(kd, v.astype(BF16), preferred_element_type=F32)
        return s * sdec + jnp.where(blockdiag, u, 0.0)

    if has_s0:
        s_f = _block_diag(s0_ref[0, 0, 0, 0], s0_ref[0, 0, 0, 1])
        s_b = _block_diag(s0_ref[0, 0, 1, 0], s0_ref[0, 0, 1, 1])
    else:
        s_f = jnp.zeros((CHUNK, CHUNK), F32)
        s_b = jnp.zeros((CHUNK, CHUNK), F32)

    for c in range(n_chunks):
        sf_scr[c] = s_f
        _, k, v = load(c)
        s_f = state_update(s_f, k, v, kdec_f, sdec_f)
    for c in reversed(range(n_chunks)):
        sb_scr[c] = s_b
        _, k, v = load(c)
        s_b = state_update(s_b, k, v, kdec_b, sdec_b)

    if emit_state:
        st_ref[0, 0, 0] = s_f[:HEAD_DIM, :HEAD_DIM]
        st_ref[0, 0, 1] = s_f[HEAD_DIM:, HEAD_DIM:]
        st_ref[0, 1, 0] = s_b[:HEAD_DIM, :HEAD_DIM]
        st_ref[0, 1, 1] = s_b[HEAD_DIM:, HEAD_DIM:]

    for c in range(n_chunks):
        q, k, v = load(c)
        qb, kb, vb = q.astype(BF16), k.astype(BF16), v.astype(BF16)
        outs = []
        for h in range(2):
            qh = jnp.where(lo_mask if h == 0 else ~lo_mask, qb, jnp.zeros_like(qb))
            a = lax.dot_general(qh, kb, NT_DIMS, preferred_element_type=F32) * dmat[h]
            outs.append(jnp.dot(a.astype(BF16), vb, preferred_element_type=F32))
        o = jnp.where(lo_mask, outs[0], outs[1])
        o = o + jnp.dot(qb, sf_scr[c].astype(BF16), preferred_element_type=F32) * qdec_f
        o = o + jnp.dot(qb, sb_scr[c].astype(BF16), preferred_element_type=F32) * qdec_b
        o2 = o * o
        ms0 = jnp.sum(jnp.where(lo_mask, o2, 0.0), axis=1, keepdims=True) * (1.0 / HEAD_DIM)
        ms1 = jnp.sum(jnp.where(lo_mask, 0.0, o2), axis=1, keepdims=True) * (1.0 / HEAD_DIM)
        inv = jnp.where(lo_mask, lax.rsqrt(ms0 + EPS), lax.rsqrt(ms1 + EPS))
        g = g_ref[pl.ds(c * CHUNK, CHUNK), :]
        o_ref[pl.ds(c * CHUNK, CHUNK), :] = (_silu(g) * (o * inv)).astype(BF16)


def _retention(z, log_g, *, nb, seq, row_block0, rope=None, s0=None, layer=0, emit_state=False):
    n_chunks = seq // CHUNK
    rotate = rope is not None

    def zspec(cb):
        return pl.BlockSpec((seq, LANES), lambda b, p: (row_block0 + b, cb + p))

    in_specs = [pl.BlockSpec(memory_space=pltpu.SMEM), zspec(0), zspec(4), zspec(8), zspec(12)]
    args = [log_g, z, z, z, z]
    if rotate:
        in_specs += [pl.BlockSpec((seq, LANES), lambda b, p: (0, 0))] * 2
        args += list(rope)
    if s0 is not None:
        in_specs.append(pl.BlockSpec((1, 1, 2, 2, HEAD_DIM, HEAD_DIM), lambda b, p: (b, layer, 0, p, 0, 0)))
        args.append(s0)
    out_specs = [pl.BlockSpec((seq, LANES), lambda b, p: (b, p))]
    out_shape = [jax.ShapeDtypeStruct((nb * seq, N_PAIRS * LANES), BF16)]
    if emit_state:
        out_specs.append(pl.BlockSpec((1, 2, 2, HEAD_DIM, HEAD_DIM), lambda b, p: (b, 0, p, 0, 0)))
        out_shape.append(jax.ShapeDtypeStruct((nb, 2, 2 * N_PAIRS, HEAD_DIM, HEAD_DIM), F32))
    return pl.pallas_call(
        functools.partial(_retention_kernel, n_chunks=n_chunks, rotate=rotate,
                          has_s0=s0 is not None, emit_state=emit_state),
        name="retention",
        grid=(nb, N_PAIRS),
        in_specs=in_specs,
        out_specs=out_specs,
        out_shape=out_shape,
        scratch_shapes=[pltpu.VMEM((n_chunks, CHUNK, CHUNK), F32),
                        pltpu.VMEM((n_chunks, CHUNK, CHUNK), F32)],
        compiler_params=_params(("arbitrary", "arbitrary")),
    )(*args)


def _rope_tables():
    t = np.arange(DEC_SEQ)
    posn = [(t // GRID_W).astype(np.float32), (t % GRID_W).astype(np.float32)]
    nf = HEAD_DIM // 4
    freqs = (1.0 / (np.float32(ROPE_BASE) ** (np.arange(nf, dtype=np.float32) / np.float32(nf)))).astype(np.float32)
    cos = np.zeros((DEC_SEQ, HEAD_DIM), np.float32)
    sin = np.zeros((DEC_SEQ, HEAD_DIM), np.float32)
    for half in range(2):
        ang = (posn[half][:, None] * freqs[None, :]).astype(np.float32)
        for grp in range(2):
            lo = half * 32 + grp * nf
            cos[:, lo:lo + nf] = np.cos(ang)
            sin[:, lo:lo + nf] = np.sin(ang) * (-1.0 if grp == 0 else 1.0)
    return jnp.asarray(np.tile(cos, (1, 2))), jnp.asarray(np.tile(sin, (1, 2)))


def _pool_kernel(u_ref, w_ref, sc_ref, o_ref, *, seq):
    padded = seq + 2 * POOL_PAD
    t = lax.broadcasted_iota(I32, (seq, 1), 0)
    zpad = jnp.zeros((POOL_PAD, LANES), F32)
    for gi, w in enumerate(POOL_WINDOWS):
        x = u_ref[:, gi * LANES:(gi + 1) * LANES]
        run = jnp.concatenate([zpad, x, zpad], axis=0)
        span = 1
        while span < w:
            run = run + pltpu.roll(run, padded - span, 0)
            span *= 2
        win = pltpu.roll(run, padded - (POOL_PAD - w // 2), 0)[:seq]
        cnt = (jnp.minimum(t + w // 2, seq) - jnp.maximum(t - w // 2, 0)).astype(F32)
        pooled = win / cnt - x
        mixed = jnp.dot(pooled.astype(BF16), w_ref[gi].astype(BF16), preferred_element_type=F32)
        o_ref[:, gi * LANES:(gi + 1) * LANES] = (mixed * sc_ref[:, gi * LANES:(gi + 1) * LANES]).astype(BF16)


def _pool(z, pool_w, pool_scale, *, nb, seq, row_block0):
    width = len(POOL_WINDOWS) * LANES
    return pl.pallas_call(
        functools.partial(_pool_kernel, seq=seq),
        name="pool",
        grid=(nb,),
        in_specs=[pl.BlockSpec((seq, width), lambda b: (row_block0 + b, 2048 // width)),
                  pl.BlockSpec((len(POOL_WINDOWS), LANES, LANES), lambda b: (0, 0, 0)),
                  pl.BlockSpec((1, width), lambda b: (0, 0))],
        out_specs=pl.BlockSpec((seq, width), lambda b: (b, 0)),
        out_shape=jax.ShapeDtypeStruct((nb * seq, width), BF16),
        compiler_params=_params(("arbitrary",), VMEM_LIMIT),
    )(z, pool_w, pool_scale.reshape(1, width))


def _head_select(h, shape):
    lane = lax.broadcasted_iota(I32, shape, 1)
    return (lane < HEAD_DIM) if h == 0 else (lane >= HEAD_DIM)


def _ctx_attn_kernel(q_ref, k_ref, v_ref, o_ref):
    qb = (q_ref[...] * (HEAD_DIM ** -0.5)).astype(BF16)
    kb = k_ref[...].astype(BF16)
    vb = v_ref[...].astype(BF16)
    outs = []
    for h in range(2):
        qh = jnp.where(_head_select(h, qb.shape), qb, jnp.zeros_like(qb))
        s = lax.dot_general(qh, kb, NT_DIMS, preferred_element_type=F32)
        p = jnp.exp(s - jnp.max(s, axis=1, keepdims=True))
        denom = jnp.sum(p, axis=1, keepdims=True)
        outs.append(jnp.dot(p.astype(BF16), vb, preferred_element_type=F32) / denom)
    o_ref[...] = jnp.where(_head_select(0, outs[0].shape), outs[0], outs[1]).astype(BF16)


def _ctx_attention(z):
    def zspec(cb):
        return pl.BlockSpec((SEQ, LANES), lambda b, p: (b, cb + p))

    return pl.pallas_call(
        _ctx_attn_kernel,
        name="ctx_attn",
        grid=(BATCH, N_PAIRS),
        in_specs=[zspec(20), zspec(24), zspec(28)],
        out_specs=pl.BlockSpec((SEQ, LANES), lambda b, p: (b, p)),
        out_shape=jax.ShapeDtypeStruct((NP, N_PAIRS * LANES), BF16),
        compiler_params=_params(("arbitrary", "arbitrary")),
    )(z, z, z)


NA_QROWS = 4
NA_QBLK = NA_QROWS * GRID_W
NA_KROWS = 12
NA_NBLK = DEC_SEQ // NA_QBLK


def _na_key_block(i):
    return jnp.clip(i - 1, 0, NA_NBLK - 3)


def _na_bias_tables(rpb):
    rows = DEC_SEQ // GRID_W
    ny, nx = 2 * NA_KH - 1, 2 * NA_KW - 1
    cq = np.arange(GRID_W)[:, None]
    ck = np.arange(GRID_W)[None, :]
    c_start = np.clip(cq - NA_KW // 2, 0, GRID_W - NA_KW)
    col_ok = (ck >= c_start) & (ck < c_start + NA_KW)
    dx = np.clip(ck - cq + NA_KW - 1, 0, nx - 1)
    xsel = (dx[None] == np.arange(nx)[:, None, None]).astype(np.float32)
    ysel, row_ok = [], []
    for r0, ks in ((0, 0), (NA_QROWS, 0), (rows - NA_QROWS, rows - NA_KROWS)):
        r = r0 + np.arange(NA_QROWS)[:, None]
        kr = ks + np.arange(NA_KROWS)[None, :]
        start = np.clip(r - NA_KH // 2, 0, rows - NA_KH)
        row_ok.append((kr >= start) & (kr < start + NA_KH))
        dy = np.clip(kr - r + NA_KH - 1, 0, ny - 1)
        ysel.append((dy[..., None] == np.arange(ny)).astype(np.float32))
    by_row = jnp.einsum("prky,hyx->phrkx", jnp.asarray(np.stack(ysel)), rpb.astype(F32),
                        precision=lax.Precision.HIGHEST)
    bias = jnp.einsum("phrkx,xqc->phrqkc", by_row, jnp.asarray(xsel), precision=lax.Precision.HIGHEST)
    valid = np.stack(row_ok)[:, None, :, None, :, None] & col_ok[None, None, None, :, None, :]
    bias = jnp.where(jnp.asarray(valid), bias, -jnp.inf)
    return bias.reshape(3, 2 * N_PAIRS, NA_QBLK, NA_KROWS * GRID_W)


def _na_kernel(q_ref, k0_ref, k1_ref, k2_ref, v0_ref, v1_ref, v2_ref, ck_ref, cv_ref, bias_ref, o_ref):
    qb = (q_ref[...] * (HEAD_DIM ** -0.5)).astype(BF16)
    ks = [r[...].astype(BF16) for r in (k0_ref, k1_ref, k2_ref)] + [ck_ref[0, 0].astype(BF16)]
    vs = [r[...].astype(BF16) for r in (v0_ref, v1_ref, v2_ref)] + [cv_ref[0, 0].astype(BF16)]
    outs = []
    for h in range(2):
        qh = jnp.where(_head_select(h, qb.shape), qb, jnp.zeros_like(qb))
        ss = []
        for j in range(4):
            s = lax.dot_general(qh, ks[j], NT_DIMS, preferred_element_type=F32)
            if j < 3:
                s = s + bias_ref[0, h, :, j * NA_QBLK:(j + 1) * NA_QBLK]
            ss.append(s)
        m = functools.reduce(jnp.maximum, [jnp.max(s, axis=1, keepdims=True) for s in ss])
        ps = [jnp.exp(s - m) for s in ss]
        denom = functools.reduce(jnp.add, [jnp.sum(p, axis=1, keepdims=True) for p in ps])
        acc = functools.reduce(jnp.add, [jnp.dot(p.astype(BF16), v, preferred_element_type=F32)
                                         for p, v in zip(ps, vs)])
        outs.append(acc / denom)
    o_ref[...] = jnp.where(_head_select(0, outs[0].shape), outs[0], outs[1]).astype(BF16)


def _neighbourhood_attention(z, bias, cache_k, cache_v, layer):
    base = NP // NA_QBLK

    def qspec():
        return pl.BlockSpec((NA_QBLK, LANES), lambda b, i, p: (base + b * NA_NBLK + i, 20 + p))

    def kvspec(cb, j):
        return pl.BlockSpec((NA_QBLK, LANES),
                            lambda b, i, p: (base + b * NA_NBLK + _na_key_block(i) + j, cb + p))

    def cspec():
        return pl.BlockSpec((1, 1, SEQ, LANES), lambda b, i, p: (b, layer, 0, p))

    pattern = lambda i: jnp.where(i == 0, 0, jnp.where(i == NA_NBLK - 1, 2, 1))
    ck = cache_k.reshape(DEC_BATCH, DEPTH, SEQ, N_PAIRS * LANES)
    cv = cache_v.reshape(DEC_BATCH, DEPTH, SEQ, N_PAIRS * LANES)
    return pl.pallas_call(
        _na_kernel,
        name="nbr_attn",
        grid=(DEC_BATCH, NA_NBLK, N_PAIRS),
        in_specs=[qspec()] + [kvspec(24, j) for j in range(3)] + [kvspec(28, j) for j in range(3)]
                 + [cspec(), cspec(),
                    pl.BlockSpec((1, 2, NA_QBLK, NA_KROWS * GRID_W), lambda b, i, p: (pattern(i), p, 0, 0))],
        out_specs=pl.BlockSpec((NA_QBLK, LANES), lambda b, i, p: (b * NA_NBLK + i, p)),
        out_shape=jax.ShapeDtypeStruct((NS, N_PAIRS * LANES), BF16),
        compiler_params=_params(("arbitrary", "arbitrary", "arbitrary")),
    )(z, z, z, z, z, z, z, ck, cv, bias)


def _merge_kernel(rp_ref, rs_ref, pp_ref, ps_ref, ap_ref, as_ref, xp_ref, xs_ref, g0_ref, g1_ref, g2_ref,
                  wr_ref, wp_ref, wa_ref, wo_ref, m_ref, ln_ref, o_ref, wrb, wpb, wab, wob):
    @pl.when(pl.program_id(0) == 0)
    def _():
        wrb[...] = wr_ref[...].astype(BF16)
        wpb[...] = wp_ref[...].astype(BF16)
        wab[...] = wa_ref[...].astype(BF16)
        wob[...] = wo_ref[...].astype(BF16)

    branch = lambda p_ref, s_ref, w: jnp.dot(_pick(p_ref, s_ref), w[...], preferred_element_type=F32)
    merged = (jax.nn.sigmoid(g0_ref[...]) * branch(rp_ref, rs_ref, wrb)
              + jax.nn.sigmoid(g1_ref[...]) * branch(pp_ref, ps_ref, wpb)
              + jax.nn.sigmoid(g2_ref[...]) * branch(ap_ref, as_ref, wab))
    mix = jnp.dot(merged.astype(BF16), wob[...], preferred_element_type=F32)
    o_ref[...] = _pick(xp_ref, xs_ref) + m_ref[0, 2:3, :] * (_rms(mix) * ln_ref[...])


def _merge(ret_pair, pool_pair, na_pair, x_pair, z, w_ret_o, w_pool_o, w_na_o, w_o, mod_l, ln):
    tm = 512
    half = N_PAIRS * LANES
    row = lambda i: (i, 0)
    const = lambda i: (0, 0)
    return pl.pallas_call(
        _merge_kernel,
        name="merge",
        grid=(NT // tm,),
        in_specs=_pair_specs(tm, half) * 3 + _pair_specs(tm, D)
                 + [pl.BlockSpec((tm, D), lambda i, c=c: (i, 4 + c)) for c in range(3)]
                 + [pl.BlockSpec((half, D), const)] * 3
                 + [pl.BlockSpec((D, D), const),
                    pl.BlockSpec((1, N_MOD, D), lambda i: (_mod_row(i * tm), 0, 0)),
                    pl.BlockSpec((1, D), const)],
        out_specs=pl.BlockSpec((tm, D), row),
        out_shape=jax.ShapeDtypeStruct((NT, D), F32),
        scratch_shapes=[pltpu.VMEM((half, D), BF16)] * 3 + [pltpu.VMEM((D, D), BF16)],
        compiler_params=_params(("arbitrary",), VMEM_LIMIT),
    )(*ret_pair, *pool_pair, *na_pair, *x_pair, z, z, z, w_ret_o, w_pool_o, w_na_o, w_o, mod_l, ln.reshape(1, D))


def _ffnprep_kernel(x_ref, m_ref, ln_ref, wr_ref, h_ref, aff_ref):
    y = _rms(x_ref[...]) * ln_ref[...]
    h = y * (1.0 + m_ref[0, 4:5, :]) + m_ref[0, 3:4, :]
    hb = h.astype(BF16)
    h_ref[...] = hb
    hl = (h - hb.astype(F32)).astype(BF16)
    w = wr_ref[...]
    wb = w.astype(BF16)
    wl = (w - wb.astype(F32)).astype(BF16)
    logits = (lax.dot_general(wb, hb, NT_DIMS, preferred_element_type=F32)
              + lax.dot_general(wb, hl, NT_DIMS, preferred_element_type=F32)
              + lax.dot_general(wl, hb, NT_DIMS, preferred_element_type=F32))
    e = jnp.exp(logits - jnp.max(logits, axis=0, keepdims=True))
    aff = e / jnp.sum(e, axis=0, keepdims=True)
    for j in range(aff_ref.shape[0]):
        aff_ref[j] = aff[:, j * LANES:(j + 1) * LANES]


def _ffnprep(x_all, mod_l, ln, w_router_t):
    tm = 512
    return pl.pallas_call(
        _ffnprep_kernel,
        name="ffn_prep",
        grid=(NT // tm,),
        in_specs=[pl.BlockSpec((tm, D), lambda i: (i, 0)),
                  pl.BlockSpec((1, N_MOD, D), lambda i: (_mod_row(i * tm), 0, 0)),
                  pl.BlockSpec((1, D), lambda i: (0, 0)),
                  pl.BlockSpec((N_EXPERTS, D), lambda i: (0, 0))],
        out_specs=[pl.BlockSpec((tm, D), lambda i: (i, 0)),
                   pl.BlockSpec((tm // LANES, N_EXPERTS, LANES), lambda i: (i, 0, 0))],
        out_shape=[jax.ShapeDtypeStruct((NT, D), BF16),
                   jax.ShapeDtypeStruct((NT // LANES, N_EXPERTS, LANES), F32)],
        compiler_params=_params(("arbitrary",)),
    )(x_all, mod_l, ln.reshape(1, D), w_router_t)


def _route_kernel(aff_ref, slot_ref, offs_ref, ceq_ref, csel_ref, *, cap, nblk):
    def count_ge(v):
        bits = lax.bitcast_convert_type(aff_ref[...], I32)
        c = jnp.sum(jnp.where(bits >= v[None], 1.0, 0.0), axis=0)
        return jnp.sum(c, axis=1, keepdims=True)

    def search(_, lohi):
        lo, hi = lohi
        mid = lo + ((hi - lo + 1) >> 1)
        ok = count_ge(mid) >= cap
        return jnp.where(ok, mid, lo), jnp.where(ok, hi, mid - 1)

    lo0 = jnp.zeros((N_EXPERTS, 1), I32)
    hi0 = jnp.full((N_EXPERTS, 1), 0x3F800000, I32)
    thr, _ = lax.fori_loop(0, 31, search, (lo0, hi0))
    need = cap - (count_ge(thr + 1))

    upper = (lax.broadcasted_iota(I32, (LANES, LANES), 0)
             < lax.broadcasted_iota(I32, (LANES, LANES), 1)).astype(BF16)
    lane = lax.broadcasted_iota(I32, (N_EXPERTS, LANES), 1)

    ceq_ref[...] = jnp.zeros_like(ceq_ref)
    csel_ref[...] = jnp.zeros_like(csel_ref)
    offs_ref[...] = jnp.zeros_like(offs_ref)

    def block(b, carry):
        c_eq = ceq_ref[...]
        c_sel = csel_ref[...]
        bits = lax.bitcast_convert_type(aff_ref[b], I32)
        eq = bits == thr
        eqf = jnp.where(eq, 1.0, 0.0)
        eq_rank = jnp.dot(eqf.astype(BF16), upper, preferred_element_type=F32) + c_eq
        sel = (bits > thr) | (eq & (eq_rank < need))
        self_ = jnp.where(sel, 1.0, 0.0)
        rank = jnp.dot(self_.astype(BF16), upper, preferred_element_type=F32) + c_sel
        slot_ref[b] = jnp.where(sel, rank, -1.0)
        offs_ref[...] = jnp.where(lane == b, c_sel.astype(I32), offs_ref[...])
        ceq_ref[...] = c_eq + jnp.sum(eqf, axis=1, keepdims=True)
        csel_ref[...] = c_sel + jnp.sum(self_, axis=1, keepdims=True)
        return carry

    lax.fori_loop(0, nblk, block, 0)
    offs_ref[...] = jnp.where(lane >= nblk, csel_ref[...].astype(I32), offs_ref[...])


def _route(aff_blocks, *, blk0, nblk, cap):
    return pl.pallas_call(
        functools.partial(_route_kernel, cap=cap, nblk=nblk),
        name="route",
        grid=(1,),
        in_specs=[pl.BlockSpec((nblk, N_EXPERTS, LANES), lambda i: (blk0 // nblk, 0, 0))],
        out_specs=[pl.BlockSpec((nblk, N_EXPERTS, LANES), lambda i: (0, 0, 0)),
                   pl.BlockSpec((N_EXPERTS, LANES), lambda i: (0, 0))],
        out_shape=[jax.ShapeDtypeStruct((nblk, N_EXPERTS, LANES), F32),
                   jax.ShapeDtypeStruct((N_EXPERTS, LANES), I32)],
        scratch_shapes=[pltpu.VMEM((N_EXPERTS, LANES), F32)] * 2,
        compiler_params=_params(("arbitrary",)),
    )(aff_blocks)


GATHER_TOKENS = 256


def _gather_kernel(offs_ref, slot_ref, h_ref, xe_ref, acc_ref, *, n_tiles):
    e = pl.program_id(0)
    t = pl.program_id(1)

    @pl.when(t == 0)
    def _():
        acc_ref[...] = jnp.zeros_like(acc_ref)

    per = GATHER_TOKENS // LANES
    k0 = offs_ref[e, per * t] >> SLOT_SHIFT
    k1 = (offs_ref[e, per * t + per] + (SLOT_CHUNK - 1)) >> SLOT_SHIFT
    srow = jnp.concatenate([slot_ref[j, pl.ds(e, 1), :] for j in range(per)], axis=1)
    hb = h_ref[...]
    jj = lax.broadcasted_iota(I32, (SLOT_CHUNK, GATHER_TOKENS), 0).astype(F32)

    def body(k, carry):
        base = pl.multiple_of(k * SLOT_CHUNK, SLOT_CHUNK)
        onehot = jnp.where(srow - base.astype(F32) == jj, 1.0, 0.0).astype(BF16)
        acc_ref[pl.ds(base, SLOT_CHUNK), :] += jnp.dot(onehot, hb, preferred_element_type=F32)
        return carry

    lax.fori_loop(k0, k1, body, 0)

    @pl.when(t == n_tiles - 1)
    def _():
        xe_ref[0] = acc_ref[...].astype(BF16)


def _gather(offs, slot, h_all, *, row0, n, cap):
    n_tiles = n // GATHER_TOKENS
    per = GATHER_TOKENS // LANES
    tile0 = row0 // GATHER_TOKENS
    return pl.pallas_call(
        functools.partial(_gather_kernel, n_tiles=n_tiles),
        name="gather",
        grid_spec=pltpu.PrefetchScalarGridSpec(
            num_scalar_prefetch=1,
            grid=(N_EXPERTS, n_tiles),
            in_specs=[pl.BlockSpec((per, N_EXPERTS, LANES), lambda e, t, o: (t, 0, 0)),
                      pl.BlockSpec((GATHER_TOKENS, D), lambda e, t, o: (tile0 + t, 0))],
            out_specs=pl.BlockSpec((1, cap, D), lambda e, t, o: (e, 0, 0)),
            scratch_shapes=[pltpu.VMEM((cap, D), F32)]),
        out_shape=jax.ShapeDtypeStruct((N_EXPERTS, cap, D), BF16),
        compiler_params=_params(("arbitrary", "arbitrary")),
    )(offs, slot, h_all)


FF_CHUNK = 512


def _experts_kernel(xp_ref, xs_ref, wg_ref, wu_ref, wd_ref, yp_ref, ys_ref, accp_ref, accs_ref, *, n_f):
    f = pl.program_id(1)
    wg = wg_ref[0].astype(BF16)
    wu = wu_ref[0].astype(BF16)
    wd = wd_ref[0].astype(BF16)

    def ffn(x):
        a = jnp.dot(x, wg, preferred_element_type=F32)
        b = jnp.dot(x, wu, preferred_element_type=F32)
        return jnp.dot((_silu(a) * b).astype(BF16), wd, preferred_element_type=F32)

    yp = ffn(xp_ref[0])
    ys = ffn(xs_ref[0])

    @pl.when(f == 0)
    def _():
        accp_ref[...] = yp
        accs_ref[...] = ys

    @pl.when(f > 0)
    def _():
        accp_ref[...] += yp
        accs_ref[...] += ys

    @pl.when(f == n_f - 1)
    def _():
        yp_ref[0] = accp_ref[...].astype(BF16)
        ys_ref[0] = accs_ref[...].astype(BF16)


def _experts(xe_p, xe_s, w_gate, w_up, w_down, layer):
    n_f = EXPERT_FF // FF_CHUNK
    cap_p, cap_s = xe_p.shape[1], xe_s.shape[1]
    xspec = lambda cap: pl.BlockSpec((1, cap, D), lambda e, f: (e, 0, 0))
    return pl.pallas_call(
        functools.partial(_experts_kernel, n_f=n_f),
        name="experts",
        grid=(N_EXPERTS, n_f),
        in_specs=[xspec(cap_p), xspec(cap_s),
                  pl.BlockSpec((None, 1, D, FF_CHUNK), lambda e, f: (layer, e, 0, f)),
                  pl.BlockSpec((None, 1, D, FF_CHUNK), lambda e, f: (layer, e, 0, f)),
                  pl.BlockSpec((None, 1, FF_CHUNK, D), lambda e, f: (layer, e, f, 0))],
        out_specs=[xspec(cap_p), xspec(cap_s)],
        out_shape=[jax.ShapeDtypeStruct(xe_p.shape, BF16), jax.ShapeDtypeStruct(xe_s.shape, BF16)],
        scratch_shapes=[pltpu.VMEM((cap_p, D), F32), pltpu.VMEM((cap_s, D), F32)],
        compiler_params=_params(("arbitrary", "arbitrary"), VMEM_LIMIT),
    )(xe_p, xe_s, w_gate, w_up, w_down)


COMBINE_ROWS = 1024


def _combine_kernel(offs_ref, ye_ref, slot_ref, aff_ref, x_ref, m_ref, ln_ref, o_ref):
    st = pl.program_id(0)
    e = pl.program_id(1)

    @pl.when(e == 0)
    def _():
        o_ref[...] = jnp.zeros_like(o_ref)

    mine = lax.broadcasted_iota(I32, (COMBINE_ROWS, N_EXPERTS), 1) == e
    scol = jnp.sum(jnp.where(mine, slot_ref[...], 0.0), axis=1, keepdims=True)
    gcol = jnp.sum(jnp.where(mine, aff_ref[...], 0.0), axis=1, keepdims=True)
    lane = lax.broadcasted_iota(I32, (GATHER_TOKENS, SLOT_CHUNK), 1).astype(F32)
    per = GATHER_TOKENS // LANES
    for s in range(COMBINE_ROWS // GATHER_TOKENS):
        blk = st * (COMBINE_ROWS // LANES) + s * per
        k0 = offs_ref[e, blk] >> SLOT_SHIFT
        k1 = (offs_ref[e, blk + per] + (SLOT_CHUNK - 1)) >> SLOT_SHIFT
        rows = slice(s * GATHER_TOKENS, (s + 1) * GATHER_TOKENS)
        sc = scol[rows]
        gc = gcol[rows]

        def body(k, carry, sc=sc, gc=gc, rows=rows):
            base = pl.multiple_of(k * SLOT_CHUNK, SLOT_CHUNK)
            onehot = jnp.where(sc - base.astype(F32) == lane, 1.0, 0.0).astype(BF16)
            y = jnp.dot(onehot, ye_ref[0, pl.ds(base, SLOT_CHUNK), :], preferred_element_type=F32)
            o_ref[rows, :] += gc * y
            return carry

        lax.fori_loop(k0, k1, body, 0)

    @pl.when(e == N_EXPERTS - 1)
    def _():
        o_ref[...] = x_ref[...] + m_ref[0, 5:6, :] * (_rms(o_ref[...]) * ln_ref[...])


def _combine(offs, ye, slot_cols, aff_cols, x_all, mod_l, ln, *, row0, n):
    tile0 = row0 // COMBINE_ROWS
    cap = ye.shape[1]
    return pl.pallas_call(
        _combine_kernel,
        name="combine",
        grid_spec=pltpu.PrefetchScalarGridSpec(
            num_scalar_prefetch=1,
            grid=(n // COMBINE_ROWS, N_EXPERTS),
            in_specs=[pl.BlockSpec((1, cap, D), lambda s, e, o: (e, 0, 0)),
                      pl.BlockSpec((COMBINE_ROWS, N_EXPERTS), lambda s, e, o: (s, 0)),
                      pl.BlockSpec((COMBINE_ROWS, N_EXPERTS), lambda s, e, o: (s, 0)),
                      pl.BlockSpec((COMBINE_ROWS, D), lambda s, e, o: (tile0 + s, 0)),
                      pl.BlockSpec((1, N_MOD, D), lambda s, e, o: (_mod_row(row0 + s * COMBINE_ROWS), 0, 0)),
                      pl.BlockSpec((1, D), lambda s, e, o: (0, 0))],
            out_specs=pl.BlockSpec((COMBINE_ROWS, D), lambda s, e, o: (s, 0))),
        out_shape=jax.ShapeDtypeStruct((n, D), F32),
        compiler_params=_params(("arbitrary", "arbitrary"), VMEM_LIMIT),
    )(offs, ye, slot_cols, aff_cols, x_all, mod_l, ln.reshape(1, D))


def _token_major(blocks):
    return blocks.transpose(0, 2, 1).reshape(-1, N_EXPERTS)


def kernel(x_prompt, x_sample, cache_k, cache_v, state_ret, c, c_ctx, w_mod, b_mod, ln_pre_mix, ln_post_mix,
           ln_pre_ffn, ln_post_ffn, w_in, ret_decay, pool_w, pool_scale, na_rpb, w_ret_o, w_pool_o, w_na_o, w_o,
           w_router, w_gate, w_up, w_down):
    cvecs = jnp.zeros((8, D), F32).at[0].set(c_ctx).at[1:1 + DEC_BATCH].set(c)
    mod = _modulation(cvecs, w_mod, b_mod)
    rope = _rope_tables()
    x_pair = (x_prompt.reshape(NP, D), x_sample.reshape(NS, D))
    new_k, new_v, new_s = [], [], []
    groups = ((0, NP, NP // N_EXPERTS * 2), (NP, NS, NS // N_EXPERTS * 2))

    for l in range(DEPTH):
        mod_l = mod[l]
        h_all = _prenorm(x_pair, mod_l, ln_pre_mix[l])
        z = _in_proj(h_all, w_in[l])
        log_g = jax.nn.log_sigmoid(ret_decay[l].astype(F32))

        ret_p, st = _retention(z, log_g, nb=BATCH, seq=SEQ, row_block0=0, emit_state=True)
        (ret_s,) = _retention(z, log_g, nb=DEC_BATCH, seq=DEC_SEQ, row_block0=NP // DEC_SEQ,
                              rope=rope, s0=state_ret, layer=l)
        pool_p = _pool(z, pool_w[l], pool_scale[l], nb=BATCH, seq=SEQ, row_block0=0)
        pool_s = _pool(z, pool_w[l], pool_scale[l], nb=DEC_BATCH, seq=DEC_SEQ, row_block0=NP // DEC_SEQ)
        na_p = _ctx_attention(z)
        na_s = _neighbourhood_attention(z, _na_bias_tables(na_rpb[l]), cache_k, cache_v, l)
        x_mid = _merge((ret_p, ret_s), (pool_p, pool_s), (na_p, na_s), x_pair, z,
                       w_ret_o[l], w_pool_o[l], w_na_o[l], w_o[l], mod_l, ln_post_mix[l])
        new_k.append(z[:NP, 3072:3584].reshape(BATCH, SEQ, 8, HEAD_DIM))
        new_v.append(z[:NP, 3584:4096].reshape(BATCH, SEQ, 8, HEAD_DIM))
        new_s.append(st)

        h2, aff = _ffnprep(x_mid, mod_l, ln_pre_ffn[l], w_router[l].T)
        routed = []
        for row0, n, cap in groups:
            slot, offs = _route(aff, blk0=row0 // LANES, nblk=n // LANES, cap=cap)
            routed.append((slot, offs, _gather(offs, slot, h2, row0=row0, n=n, cap=cap)))
        ye = _experts(routed[0][2], routed[1][2], w_gate, w_up, w_down, l)
        outs = []
        for (row0, n, cap), (slot, offs, _), y in zip(groups, routed, ye):
            aff_cols = _token_major(aff[row0 // LANES:(row0 + n) // LANES])
            outs.append(_combine(offs, y, _token_major(slot), aff_cols, x_mid, mod_l, ln_post_ffn[l],
                                 row0=row0, n=n))
        x_pair = tuple(outs)

    y_prompt = x_pair[0].reshape(BATCH, SEQ, D)
    y_sample = x_pair[1].reshape(DEC_BATCH, DEC_SEQ, D)
    return (y_prompt, y_sample, jnp.stack(new_k, axis=1), jnp.stack(new_v, axis=1), jnp.stack(new_s, axis=1))
```

```python
import functools

import numpy as np
import jax
import jax.numpy as jnp
from jax import lax
from jax.experimental import pallas as pl
from jax.experimental.pallas import tpu as pltpu

F32 = jnp.float32
BF16 = jnp.bfloat16
I32 = jnp.int32

D = 1024
BATCH, SEQ = 32, 256
DEC_BATCH, DEC_SEQ = 2, 2048
DEPTH = 2
NP = BATCH * SEQ
NS = DEC_BATCH * DEC_SEQ
NT = NP + NS
GRID_W = 64
N_MOD = 6
EPS = 1e-6
ROPE_BASE = 10000.0
HEAD_DIM = 64
N_PAIRS = 4
CHUNK = 128
POOL_WINDOWS = (2, 4, 8, 16)
POOL_PAD = 16
NA_KH, NA_KW = 8, 16
N_EXPERTS = 16
EXPERT_FF = 2048
IN_COLS = 7168
LANES = 128
VMEM_LIMIT = 56 * 1024 * 1024

NT_DIMS = (((1,), (1,)), ((), ()))


def _params(sem, vmem=None):
    return pltpu.CompilerParams(dimension_semantics=sem, vmem_limit_bytes=vmem)


def _mod_row(row_start):
    return jnp.where(row_start < NP, 0, 1 + (row_start - NP) // DEC_SEQ)


def _silu(x):
    return x * jax.nn.sigmoid(x)


def _rms(x):
    return x * lax.rsqrt(jnp.mean(x * x, axis=-1, keepdims=True) + EPS)


def _pair_specs(tm, width):
    n_p = NP // tm
    return [pl.BlockSpec((tm, width), lambda i: (jnp.minimum(i, n_p - 1), 0)),
            pl.BlockSpec((tm, width), lambda i: (jnp.maximum(i - n_p, 0), 0))]


def _pick(p_ref, s_ref):
    return jnp.where(pl.program_id(0) < NP // p_ref.shape[0], p_ref[...], s_ref[...])


def _mod_kernel(c_ref, w_ref, b_ref, o_ref):
    a = _silu(c_ref[...]).astype(BF16)
    o_ref[0] = jnp.dot(a, w_ref[0].astype(BF16), preferred_element_type=F32) + b_ref[0]


def _modulation(cvecs, w_mod, b_mod):
    out = pl.pallas_call(
        _mod_kernel,
        name="modulation",
        grid=(DEPTH, N_MOD),
        in_specs=[pl.BlockSpec((8, D), lambda l, j: (0, 0)),
                  pl.BlockSpec((1, D, D), lambda l, j: (l, 0, j)),
                  pl.BlockSpec((1, 1, D), lambda l, j: (l, 0, j))],
        out_specs=pl.BlockSpec((1, 8, D), lambda l, j: (l, 0, j)),
        out_shape=jax.ShapeDtypeStruct((DEPTH, 8, N_MOD * D), F32),
        compiler_params=_params(("arbitrary", "arbitrary")),
    )(cvecs, w_mod, b_mod.reshape(DEPTH, 1, N_MOD * D))
    return out.reshape(DEPTH, 8, N_MOD, D)


def _prenorm_kernel(xp_ref, xs_ref, m_ref, ln_ref, h_ref):
    y = _rms(_pick(xp_ref, xs_ref)) * ln_ref[...]
    h_ref[...] = (y * (1.0 + m_ref[0, 1:2, :]) + m_ref[0, 0:1, :]).astype(BF16)


def _prenorm(x_pair, mod_l, ln):
    tm = 512
    return pl.pallas_call(
        _prenorm_kernel,
        name="prenorm",
        grid=(NT // tm,),
        in_specs=_pair_specs(tm, D)
                 + [pl.BlockSpec((1, N_MOD, D), lambda i: (_mod_row(i * tm), 0, 0)),
                  pl.BlockSpec((1, D), lambda i: (0, 0))],
        out_specs=pl.BlockSpec((tm, D), lambda i: (i, 0)),
        out_shape=jax.ShapeDtypeStruct((NT, D), BF16),
        compiler_params=_params(("arbitrary",)),
    )(*x_pair, mod_l, ln.reshape(1, D))


def _mm_kernel(a_ref, w_ref, o_ref, wb_ref):
    @pl.when(pl.program_id(1) == 0)
    def _():
        wb_ref[...] = w_ref[...].astype(BF16)

    o_ref[...] = jnp.dot(a_ref[...], wb_ref[...], preferred_element_type=F32)


def _in_proj(h_all, w_in):
    tm, tn = 1024, 512
    return pl.pallas_call(
        _mm_kernel,
        name="in_proj",
        grid=(IN_COLS // tn, NT // tm),
        in_specs=[pl.BlockSpec((tm, D), lambda j, i: (i, 0)),
                  pl.BlockSpec((D, tn), lambda j, i: (0, j))],
        out_specs=pl.BlockSpec((tm, tn), lambda j, i: (i, j)),
        out_shape=jax.ShapeDtypeStruct((NT, IN_COLS), F32),
        scratch_shapes=[pltpu.VMEM((D, tn), BF16)],
        compiler_params=_params(("arbitrary", "arbitrary"), VMEM_LIMIT),
    )(h_all, w_in)


def _swap16(x):
    lane = lax.broadcasted_iota(I32, x.shape, 1)
    return jnp.where((lane // 16) % 2 == 0, pltpu.roll(x, LANES - 16, 1), pltpu.roll(x, 16, 1))


def _block_diag(top, bottom):
    z = jnp.zeros((HEAD_DIM, HEAD_DIM), F32)
    return jnp.concatenate([jnp.concatenate([top, z], axis=1),
                            jnp.concatenate([z, bottom], axis=1)], axis=0)


def _retention_kernel(*refs, n_chunks, rotate, has_s0, emit_state):
    refs = list(refs)
    lg_ref, q_ref, k_ref, v_ref, g_ref = refs[:5]
    pos = 5
    if rotate:
        cos_ref, sin_ref = refs[pos:pos + 2]
        pos += 2
    if has_s0:
        s0_ref = refs[pos]
        pos += 1
    o_ref = refs[pos]
    pos += 1
    if emit_state:
        st_ref = refs[pos]
        pos += 1
    sf_scr, sb_scr = refs[pos:pos + 2]

    pair = pl.program_id(1)
    lane1 = lax.broadcasted_iota(I32, (1, LANES), 1)
    lo1 = lane1 < HEAD_DIM
    lgf = jnp.where(lo1, lg_ref[0, 2 * pair], lg_ref[0, 2 * pair + 1])
    lgb = jnp.where(lo1, lg_ref[1, 2 * pair], lg_ref[1, 2 * pair + 1])
    lg_heads = [(lg_ref[0, 2 * pair], lg_ref[1, 2 * pair]),
                (lg_ref[0, 2 * pair + 1], lg_ref[1, 2 * pair + 1])]

    row = lax.broadcasted_iota(I32, (CHUNK, CHUNK), 0)
    col = lax.broadcasted_iota(I32, (CHUNK, CHUNK), 1)
    rel = (row - col).astype(F32)
    lo_mask = col < HEAD_DIM
    blockdiag = (row < HEAD_DIM) == (col < HEAD_DIM)
    posf = row.astype(F32)
    dmat = []
    for hf, hb in lg_heads:
        dmat.append(jnp.where(rel >= 0, jnp.exp(jnp.where(rel >= 0, rel, 0.0) * hf), 0.0)
                    + jnp.where(rel <= 0, jnp.exp(jnp.where(rel <= 0, -rel, 0.0) * hb), 0.0))
    qdec_f = jnp.exp((posf + 1.0) * lgf)
    kdec_f = jnp.exp((CHUNK - 1.0 - posf) * lgf)
    qdec_b = jnp.exp((CHUNK - posf) * lgb)
    kdec_b = jnp.exp(posf * lgb)
    sdec_f = jnp.exp(CHUNK * lgf)
    sdec_b = jnp.exp(CHUNK * lgb)

    def load(c):
        rows = pl.ds(c * CHUNK, CHUNK)
        q = q_ref[rows, :]
        k = k_ref[rows, :]
        if rotate:
            cs, sn = cos_ref[rows, :], sin_ref[rows, :]
            q = q * cs + _swap16(q) * sn
            k = k * cs + _swap16(k) * sn
        return q, k * (HEAD_DIM ** -0.5), v_ref[rows, :]

    def state_update(s, k, v, kdec, sdec):
        kd = (k * kdec).T.astype(BF16)
        u = jnp.dot(kd, v.astype(BF16), preferred_element_type=F32)
        return s * sdec + jnp.where(blockdiag, u, 0.0)

    if has_s0:
        s_f = _block_diag(s0_ref[0, 0, 0, 0], s0_ref[0, 0, 0, 1])
        s_b = _block_diag(s0_ref[0, 0, 1, 0], s0_ref[0, 0, 1, 1])
    else:
        s_f = jnp.zeros((CHUNK, CHUNK), F32)
        s_b = jnp.zeros((CHUNK, CHUNK), F32)

    for c in range(n_chunks):
        sf_scr[c] = s_f
        _, k, v = load(c)
        s_f = state_update(s_f, k, v, kdec_f, sdec_f)
    for c in reversed(range(n_chunks)):
        sb_scr[c] = s_b
        _, k, v = load(c)
        s_b = state_update(s_b, k, v, kdec_b, sdec_b)

    if emit_state:
        st_ref[0, 0, 0] = s_f[:HEAD_DIM, :HEAD_DIM]
        st_ref[0, 0, 1] = s_f[HEAD_DIM:, HEAD_DIM:]
        st_ref[0, 1, 0] = s_b[:HEAD_DIM, :HEAD_DIM]
        st_ref[0, 1, 1] = s_b[HEAD_DIM:, HEAD_DIM:]

    for c in range(n_chunks):
        q, k, v = load(c)
        qb, kb, vb = q.astype(BF16), k.astype(BF16), v.astype(BF16)
        outs = []
        for h in range(2):
            qh = jnp.where(lo_mask if h == 0 else ~lo_mask, qb, jnp.zeros_like(qb))
            a = lax.dot_general(qh, kb, NT_DIMS, preferred_element_type=F32) * dmat[h]
            outs.append(jnp.dot(a.astype(BF16), vb, preferred_element_type=F32))
        o = jnp.where(lo_mask, outs[0], outs[1])
        o = o + jnp.dot(qb, sf_scr[c].astype(BF16), preferred_element_type=F32) * qdec_f
        o = o + jnp.dot(qb, sb_scr[c].astype(BF16), preferred_element_type=F32) * qdec_b
        o2 = o * o
        ms0 = jnp.sum(jnp.where(lo_mask, o2, 0.0), axis=1, keepdims=True) * (1.0 / HEAD_DIM)
        ms1 = jnp.sum(jnp.where(lo_mask, 0.0, o2), axis=1, keepdims=True) * (1.0 / HEAD_DIM)
        inv = jnp.where(lo_mask, lax.rsqrt(ms0 + EPS), lax.rsqrt(ms1 + EPS))
        g = g_ref[pl.ds(c * CHUNK, CHUNK), :]
        o_ref[pl.ds(c * CHUNK, CHUNK), :] = (_silu(g) * (o * inv)).astype(BF16)


def _retention(z, log_g, *, nb, seq, row_block0, rope=None, s0=None, layer=0, emit_state=False):
    n_chunks = seq // CHUNK
    rotate = rope is not None

    def zspec(cb):
        return pl.BlockSpec((seq, LANES), lambda b, p: (row_block0 + b, cb + p))

    in_specs = [pl.BlockSpec(memory_space=pltpu.SMEM), zspec(0), zspec(4), zspec(8), zspec(12)]
    args = [log_g, z, z, z, z]
    if rotate:
        in_specs += [pl.BlockSpec((seq, LANES), lambda b, p: (0, 0))] * 2
        args += list(rope)
    if s0 is not None:
        in_specs.append(pl.BlockSpec((1, 1, 2, 2, HEAD_DIM, HEAD_DIM), lambda b, p: (b, layer, 0, p, 0, 0)))
        args.append(s0)
    out_specs = [pl.BlockSpec((seq, LANES), lambda b, p: (b, p))]
    out_shape = [jax.ShapeDtypeStruct((nb * seq, N_PAIRS * LANES), BF16)]
    if emit_state:
        out_specs.append(pl.BlockSpec((1, 2, 2, HEAD_DIM, HEAD_DIM), lambda b, p: (b, 0, p, 0, 0)))
        out_shape.append(jax.ShapeDtypeStruct((nb, 2, 2 * N_PAIRS, HEAD_DIM, HEAD_DIM), F32))
    return pl.pallas_call(
        functools.partial(_retention_kernel, n_chunks=n_chunks, rotate=rotate,
                          has_s0=s0 is not None, emit_state=emit_state),
        name="retention",
        grid=(nb, N_PAIRS),
        in_specs=in_specs,
        out_specs=out_specs,
        out_shape=out_shape,
        scratch_shapes=[pltpu.VMEM((n_chunks, CHUNK, CHUNK), F32),
                        pltpu.VMEM((n_chunks, CHUNK, CHUNK), F32)],
        compiler_params=_params(("arbitrary", "arbitrary")),
    )(*args)


def _rope_tables():
    t = np.arange(DEC_SEQ)
    posn = [(t // GRID_W).astype(np.float32), (t % GRID_W).astype(np.float32)]
    nf = HEAD_DIM // 4
    freqs = (1.0 / (np.float32(ROPE_BASE) ** (np.arange(nf, dtype=np.float32) / np.float32(nf)))).astype(np.float32)
    cos = np.zeros((DEC_SEQ, HEAD_DIM), np.float32)
    sin = np.zeros((DEC_SEQ, HEAD_DIM), np.float32)
    for half in range(2):
        ang = (posn[half][:, None] * freqs[None, :]).astype(np.float32)
        for grp in range(2):
            lo = half * 32 + grp * nf
            cos[:, lo:lo + nf] = np.cos(ang)
            sin[:, lo:lo + nf] = np.sin(ang) * (-1.0 if grp == 0 else 1.0)
    return jnp.asarray(np.tile(cos, (1, 2))), jnp.asarray(np.tile(sin, (1, 2)))


def _pool_kernel(u_ref, w_ref, sc_ref, o_ref, *, seq):
    padded = seq + 2 * POOL_PAD
    t = lax.broadcasted_iota(I32, (seq, 1), 0)
    zpad = jnp.zeros((POOL_PAD, LANES), F32)
    for gi, w in enumerate(POOL_WINDOWS):
        x = u_ref[:, gi * LANES:(gi + 1) * LANES]
        run = jnp.concatenate([zpad, x, zpad], axis=0)
        span = 1
        while span < w:
            run = run + pltpu.roll(run, padded - span, 0)
            span *= 2
        win = pltpu.roll(run, padded - (POOL_PAD - w // 2), 0)[:seq]
        cnt = (jnp.minimum(t + w // 2, seq) - jnp.maximum(t - w // 2, 0)).astype(F32)
        pooled = win / cnt - x
        mixed = jnp.dot(pooled.astype(BF16), w_ref[gi].astype(BF16), preferred_element_type=F32)
        o_ref[:, gi * LANES:(gi + 1) * LANES] = (mixed * sc_ref[:, gi * LANES:(gi + 1) * LANES]).astype(BF16)


def _pool(z, pool_w, pool_scale, *, nb, seq, row_block0):
    width = len(POOL_WINDOWS) * LANES
    return pl.pallas_call(
        functools.partial(_pool_kernel, seq=seq),
        name="pool",
        grid=(nb,),
        in_specs=[pl.BlockSpec((seq, width), lambda b: (row_block0 + b, 2048 // width)),
                  pl.BlockSpec((len(POOL_WINDOWS), LANES, LANES), lambda b: (0, 0, 0)),
                  pl.BlockSpec((1, width), lambda b: (0, 0))],
        out_specs=pl.BlockSpec((seq, width), lambda b: (b, 0)),
        out_shape=jax.ShapeDtypeStruct((nb * seq, width), BF16),
        compiler_params=_params(("arbitrary",), VMEM_LIMIT),
    )(z, pool_w, pool_scale.reshape(1, width))


def _head_select(h, shape):
    lane = lax.broadcasted_iota(I32, shape, 1)
    return (lane < HEAD_DIM) if h == 0 else (lane >= HEAD_DIM)


def _ctx_attn_kernel(q_ref, k_ref, v_ref, o_ref):
    qb = (q_ref[...] * (HEAD_DIM ** -0.5)).astype(BF16)
    kb = k_ref[...].astype(BF16)
    vb = v_ref[...].astype(BF16)
    outs = []
    for h in range(2):
        qh = jnp.where(_head_select(h, qb.shape), qb, jnp.zeros_like(qb))
        s = lax.dot_general(qh, kb, NT_DIMS, preferred_element_type=F32)
        p = jnp.exp(s - jnp.max(s, axis=1, keepdims=True))
        denom = jnp.sum(p, axis=1, keepdims=True)
        outs.append(jnp.dot(p.astype(BF16), vb, preferred_element_type=F32) / denom)
    o_ref[...] = jnp.where(_head_select(0, outs[0].shape), outs[0], outs[1]).astype(BF16)


def _ctx_attention(z):
    def zspec(cb):
        return pl.BlockSpec((SEQ, LANES), lambda b, p: (b, cb + p))

    return pl.pallas_call(
        _ctx_attn_kernel,
        name="ctx_attn",
        grid=(BATCH, N_PAIRS),
        in_specs=[zspec(20), zspec(24), zspec(28)],
        out_specs=pl.BlockSpec((SEQ, LANES), lambda b, p: (b, p)),
        out_shape=jax.ShapeDtypeStruct((NP, N_PAIRS * LANES), BF16),
        compiler_params=_params(("arbitrary", "arbitrary")),
    )(z, z, z)


NA_QROWS = 4
NA_QBLK = NA_QROWS * GRID_W
NA_KROWS = 12
NA_NBLK = DEC_SEQ // NA_QBLK


def _na_key_block(i):
    return jnp.clip(i - 1, 0, NA_NBLK - 3)


def _na_bias_tables(rpb):
    rows = DEC_SEQ // GRID_W
    ny, nx = 2 * NA_KH - 1, 2 * NA_KW - 1
    cq = np.arange(GRID_W)[:, None]
    ck = np.arange(GRID_W)[None, :]
    c_start = np.clip(cq - NA_KW // 2, 0, GRID_W - NA_KW)
    col_ok = (ck >= c_start) & (ck < c_start + NA_KW)
    dx = np.clip(ck - cq + NA_KW - 1, 0, nx - 1)
    xsel = (dx[None] == np.arange(nx)[:, None, None]).astype(np.float32)
    ysel, row_ok = [], []
    for r0, ks in ((0, 0), (NA_QROWS, 0), (rows - NA_QROWS, rows - NA_KROWS)):
        r = r0 + np.arange(NA_QROWS)[:, None]
        kr = ks + np.arange(NA_KROWS)[None, :]
        start = np.clip(r - NA_KH // 2, 0, rows - NA_KH)
        row_ok.append((kr >= start) & (kr < start + NA_KH))
        dy = np.clip(kr - r + NA_KH - 1, 0, ny - 1)
        ysel.append((dy[..., None] == np.arange(ny)).astype(np.float32))
    by_row = jnp.einsum("prky,hyx->phrkx", jnp.asarray(np.stack(ysel)), rpb.astype(F32),
                        precision=lax.Precision.HIGHEST)
    bias = jnp.einsum("phrkx,xqc->phrqkc", by_row, jnp.asarray(xsel), precision=lax.Precision.HIGHEST)
    valid = np.stack(row_ok)[:, None, :, None, :, None] & col_ok[None, None, None, :, None, :]
    bias = jnp.where(jnp.asarray(valid), bias, -jnp.inf)
    return bias.reshape(3, 2 * N_PAIRS, NA_QBLK, NA_KROWS * GRID_W)


def _na_kernel(q_ref, k0_ref, k1_ref, k2_ref, v0_ref, v1_ref, v2_ref, ck_ref, cv_ref, bias_ref, o_ref):
    qb = (q_ref[...] * (HEAD_DIM ** -0.5)).astype(BF16)
    ks = [r[...].astype(BF16) for r in (k0_ref, k1_ref, k2_ref)] + [ck_ref[0, 0].astype(BF16)]
    vs = [r[...].astype(BF16) for r in (v0_ref, v1_ref, v2_ref)] + [cv_ref[0, 0].astype(BF16)]
    outs = []
    for h in range(2):
        qh = jnp.where(_head_select(h, qb.shape), qb, jnp.zeros_like(qb))
        ss = []
        for j in range(4):
            s = lax.dot_general(qh, ks[j], NT_DIMS, preferred_element_type=F32)
            if j < 3:
                s = s + bias_ref[0, h, :, j * NA_QBLK:(j + 1) * NA_QBLK]
            ss.append(s)
        m = functools.reduce(jnp.maximum, [jnp.max(s, axis=1, keepdims=True) for s in ss])
        ps = [jnp.exp(s - m) for s in ss]
        denom = functools.reduce(jnp.add, [jnp.sum(p, axis=1, keepdims=True) for p in ps])
        acc = functools.reduce(jnp.add, [jnp.dot(p.astype(BF16), v, preferred_element_type=F32)
                                         for p, v in zip(ps, vs)])
        outs.append(acc / denom)
    o_ref[...] = jnp.where(_head_select(0, outs[0].shape), outs[0], outs[1]).astype(BF16)


def _neighbourhood_attention(z, bias, cache_k, cache_v, layer):
    base = NP // NA_QBLK

    def qspec():
        return pl.BlockSpec((NA_QBLK, LANES), lambda b, i, p: (base + b * NA_NBLK + i, 20 + p))

    def kvspec(cb, j):
        return pl.BlockSpec((NA_QBLK, LANES),
                            lambda b, i, p: (base + b * NA_NBLK + _na_key_block(i) + j, cb + p))

    def cspec():
        return pl.BlockSpec((1, 1, SEQ, LANES), lambda b, i, p: (b, layer, 0, p))

    pattern = lambda i: jnp.where(i == 0, 0, jnp.where(i == NA_NBLK - 1, 2, 1))
    ck = cache_k.reshape(DEC_BATCH, DEPTH, SEQ, N_PAIRS * LANES)
    cv = cache_v.reshape(DEC_BATCH, DEPTH, SEQ, N_PAIRS * LANES)
    return pl.pallas_call(
        _na_kernel,
        name="nbr_attn",
        grid=(DEC_BATCH, NA_NBLK, N_PAIRS),
        in_specs=[qspec()] + [kvspec(24, j) for j in range(3)] + [kvspec(28, j) for j in range(3)]
                 + [cspec(), cspec(),
                    pl.BlockSpec((1, 2, NA_QBLK, NA_KROWS * GRID_W), lambda b, i, p: (pattern(i), p, 0, 0))],
        out_specs=pl.BlockSpec((NA_QBLK, LANES), lambda b, i, p: (b * NA_NBLK + i, p)),
        out_shape=jax.ShapeDtypeStruct((NS, N_PAIRS * LANES), BF16),
        compiler_params=_params(("arbitrary", "arbitrary", "arbitrary")),
    )(z, z, z, z, z, z, z, ck, cv, bias)


def _merge_kernel(rp_ref, rs_ref, pp_ref, ps_ref, ap_ref, as_ref, xp_ref, xs_ref, g0_ref, g1_ref, g2_ref,
                  wr_ref, wp_ref, wa_ref, wo_ref, m_ref, ln_ref, o_ref, wrb, wpb, wab, wob):
    @pl.when(pl.program_id(0) == 0)
    def _():
        wrb[...] = wr_ref[...].astype(BF16)
        wpb[...] = wp_ref[...].astype(BF16)
        wab[...] = wa_ref[...].astype(BF16)
        wob[...] = wo_ref[...].astype(BF16)

    branch = lambda p_ref, s_ref, w: jnp.dot(_pick(p_ref, s_ref), w[...], preferred_element_type=F32)
    merged = (jax.nn.sigmoid(g0_ref[...]) * branch(rp_ref, rs_ref, wrb)
              + jax.nn.sigmoid(g1_ref[...]) * branch(pp_ref, ps_ref, wpb)
              + jax.nn.sigmoid(g2_ref[...]) * branch(ap_ref, as_ref, wab))
    mix = jnp.dot(merged.astype(BF16), wob[...], preferred_element_type=F32)
    o_ref[...] = _pick(xp_ref, xs_ref) + m_ref[0, 2:3, :] * (_rms(mix) * ln_ref[...])


def _merge(ret_pair, pool_pair, na_pair, x_pair, z, w_ret_o, w_pool_o, w_na_o, w_o, mod_l, ln):
    tm = 512
    half = N_PAIRS * LANES
    row = lambda i: (i, 0)
    const = lambda i: (0, 0)
    return pl.pallas_call(
        _merge_kernel,
        name="merge",
        grid=(NT // tm,),
        in_specs=_pair_specs(tm, half) * 3 + _pair_specs(tm, D)
                 + [pl.BlockSpec((tm, D), lambda i, c=c: (i, 4 + c)) for c in range(3)]
                 + [pl.BlockSpec((half, D), const)] * 3
                 + [pl.BlockSpec((D, D), const),
                    pl.BlockSpec((1, N_MOD, D), lambda i: (_mod_row(i * tm), 0, 0)),
                    pl.BlockSpec((1, D), const)],
        out_specs=pl.BlockSpec((tm, D), row),
        out_shape=jax.ShapeDtypeStruct((NT, D), F32),
        scratch_shapes=[pltpu.VMEM((half, D), BF16)] * 3 + [pltpu.VMEM((D, D), BF16)],
        compiler_params=_params(("arbitrary",), VMEM_LIMIT),
    )(*ret_pair, *pool_pair, *na_pair, *x_pair, z, z, z, w_ret_o, w_pool_o, w_na_o, w_o, mod_l, ln.reshape(1, D))


def _ffnprep_kernel(x_ref, m_ref, ln_ref, wr_ref, h_ref, aff_ref):
    y = _rms(x_ref[...]) * ln_ref[...]
    h = y * (1.0 + m_ref[0, 4:5, :]) + m_ref[0, 3:4, :]
    hb = h.astype(BF16)
    h_ref[...] = hb
    hl = (h - hb.astype(F32)).astype(BF16)
    w = wr_ref[...]
    wb = w.astype(BF16)
    wl = (w - wb.astype(F32)).astype(BF16)
    logits = (lax.dot_general(wb, hb, NT_DIMS, preferred_element_type=F32)
              + lax.dot_general(wb, hl, NT_DIMS, preferred_element_type=F32)
              + lax.dot_general(wl, hb, NT_DIMS, preferred_element_type=F32))
    e = jnp.exp(logits - jnp.max(logits, axis=0, keepdims=True))
    aff = e / jnp.sum(e, axis=0, keepdims=True)
    for j in range(aff_ref.shape[0]):
        aff_ref[j] = aff[:, j * LANES:(j + 1) * LANES]


def _ffnprep(x_all, mod_l, ln, w_router_t):
    tm = 512
    return pl.pallas_call(
        _ffnprep_kernel,
        name="ffn_prep",
        grid=(NT // tm,),
        in_specs=[pl.BlockSpec((tm, D), lambda i: (i, 0)),
                  pl.BlockSpec((1, N_MOD, D), lambda i: (_mod_row(i * tm), 0, 0)),
                  pl.BlockSpec((1, D), lambda i: (0, 0)),
                  pl.BlockSpec((N_EXPERTS, D), lambda i: (0, 0))],
        out_specs=[pl.BlockSpec((tm, D), lambda i: (i, 0)),
                   pl.BlockSpec((tm // LANES, N_EXPERTS, LANES), lambda i: (i, 0, 0))],
        out_shape=[jax.ShapeDtypeStruct((NT, D), BF16),
                   jax.ShapeDtypeStruct((NT // LANES, N_EXPERTS, LANES), F32)],
        compiler_params=_params(("arbitrary",)),
    )(x_all, mod_l, ln.reshape(1, D), w_router_t)


def _route_kernel(aff_ref, slot_ref, offs_ref, ceq_ref, csel_ref, *, cap, nblk):
    def count_ge(v):
        bits = lax.bitcast_convert_type(aff_ref[...], I32)
        c = jnp.sum(jnp.where(bits >= v[None], 1.0, 0.0), axis=0)
        return jnp.sum(c, axis=1, keepdims=True)

    def search(_, lohi):
        lo, hi = lohi
        mid = lo + ((hi - lo + 1) >> 1)
        ok = count_ge(mid) >= cap
        return jnp.where(ok, mid, lo), jnp.where(ok, hi, mid - 1)

    lo0 = jnp.zeros((N_EXPERTS, 1), I32)
    hi0 = jnp.full((N_EXPERTS, 1), 0x7F800000, I32)
    thr, _ = lax.fori_loop(0, 31, search, (lo0, hi0))
    need = cap - (count_ge(thr + 1))

    upper = (lax.broadcasted_iota(I32, (LANES, LANES), 0)
             < lax.broadcasted_iota(I32, (LANES, LANES), 1)).astype(BF16)
    lane = lax.broadcasted_iota(I32, (N_EXPERTS, LANES), 1)

    ceq_ref[...] = jnp.zeros_like(ceq_ref)
    csel_ref[...] = jnp.zeros_like(csel_ref)
    offs_ref[...] = jnp.zeros_like(offs_ref)

    def block(b, carry):
        c_eq = ceq_ref[...]
        c_sel = csel_ref[...]
        bits = lax.bitcast_convert_type(aff_ref[b], I32)
        eq = bits == thr
        eqf = jnp.where(eq, 1.0, 0.0)
        eq_rank = jnp.dot(eqf.astype(BF16), upper, preferred_element_type=F32) + c_eq
        sel = (bits > thr) | (eq & (eq_rank < need))
        self_ = jnp.where(sel, 1.0, 0.0)
        rank = jnp.dot(self_.astype(BF16), upper, preferred_element_type=F32) + c_sel
        slot_ref[b] = jnp.where(sel, rank, -1.0)
        offs_ref[...] = jnp.where(lane == b, c_sel.astype(I32), offs_ref[...])
        ceq_ref[...] = c_eq + jnp.sum(eqf, axis=1, keepdims=True)
        csel_ref[...] = c_sel + jnp.sum(self_, axis=1, keepdims=True)
        return carry

    lax.fori_loop(0, nblk, block, 0)
    offs_ref[...] = jnp.where(lane >= nblk, csel_ref[...].astype(I32), offs_ref[...])


def _route(aff_blocks, *, blk0, nblk, cap):
    return pl.pallas_call(
        functools.partial(_route_kernel, cap=cap, nblk=nblk),
        name="route",
        grid=(1,),
        in_specs=[pl.BlockSpec((nblk, N_EXPERTS, LANES), lambda i: (blk0 // nblk, 0, 0))],
        out_specs=[pl.BlockSpec((nblk, N_EXPERTS, LANES), lambda i: (0, 0, 0)),
                   pl.BlockSpec((N_EXPERTS, LANES), lambda i: (0, 0))],
        out_shape=[jax.ShapeDtypeStruct((nblk, N_EXPERTS, LANES), F32),
                   jax.ShapeDtypeStruct((N_EXPERTS, LANES), I32)],
        scratch_shapes=[pltpu.VMEM((N_EXPERTS, LANES), F32)] * 2,
        compiler_params=_params(("arbitrary",)),
    )(aff_blocks)


GATHER_TOKENS = 256
TILE_BLOCKS = GATHER_TOKENS // LANES
SLOT_ALIGN = 16
ROUND_SLOTS = 64
ROUND_SHIFT = ROUND_SLOTS.bit_length() - 1
WINDOW = 128
GATHER_GROUP = 4


def _round_bounds(offs_ref, e, t, r):
    off0 = offs_ref[e, TILE_BLOCKS * t]
    off1 = offs_ref[e, TILE_BLOCKS * t + TILE_BLOCKS]
    lo = jnp.minimum(off0 + ROUND_SLOTS * r, off1)
    hi = jnp.minimum(lo + ROUND_SLOTS, off1)
    return lo, hi, pl.multiple_of(lo & -SLOT_ALIGN, SLOT_ALIGN)


def _n_rounds(offs_ref, t):
    rounds = jnp.int32(0)
    for e in range(N_EXPERTS):
        cnt = offs_ref[e, TILE_BLOCKS * t + TILE_BLOCKS] - offs_ref[e, TILE_BLOCKS * t]
        rounds = jnp.maximum(rounds, (cnt + (ROUND_SLOTS - 1)) >> ROUND_SHIFT)
    return rounds


def _gather_kernel(offs_ref, slot_ref, h_ref, xe_hbm, stage_ref, carry_ref, sem, nround_ref, *, n_tiles):
    t = pl.program_id(0)
    cap = xe_hbm.shape[1] - WINDOW

    def out_copy(buf, e, start):
        return pltpu.make_async_copy(stage_ref.at[buf, e], xe_hbm.at[e, pl.ds(start, WINDOW)], sem.at[buf])

    def wait_round(buf):
        for e in range(N_EXPERTS):
            out_copy(buf, e, 0).wait()

    @pl.when(t == 0)
    def _():
        carry_ref[...] = jnp.zeros_like(carry_ref)
        nround_ref[0] = 0
        stage_ref[0, 0] = jnp.zeros((WINDOW, D), BF16)
        for e in range(N_EXPERTS):
            pltpu.make_async_copy(stage_ref.at[0, 0], xe_hbm.at[e, pl.ds(cap, WINDOW)], sem.at[0]).start()
        wait_round(0)

    hb = h_ref[...]
    sub = lax.broadcasted_iota(I32, (WINDOW, GATHER_TOKENS), 0).astype(F32)

    def one_round(r, carry):
        done = nround_ref[0]
        buf = done & 1
        bounds = [_round_bounds(offs_ref, e, t, r) for e in range(N_EXPERTS)]
        for e0 in range(0, N_EXPERTS, GATHER_GROUP):
            group = range(e0, e0 + GATHER_GROUP)
            onehots = []
            for e in group:
                lo, hi, start = bounds[e]
                srow = jnp.concatenate([slot_ref[j, e:e + 1, :] for j in range(TILE_BLOCKS)], axis=1)
                hit = ((srow - start.astype(F32) == sub) & (srow >= lo.astype(F32)) & (srow < hi.astype(F32)))
                onehots.append(jnp.where(hit, 1.0, 0.0).astype(BF16))
            rows = jnp.dot(jnp.concatenate(onehots, axis=0), hb, preferred_element_type=F32)
            for i, e in enumerate(group):
                lo, hi, start = bounds[e]
                piece = rows[i * WINDOW:(i + 1) * WINDOW]
                head = piece[:SLOT_ALIGN] + carry_ref[e].astype(F32)
                stage_ref[buf, e, :SLOT_ALIGN, :] = head.astype(BF16)
                stage_ref[buf, e, SLOT_ALIGN:, :] = piece[SLOT_ALIGN:].astype(BF16)
                tail = pl.multiple_of((hi & -SLOT_ALIGN) - start, SLOT_ALIGN)
                carry_ref[e] = stage_ref[buf, e, pl.ds(tail, SLOT_ALIGN), :]

        @pl.when(done > 0)
        def _():
            wait_round(1 - buf)

        for e in range(N_EXPERTS):
            out_copy(buf, e, bounds[e][2]).start()
        nround_ref[0] = done + 1
        return carry

    lax.fori_loop(0, _n_rounds(offs_ref, t), one_round, 0)

    @pl.when((t == n_tiles - 1) & (nround_ref[0] > 0))
    def _():
        wait_round((nround_ref[0] - 1) & 1)


def _gather(offs, slot, h_all, *, row0, n, cap):
    n_tiles = n // GATHER_TOKENS
    tile0 = row0 // GATHER_TOKENS
    return pl.pallas_call(
        functools.partial(_gather_kernel, n_tiles=n_tiles),
        name="gather",
        grid_spec=pltpu.PrefetchScalarGridSpec(
            num_scalar_prefetch=1,
            grid=(n_tiles,),
            in_specs=[pl.BlockSpec((TILE_BLOCKS, N_EXPERTS, LANES), lambda t, o: (t, 0, 0)),
                      pl.BlockSpec((GATHER_TOKENS, D), lambda t, o: (tile0 + t, 0))],
            out_specs=pl.BlockSpec(memory_space=pl.ANY),
            scratch_shapes=[pltpu.VMEM((2, N_EXPERTS, WINDOW, D), BF16),
                            pltpu.VMEM((N_EXPERTS, SLOT_ALIGN, D), BF16),
                            pltpu.SemaphoreType.DMA((2,)),
                            pltpu.SMEM((1,), I32)]),
        out_shape=jax.ShapeDtypeStruct((N_EXPERTS, cap + WINDOW, D), BF16),
        compiler_params=_params(("arbitrary",), VMEM_LIMIT),
    )(offs, slot, h_all)


FF_CHUNK = 512


def _experts_kernel(xp_ref, xs_ref, wg_ref, wu_ref, wd_ref, yp_ref, ys_ref, accp_ref, accs_ref, *, n_f):
    f = pl.program_id(1)
    wg = wg_ref[0].astype(BF16)
    wu = wu_ref[0].astype(BF16)
    wd = wd_ref[0].astype(BF16)

    def ffn(x):
        a = jnp.dot(x, wg, preferred_element_type=F32)
        b = jnp.dot(x, wu, preferred_element_type=F32)
        return jnp.dot((_silu(a) * b).astype(BF16), wd, preferred_element_type=F32)

    yp = ffn(xp_ref[0])
    ys = ffn(xs_ref[0])

    @pl.when(f == 0)
    def _():
        accp_ref[...] = yp
        accs_ref[...] = ys

    @pl.when(f > 0)
    def _():
        accp_ref[...] += yp
        accs_ref[...] += ys

    @pl.when(f == n_f - 1)
    def _():
        for y_ref, acc_ref in ((yp_ref, accp_ref), (ys_ref, accs_ref)):
            cap = acc_ref.shape[0]
            y_ref[0, :cap, :] = acc_ref[...].astype(BF16)
            y_ref[0, cap:, :] = jnp.zeros((WINDOW, D), BF16)


def _experts(xe_p, xe_s, w_gate, w_up, w_down, layer):
    n_f = EXPERT_FF // FF_CHUNK
    cap_p, cap_s = xe_p.shape[1] - WINDOW, xe_s.shape[1] - WINDOW
    spec = lambda rows: pl.BlockSpec((1, rows, D), lambda e, f: (e, 0, 0))
    return pl.pallas_call(
        functools.partial(_experts_kernel, n_f=n_f),
        name="experts",
        grid=(N_EXPERTS, n_f),
        in_specs=[spec(cap_p), spec(cap_s),
                  pl.BlockSpec((None, 1, D, FF_CHUNK), lambda e, f: (layer, e, 0, f)),
                  pl.BlockSpec((None, 1, D, FF_CHUNK), lambda e, f: (layer, e, 0, f)),
                  pl.BlockSpec((None, 1, FF_CHUNK, D), lambda e, f: (layer, e, f, 0))],
        out_specs=[spec(cap_p + WINDOW), spec(cap_s + WINDOW)],
        out_shape=[jax.ShapeDtypeStruct(xe_p.shape, BF16), jax.ShapeDtypeStruct(xe_s.shape, BF16)],
        scratch_shapes=[pltpu.VMEM((cap_p, D), F32), pltpu.VMEM((cap_s, D), F32)],
        compiler_params=_params(("arbitrary", "arbitrary"), VMEM_LIMIT),
    )(xe_p, xe_s, w_gate, w_up, w_down)


def _combine_kernel(offs_ref, slot_ref, aff_ref, x_ref, m_ref, ln_ref, ye_hbm, o_ref, stage_ref, sem, *, n_tiles):
    t = pl.program_id(0)
    buf = t & 1

    def in_copy(b, e, start):
        return pltpu.make_async_copy(ye_hbm.at[e, pl.ds(start, WINDOW)],
                                     stage_ref.at[b, pl.ds(e * WINDOW, WINDOW)], sem.at[b])

    def start_round(b, tile, r):
        for e in range(N_EXPERTS):
            in_copy(b, e, _round_bounds(offs_ref, e, tile, r)[2]).start()

    def wait_round(b):
        for e in range(N_EXPERTS):
            in_copy(b, e, 0).wait()

    @pl.when(t == 0)
    def _():
        start_round(0, 0, 0)

    wait_round(buf)

    @pl.when(t + 1 < n_tiles)
    def _():
        start_round(1 - buf, t + 1, 0)

    slot = slot_ref[...]
    gate = aff_ref[...]
    gate_hi = gate.astype(BF16).astype(F32)
    gate_lo = gate - gate_hi
    lane = lax.broadcasted_iota(I32, (GATHER_TOKENS, WINDOW), 1).astype(F32)

    def weighted_rows(b, r):
        his, los = [], []
        for e in range(N_EXPERTS):
            lo, hi, start = _round_bounds(offs_ref, e, t, r)
            s = slot[:, e:e + 1]
            hit = (s - start.astype(F32) == lane) & (s >= lo.astype(F32)) & (s < hi.astype(F32))
            his.append(jnp.where(hit, gate_hi[:, e:e + 1], 0.0).astype(BF16))
            los.append(jnp.where(hit, gate_lo[:, e:e + 1], 0.0).astype(BF16))
        rows = stage_ref[b]
        return (jnp.dot(jnp.concatenate(his, axis=1), rows, preferred_element_type=F32)
                + jnp.dot(jnp.concatenate(los, axis=1), rows, preferred_element_type=F32))

    o_ref[...] = weighted_rows(buf, 0)

    def extra_round(r, carry):
        start_round(buf, t, r)
        wait_round(buf)
        o_ref[...] += weighted_rows(buf, r)
        return carry

    lax.fori_loop(1, _n_rounds(offs_ref, t), extra_round, 0)
    o_ref[...] = x_ref[...] + m_ref[0, 5:6, :] * (_rms(o_ref[...]) * ln_ref[...])


def _combine(offs, ye, slot_cols, aff_cols, x_all, mod_l, ln, *, row0, n):
    n_tiles = n // GATHER_TOKENS
    tile0 = row0 // GATHER_TOKENS
    return pl.pallas_call(
        functools.partial(_combine_kernel, n_tiles=n_tiles),
        name="combine",
        grid_spec=pltpu.PrefetchScalarGridSpec(
            num_scalar_prefetch=1,
            grid=(n_tiles,),
            in_specs=[pl.BlockSpec((GATHER_TOKENS, N_EXPERTS), lambda t, o: (t, 0)),
                      pl.BlockSpec((GATHER_TOKENS, N_EXPERTS), lambda t, o: (t, 0)),
                      pl.BlockSpec((GATHER_TOKENS, D), lambda t, o: (tile0 + t, 0)),
                      pl.BlockSpec((1, N_MOD, D), lambda t, o: (_mod_row(row0 + t * GATHER_TOKENS), 0, 0)),
                      pl.BlockSpec((1, D), lambda t, o: (0, 0)),
                      pl.BlockSpec(memory_space=pl.ANY)],
            out_specs=pl.BlockSpec((GATHER_TOKENS, D), lambda t, o: (t, 0)),
            scratch_shapes=[pltpu.VMEM((2, N_EXPERTS * WINDOW, D), BF16),
                            pltpu.SemaphoreType.DMA((2,))]),
        out_shape=jax.ShapeDtypeStruct((n, D), F32),
        compiler_params=_params(("arbitrary",), VMEM_LIMIT),
    )(offs, slot_cols, aff_cols, x_all, mod_l, ln.reshape(1, D), ye)


def _token_major(blocks):
    return blocks.transpose(0, 2, 1).reshape(-1, N_EXPERTS)


def kernel(x_prompt, x_sample, cache_k, cache_v, state_ret, c, c_ctx, w_mod, b_mod, ln_pre_mix, ln_post_mix,
           ln_pre_ffn, ln_post_ffn, w_in, ret_decay, pool_w, pool_scale, na_rpb, w_ret_o, w_pool_o, w_na_o, w_o,
           w_router, w_gate, w_up, w_down):
    cvecs = jnp.zeros((8, D), F32).at[0].set(c_ctx).at[1:1 + DEC_BATCH].set(c)
    mod = _modulation(cvecs, w_mod, b_mod)
    rope = _rope_tables()
    x_pair = (x_prompt.reshape(NP, D), x_sample.reshape(NS, D))
    new_k, new_v, new_s = [], [], []
    groups = ((0, NP, NP // N_EXPERTS * 2), (NP, NS, NS // N_EXPERTS * 2))

    for l in range(DEPTH):
        mod_l = mod[l]
        h_all = _prenorm(x_pair, mod_l, ln_pre_mix[l])
        z = _in_proj(h_all, w_in[l])
        log_g = jax.nn.log_sigmoid(ret_decay[l].astype(F32))

        ret_p, st = _retention(z, log_g, nb=BATCH, seq=SEQ, row_block0=0, emit_state=True)
        (ret_s,) = _retention(z, log_g, nb=DEC_BATCH, seq=DEC_SEQ, row_block0=NP // DEC_SEQ,
                              rope=rope, s0=state_ret, layer=l)
        pool_p = _pool(z, pool_w[l], pool_scale[l], nb=BATCH, seq=SEQ, row_block0=0)
        pool_s = _pool(z, pool_w[l], pool_scale[l], nb=DEC_BATCH, seq=DEC_SEQ, row_block0=NP // DEC_SEQ)
        na_p = _ctx_attention(z)
        na_s = _neighbourhood_attention(z, _na_bias_tables(na_rpb[l]), cache_k, cache_v, l)
        x_mid = _merge((ret_p, ret_s), (pool_p, pool_s), (na_p, na_s), x_pair, z,
                       w_ret_o[l], w_pool_o[l], w_na_o[l], w_o[l], mod_l, ln_post_mix[l])
        new_k.append(z[:NP, 3072:3584].reshape(BATCH, SEQ, 8, HEAD_DIM))
        new_v.append(z[:NP, 3584:4096].reshape(BATCH, SEQ, 8, HEAD_DIM))
        new_s.append(st)

        h2, aff = _ffnprep(x_mid, mod_l, ln_pre_ffn[l], w_router[l].T)
        routed = []
        for row0, n, cap in groups:
            slot, offs = _route(aff, blk0=row0 // LANES, nblk=n // LANES, cap=cap)
            routed.append((slot, offs, _gather(offs, slot, h2, row0=row0, n=n, cap=cap)))
        ye = _experts(routed[0][2], routed[1][2], w_gate, w_up, w_down, l)
        outs = []
        for (row0, n, cap), (slot, offs, _), y in zip(groups, routed, ye):
            aff_cols = _token_major(aff[row0 // LANES:(row0 + n) // LANES])
            outs.append(_combine(offs, y, _token_major(slot), aff_cols, x_mid, mod_l, ln_post_ffn[l],
                                 row0=row0, n=n))
        x_pair = tuple(outs)

    y_prompt = x_pair[0].reshape(BATCH, SEQ, D)
    y_sample = x_pair[1].reshape(DEC_BATCH, DEC_SEQ, D)
    return (y_prompt, y_sample, jnp.stack(new_k, axis=1), jnp.stack(new_v, axis=1), jnp.stack(new_s, axis=1))
```

```python
import functools

import numpy as np
import jax
import jax.numpy as jnp
from jax import lax
from jax.experimental import pallas as pl
from jax.experimental.pallas import tpu as pltpu

F32 = jnp.float32
BF16 = jnp.bfloat16
I32 = jnp.int32

D = 1024
BATCH, SEQ = 32, 256
DEC_BATCH, DEC_SEQ = 2, 2048
DEPTH = 2
NP = BATCH * SEQ
NS = DEC_BATCH * DEC_SEQ
NT = NP + NS
GRID_W = 64
N_MOD = 6
EPS = 1e-6
ROPE_BASE = 10000.0
HEAD_DIM = 64
N_PAIRS = 4
CHUNK = 128
POOL_WINDOWS = (2, 4, 8, 16)
POOL_PAD = 16
NA_KH, NA_KW = 8, 16
N_EXPERTS = 16
EXPERT_FF = 2048
IN_COLS = 7168
LANES = 128
VMEM_LIMIT = 56 * 1024 * 1024

NT_DIMS = (((1,), (1,)), ((), ()))


def _params(sem, vmem=None):
    return pltpu.CompilerParams(dimension_semantics=sem, vmem_limit_bytes=vmem)


def _mod_row(row_start):
    return jnp.where(row_start < NP, 0, 1 + (row_start - NP) // DEC_SEQ)


def _silu(x):
    return x * jax.nn.sigmoid(x)


def _rms(x):
    return x * lax.rsqrt(jnp.mean(x * x, axis=-1, keepdims=True) + EPS)


def _pair_specs(tm, width):
    n_p = NP // tm
    return [pl.BlockSpec((tm, width), lambda i: (jnp.minimum(i, n_p - 1), 0)),
            pl.BlockSpec((tm, width), lambda i: (jnp.maximum(i - n_p, 0), 0))]


def _pick(p_ref, s_ref):
    return jnp.where(pl.program_id(0) < NP // p_ref.shape[0], p_ref[...], s_ref[...])


def _mod_kernel(c_ref, w_ref, b_ref, o_ref):
    a = _silu(c_ref[...]).astype(BF16)
    o_ref[0] = jnp.dot(a, w_ref[0].astype(BF16), preferred_element_type=F32) + b_ref[0]


def _modulation(cvecs, w_mod, b_mod):
    out = pl.pallas_call(
        _mod_kernel,
        name="modulation",
        grid=(DEPTH, N_MOD),
        in_specs=[pl.BlockSpec((8, D), lambda l, j: (0, 0)),
                  pl.BlockSpec((1, D, D), lambda l, j: (l, 0, j)),
                  pl.BlockSpec((1, 1, D), lambda l, j: (l, 0, j))],
        out_specs=pl.BlockSpec((1, 8, D), lambda l, j: (l, 0, j)),
        out_shape=jax.ShapeDtypeStruct((DEPTH, 8, N_MOD * D), F32),
        compiler_params=_params(("arbitrary", "arbitrary")),
    )(cvecs, w_mod, b_mod.reshape(DEPTH, 1, N_MOD * D))
    return out.reshape(DEPTH, 8, N_MOD, D)


def _prenorm_kernel(xp_ref, xs_ref, m_ref, ln_ref, h_ref):
    y = _rms(_pick(xp_ref, xs_ref)) * ln_ref[...]
    h_ref[...] = (y * (1.0 + m_ref[0, 1:2, :]) + m_ref[0, 0:1, :]).astype(BF16)


def _prenorm(x_pair, mod_l, ln):
    tm = 512
    return pl.pallas_call(
        _prenorm_kernel,
        name="prenorm",
        grid=(NT // tm,),
        in_specs=_pair_specs(tm, D)
                 + [pl.BlockSpec((1, N_MOD, D), lambda i: (_mod_row(i * tm), 0, 0)),
                  pl.BlockSpec((1, D), lambda i: (0, 0))],
        out_specs=pl.BlockSpec((tm, D), lambda i: (i, 0)),
        out_shape=jax.ShapeDtypeStruct((NT, D), BF16),
        compiler_params=_params(("arbitrary",)),
    )(*x_pair, mod_l, ln.reshape(1, D))


def _mm_kernel(a_ref, w_ref, o_ref, wb_ref):
    @pl.when(pl.program_id(1) == 0)
    def _():
        wb_ref[...] = w_ref[...].astype(BF16)

    o_ref[...] = jnp.dot(a_ref[...], wb_ref[...], preferred_element_type=F32)


def _kv_kernel(h0_ref, h1_ref, w_ref, k_ref, v_ref, wb_ref):
    @pl.when(pl.program_id(1) == 0)
    def _():
        wb_ref[...] = w_ref[0].astype(BF16)

    h = jnp.where(pl.program_id(0) == 0, h0_ref[...], h1_ref[...])
    kv = jnp.dot(h, wb_ref[...], preferred_element_type=F32)
    half = N_PAIRS * LANES
    k_ref[...] = kv[:, :half].reshape(k_ref.shape)
    v_ref[...] = kv[:, half:].reshape(v_ref.shape)


def _kv_proj(h_layers, w_in):
    per = 4
    kv_col = 3072 // D
    hspec = pl.BlockSpec((per * SEQ, D), lambda l, i: (i, 0))
    ospec = pl.BlockSpec((per, 1, SEQ, N_PAIRS * LANES), lambda l, i: (i, l, 0, 0))
    shape = jax.ShapeDtypeStruct((BATCH, DEPTH, SEQ, N_PAIRS * LANES), F32)
    k, v = pl.pallas_call(
        _kv_kernel,
        name="kv_proj",
        grid=(DEPTH, BATCH // per),
        in_specs=[hspec, hspec, pl.BlockSpec((1, D, D), lambda l, i: (l, 0, kv_col))],
        out_specs=[ospec, ospec],
        out_shape=[shape, shape],
        scratch_shapes=[pltpu.VMEM((D, D), BF16)],
        compiler_params=_params(("arbitrary", "arbitrary"), VMEM_LIMIT),
    )(*h_layers, w_in)
    cache_shape = (BATCH, DEPTH, SEQ, 2 * N_PAIRS, HEAD_DIM)
    return k.reshape(cache_shape), v.reshape(cache_shape)


def _in_proj(h_all, w_in, layer):
    tm, tn = 512, 1792
    return pl.pallas_call(
        _mm_kernel,
        name="in_proj",
        grid=(IN_COLS // tn, NT // tm),
        in_specs=[pl.BlockSpec((tm, D), lambda j, i: (i, 0)),
                  pl.BlockSpec((None, D, tn), lambda j, i: (layer, 0, j))],
        out_specs=pl.BlockSpec((tm, tn), lambda j, i: (i, j)),
        out_shape=jax.ShapeDtypeStruct((NT, IN_COLS), F32),
        scratch_shapes=[pltpu.VMEM((D, tn), BF16)],
        compiler_params=_params(("arbitrary", "arbitrary"), VMEM_LIMIT),
    )(h_all, w_in)


def _swap16(x):
    lane = lax.broadcasted_iota(I32, x.shape, 1)
    return jnp.where((lane // 16) % 2 == 0, pltpu.roll(x, LANES - 16, 1), pltpu.roll(x, 16, 1))


def _block_diag(top, bottom):
    z = jnp.zeros((HEAD_DIM, HEAD_DIM), F32)
    return jnp.concatenate([jnp.concatenate([top, z], axis=1),
                            jnp.concatenate([z, bottom], axis=1)], axis=0)


def _retention_kernel(*refs, n_chunks, rotate, has_s0, emit_state, pairs_per_step):
    refs = list(refs)
    lg_ref, q_ref, k_ref, v_ref, g_ref = refs[:5]
    pos = 5
    cos_ref = sin_ref = s0_ref = st_ref = None
    if rotate:
        cos_ref, sin_ref = refs[pos:pos + 2]
        pos += 2
    if has_s0:
        s0_ref = refs[pos]
        pos += 1
    o_ref = refs[pos]
    pos += 1
    if emit_state:
        st_ref = refs[pos]
        pos += 1
    sf_scr, sb_scr = refs[pos:pos + 2]
    for pp in range(pairs_per_step):
        _retention_pair(lg_ref, q_ref, k_ref, v_ref, g_ref, cos_ref, sin_ref, s0_ref, o_ref, st_ref, sf_scr, sb_scr,
                        pp, pl.program_id(1) * pairs_per_step + pp, n_chunks)


def _retention_pair(lg_ref, q_ref, k_ref, v_ref, g_ref, cos_ref, sin_ref, s0_ref, o_ref, st_ref, sf_scr, sb_scr,
                    pp, pair, n_chunks):
    rotate, has_s0, emit_state = cos_ref is not None, s0_ref is not None, st_ref is not None
    lanes = slice(pp * LANES, (pp + 1) * LANES)
    lane1 = lax.broadcasted_iota(I32, (1, LANES), 1)
    lo1 = lane1 < HEAD_DIM
    lgf = jnp.where(lo1, lg_ref[0, 2 * pair], lg_ref[0, 2 * pair + 1])
    lgb = jnp.where(lo1, lg_ref[1, 2 * pair], lg_ref[1, 2 * pair + 1])
    lg_heads = [(lg_ref[0, 2 * pair], lg_ref[1, 2 * pair]),
                (lg_ref[0, 2 * pair + 1], lg_ref[1, 2 * pair + 1])]

    row = lax.broadcasted_iota(I32, (CHUNK, CHUNK), 0)
    col = lax.broadcasted_iota(I32, (CHUNK, CHUNK), 1)
    rel = (row - col).astype(F32)
    lo_mask = col < HEAD_DIM
    blockdiag = (row < HEAD_DIM) == (col < HEAD_DIM)
    posf = row.astype(F32)
    dmat = []
    for hf, hb in lg_heads:
        dmat.append(jnp.where(rel >= 0, jnp.exp(jnp.where(rel >= 0, rel, 0.0) * hf), 0.0)
                    + jnp.where(rel <= 0, jnp.exp(jnp.where(rel <= 0, -rel, 0.0) * hb), 0.0))
    qdec_f = jnp.exp((posf + 1.0) * lgf)
    kdec_f = jnp.exp((CHUNK - 1.0 - posf) * lgf)
    qdec_b = jnp.exp((CHUNK - posf) * lgb)
    kdec_b = jnp.exp(posf * lgb)
    sdec_f = jnp.exp(CHUNK * lgf)
    sdec_b = jnp.exp(CHUNK * lgb)

    def load(c):
        rows = pl.ds(c * CHUNK, CHUNK)
        q = q_ref[rows, lanes]
        k = k_ref[rows, lanes]
        if rotate:
            cs, sn = cos_ref[rows, :], sin_ref[rows, :]
            q = q * cs + _swap16(q) * sn
            k = k * cs + _swap16(k) * sn
        return q, k * (HEAD_DIM ** -0.5), v_ref[rows, lanes]

    def state_update(s, k, v, kdec, sdec):
        kd = (k * kdec).T.astype(BF16)
        u = jnp.dot(kd, v.astype(BF16), preferred_element_type=F32)
        return s * sdec + jnp.where(blockdiag, u, 0.0)

    h0, h1 = 2 * pp, 2 * pp + 1
    if has_s0:
        s_f = _block_diag(s0_ref[0, 0, 0, h0], s0_ref[0, 0, 0, h1])
        s_b = _block_diag(s0_ref[0, 0, 1, h0], s0_ref[0, 0, 1, h1])
    else:
        s_f = jnp.zeros((CHUNK, CHUNK), F32)
        s_b = jnp.zeros((CHUNK, CHUNK), F32)

    scr0 = pp * n_chunks
    for c in range(n_chunks):
        sf_scr[scr0 + c] = s_f
        _, k, v = load(c)
        s_f = state_update(s_f, k, v, kdec_f, sdec_f)
    for c in reversed(range(n_chunks)):
        sb_scr[scr0 + c] = s_b
        _, k, v = load(c)
        s_b = state_update(s_b, k, v, kdec_b, sdec_b)

    if emit_state:
        st_ref[0, 0, h0] = s_f[:HEAD_DIM, :HEAD_DIM]
        st_ref[0, 0, h1] = s_f[HEAD_DIM:, HEAD_DIM:]
        st_ref[0, 1, h0] = s_b[:HEAD_DIM, :HEAD_DIM]
        st_ref[0, 1, h1] = s_b[HEAD_DIM:, HEAD_DIM:]

    for c in range(n_chunks):
        q, k, v = load(c)
        qb, kb, vb = q.astype(BF16), k.astype(BF16), v.astype(BF16)
        outs = []
        for h in range(2):
            qh = jnp.where(lo_mask if h == 0 else ~lo_mask, qb, jnp.zeros_like(qb))
            a = lax.dot_general(qh, kb, NT_DIMS, preferred_element_type=F32) * dmat[h]
            outs.append(jnp.dot(a.astype(BF16), vb, preferred_element_type=F32))
        o = jnp.where(lo_mask, outs[0], outs[1])
        o = o + jnp.dot(qb, sf_scr[scr0 + c].astype(BF16), preferred_element_type=F32) * qdec_f
        o = o + jnp.dot(qb, sb_scr[scr0 + c].astype(BF16), preferred_element_type=F32) * qdec_b
        o2 = o * o
        ms0 = jnp.sum(jnp.where(lo_mask, o2, 0.0), axis=1, keepdims=True) * (1.0 / HEAD_DIM)
        ms1 = jnp.sum(jnp.where(lo_mask, 0.0, o2), axis=1, keepdims=True) * (1.0 / HEAD_DIM)
        inv = jnp.where(lo_mask, lax.rsqrt(ms0 + EPS), lax.rsqrt(ms1 + EPS))
        g = g_ref[pl.ds(c * CHUNK, CHUNK), lanes]
        o_ref[pl.ds(c * CHUNK, CHUNK), lanes] = (_silu(g) * (o * inv)).astype(BF16)


def _retention(z, log_g, *, nb, seq, row_block0, pairs_per_step, rope=None, s0=None, layer=0, emit_state=False):
    n_chunks = seq // CHUNK
    rotate = rope is not None
    pps = pairs_per_step
    width = pps * LANES

    def zspec(cb):
        return pl.BlockSpec((seq, width), lambda b, p: (row_block0 + b, cb // pps + p))

    in_specs = [pl.BlockSpec(memory_space=pltpu.SMEM), zspec(0), zspec(4), zspec(8), zspec(12)]
    args = [log_g, z, z, z, z]
    if rotate:
        in_specs += [pl.BlockSpec((seq, LANES), lambda b, p: (0, 0))] * 2
        args += list(rope)
    if s0 is not None:
        in_specs.append(pl.BlockSpec((1, 1, 2, 2 * pps, HEAD_DIM, HEAD_DIM), lambda b, p: (b, layer, 0, p, 0, 0)))
        args.append(s0)
    out_specs = [pl.BlockSpec((seq, width), lambda b, p: (b, p))]
    out_shape = [jax.ShapeDtypeStruct((nb * seq, N_PAIRS * LANES), BF16)]
    if emit_state:
        out_specs.append(pl.BlockSpec((1, 2, 2 * pps, HEAD_DIM, HEAD_DIM), lambda b, p: (b, 0, p, 0, 0)))
        out_shape.append(jax.ShapeDtypeStruct((nb, 2, 2 * N_PAIRS, HEAD_DIM, HEAD_DIM), F32))
    return pl.pallas_call(
        functools.partial(_retention_kernel, n_chunks=n_chunks, rotate=rotate, has_s0=s0 is not None,
                          emit_state=emit_state, pairs_per_step=pps),
        name="retention",
        grid=(nb, N_PAIRS // pps),
        in_specs=in_specs,
        out_specs=out_specs,
        out_shape=out_shape,
        scratch_shapes=[pltpu.VMEM((pps * n_chunks, CHUNK, CHUNK), F32),
                        pltpu.VMEM((pps * n_chunks, CHUNK, CHUNK), F32)],
        compiler_params=_params(("arbitrary", "arbitrary")),
    )(*args)


def _retention_ctx_kernel(lg_ref, q_ref, k_ref, v_ref, g_ref, o_ref, st_ref, decay_ref):
    heads = 2 * N_PAIRS

    @pl.when(pl.program_id(0) == 0)
    def _():
        rel = (lax.broadcasted_iota(I32, (SEQ, SEQ), 0) - lax.broadcasted_iota(I32, (SEQ, SEQ), 1)).astype(F32)
        for h in range(heads):
            decay_ref[h] = (jnp.where(rel >= 0, jnp.exp(jnp.where(rel >= 0, rel, 0.0) * lg_ref[0, h]), 0.0)
                            + jnp.where(rel <= 0, jnp.exp(jnp.where(rel <= 0, -rel, 0.0) * lg_ref[1, h]), 0.0))

    lane = lax.broadcasted_iota(I32, (SEQ, LANES), 1)
    lo_mask = lane < HEAD_DIM
    pos = lax.broadcasted_iota(I32, (SEQ, LANES), 0).astype(F32)
    blockdiag = ((lax.broadcasted_iota(I32, (LANES, LANES), 0) < HEAD_DIM)
                 == (lax.broadcasted_iota(I32, (LANES, LANES), 1) < HEAD_DIM))
    for pp in range(N_PAIRS):
        lanes = slice(pp * LANES, (pp + 1) * LANES)
        h0, h1 = 2 * pp, 2 * pp + 1
        k = k_ref[:, lanes] * (HEAD_DIM ** -0.5)
        qb, kb, vb = q_ref[:, lanes].astype(BF16), k.astype(BF16), v_ref[:, lanes].astype(BF16)
        outs = []
        for h, mask in ((h0, lo_mask), (h1, ~lo_mask)):
            qh = jnp.where(mask, qb, jnp.zeros_like(qb))
            a = lax.dot_general(qh, kb, NT_DIMS, preferred_element_type=F32) * decay_ref[h]
            outs.append(jnp.dot(a.astype(BF16), vb, preferred_element_type=F32))
        o = jnp.where(lo_mask, outs[0], outs[1])
        o2 = o * o
        ms0 = jnp.sum(jnp.where(lo_mask, o2, 0.0), axis=1, keepdims=True) * (1.0 / HEAD_DIM)
        ms1 = jnp.sum(jnp.where(lo_mask, 0.0, o2), axis=1, keepdims=True) * (1.0 / HEAD_DIM)
        inv = jnp.where(lo_mask, lax.rsqrt(ms0 + EPS), lax.rsqrt(ms1 + EPS))
        o_ref[:, lanes] = (_silu(g_ref[:, lanes]) * (o * inv)).astype(BF16)

        for d, age in ((0, SEQ - 1.0 - pos), (1, pos)):
            lg = jnp.where(lo_mask, lg_ref[d, h0], lg_ref[d, h1])
            kd = (k * jnp.exp(age * lg)).T.astype(BF16)
            s = jnp.where(blockdiag, jnp.dot(kd, vb, preferred_element_type=F32), 0.0)
            st_ref[0, d, h0] = s[:HEAD_DIM, :HEAD_DIM]
            st_ref[0, d, h1] = s[HEAD_DIM:, HEAD_DIM:]


def _retention_ctx(z, log_g):
    width = N_PAIRS * LANES
    heads = 2 * N_PAIRS
    zspec = lambda cb: pl.BlockSpec((SEQ, width), lambda b: (b, cb))
    return pl.pallas_call(
        _retention_ctx_kernel,
        name="retention_ctx",
        grid=(BATCH,),
        in_specs=[pl.BlockSpec(memory_space=pltpu.SMEM), zspec(0), zspec(1), zspec(2), zspec(3)],
        out_specs=[pl.BlockSpec((SEQ, width), lambda b: (b, 0)),
                   pl.BlockSpec((1, 2, heads, HEAD_DIM, HEAD_DIM), lambda b: (b, 0, 0, 0, 0))],
        out_shape=[jax.ShapeDtypeStruct((NP, width), BF16),
                   jax.ShapeDtypeStruct((BATCH, 2, heads, HEAD_DIM, HEAD_DIM), F32)],
        scratch_shapes=[pltpu.VMEM((heads, SEQ, SEQ), F32)],
        compiler_params=_params(("arbitrary",)),
    )(log_g, z, z, z, z)


def _rope_tables():
    t = np.arange(DEC_SEQ)
    posn = [(t // GRID_W).astype(np.float32), (t % GRID_W).astype(np.float32)]
    nf = HEAD_DIM // 4
    freqs = (1.0 / (np.float32(ROPE_BASE) ** (np.arange(nf, dtype=np.float32) / np.float32(nf)))).astype(np.float32)
    cos = np.zeros((DEC_SEQ, HEAD_DIM), np.float32)
    sin = np.zeros((DEC_SEQ, HEAD_DIM), np.float32)
    for half in range(2):
        ang = (posn[half][:, None] * freqs[None, :]).astype(np.float32)
        for grp in range(2):
            lo = half * 32 + grp * nf
            cos[:, lo:lo + nf] = np.cos(ang)
            sin[:, lo:lo + nf] = np.sin(ang) * (-1.0 if grp == 0 else 1.0)
    return jnp.asarray(np.tile(cos, (1, 2))), jnp.asarray(np.tile(sin, (1, 2)))


def _pool_kernel(u_ref, w_ref, sc_ref, o_ref, *, seq):
    padded = seq + 2 * POOL_PAD
    t = lax.broadcasted_iota(I32, (seq, 1), 0)
    zpad = jnp.zeros((POOL_PAD, LANES), F32)
    for gi, w in enumerate(POOL_WINDOWS):
        x = u_ref[:, gi * LANES:(gi + 1) * LANES]
        run = jnp.concatenate([zpad, x, zpad], axis=0)
        span = 1
        while span < w:
            run = run + pltpu.roll(run, padded - span, 0)
            span *= 2
        win = pltpu.roll(run, padded - (POOL_PAD - w // 2), 0)[:seq]
        cnt = (jnp.minimum(t + w // 2, seq) - jnp.maximum(t - w // 2, 0)).astype(F32)
        pooled = win / cnt - x
        mixed = jnp.dot(pooled.astype(BF16), w_ref[gi].astype(BF16), preferred_element_type=F32)
        o_ref[:, gi * LANES:(gi + 1) * LANES] = (mixed * sc_ref[:, gi * LANES:(gi + 1) * LANES]).astype(BF16)


def _pool(z, pool_w, pool_scale, *, nb, seq, row_block0):
    width = len(POOL_WINDOWS) * LANES
    return pl.pallas_call(
        functools.partial(_pool_kernel, seq=seq),
        name="pool",
        grid=(nb,),
        in_specs=[pl.BlockSpec((seq, width), lambda b: (row_block0 + b, 2048 // width)),
                  pl.BlockSpec((len(POOL_WINDOWS), LANES, LANES), lambda b: (0, 0, 0)),
                  pl.BlockSpec((1, width), lambda b: (0, 0))],
        out_specs=pl.BlockSpec((seq, width), lambda b: (b, 0)),
        out_shape=jax.ShapeDtypeStruct((nb * seq, width), BF16),
        compiler_params=_params(("arbitrary",), VMEM_LIMIT),
    )(z, pool_w, pool_scale.reshape(1, width))


def _head_select(h, shape):
    lane = lax.broadcasted_iota(I32, shape, 1)
    return (lane < HEAD_DIM) if h == 0 else (lane >= HEAD_DIM)


def _ctx_attn_kernel(q_ref, k_ref, v_ref, o_ref):
    for pp in range(N_PAIRS):
        lanes = slice(pp * LANES, (pp + 1) * LANES)
        qb = (q_ref[:, lanes] * (HEAD_DIM ** -0.5)).astype(BF16)
        kb = k_ref[:, lanes].astype(BF16)
        vb = v_ref[:, lanes].astype(BF16)
        outs = []
        for h in range(2):
            qh = jnp.where(_head_select(h, qb.shape), qb, jnp.zeros_like(qb))
            s = lax.dot_general(qh, kb, NT_DIMS, preferred_element_type=F32)
            p = jnp.exp(s - jnp.max(s, axis=1, keepdims=True))
            denom = jnp.sum(p, axis=1, keepdims=True)
            outs.append(jnp.dot(p.astype(BF16), vb, preferred_element_type=F32) / denom)
        o_ref[:, lanes] = jnp.where(_head_select(0, outs[0].shape), outs[0], outs[1]).astype(BF16)


def _ctx_attention(z):
    width = N_PAIRS * LANES

    def zspec(cb):
        return pl.BlockSpec((SEQ, width), lambda b: (b, cb))

    return pl.pallas_call(
        _ctx_attn_kernel,
        name="ctx_attn",
        grid=(BATCH,),
        in_specs=[zspec(5), zspec(6), zspec(7)],
        out_specs=pl.BlockSpec((SEQ, width), lambda b: (b, 0)),
        out_shape=jax.ShapeDtypeStruct((NP, width), BF16),
        compiler_params=_params(("arbitrary",)),
    )(z, z, z)


NA_QROWS = 4
NA_QBLK = NA_QROWS * GRID_W
NA_KROWS = 12
NA_NBLK = DEC_SEQ // NA_QBLK


def _na_key_block(i):
    return jnp.clip(i - 1, 0, NA_NBLK - 3)


NA_PATTERNS = ((0, 0), (NA_QROWS, 0), (DEC_SEQ // GRID_W - NA_QROWS, DEC_SEQ // GRID_W - NA_KROWS))
NA_DX_LANE = GRID_W - (NA_KW - 1)


def _na_bias_kernel(rpb_ref, o_ref):
    rows = DEC_SEQ // GRID_W
    q = lax.broadcasted_iota(I32, (GRID_W, LANES), 0)
    lane = lax.broadcasted_iota(I32, (GRID_W, LANES), 1)
    c = lane % GRID_W
    c_start = jnp.clip(q - NA_KW // 2, 0, GRID_W - NA_KW)
    col_ok = (c >= c_start) & (c < c_start + NA_KW)
    lower = lane < GRID_W
    for p, (r0, ks) in enumerate(NA_PATTERNS):
        for rr in range(NA_QROWS):
            r = r0 + rr
            start = min(max(r - NA_KH // 2, 0), rows - NA_KH)
            for kp in range(NA_KROWS // 2):
                halves = []
                for half in range(2):
                    kr = ks + 2 * kp + half
                    if start <= kr < start + NA_KH:
                        row = jnp.broadcast_to(rpb_ref[0, 0, pl.ds(kr - r + NA_KH - 1, 1), :], (GRID_W, LANES))
                        halves.append(pltpu.roll(row, GRID_W * (1 - half), 1, stride=1, stride_axis=0))
                    else:
                        halves.append(None)
                neg = jnp.full((GRID_W, LANES), -jnp.inf, F32)
                lo_half = neg if halves[0] is None else jnp.where(col_ok, halves[0], neg)
                hi_half = neg if halves[1] is None else jnp.where(col_ok, halves[1], neg)
                o_ref[0, p, 0, rr * GRID_W:(rr + 1) * GRID_W, kp * LANES:(kp + 1) * LANES] = (
                    jnp.where(lower, lo_half, hi_half))


def _na_bias(na_rpb):
    ny, nx = 2 * NA_KH - 1, 2 * NA_KW - 1
    padded = jnp.pad(na_rpb.astype(F32), ((0, 0), (0, 0), (0, 16 - ny), (NA_DX_LANE, LANES - NA_DX_LANE - nx)))
    heads = 2 * N_PAIRS
    return pl.pallas_call(
        _na_bias_kernel,
        name="nbr_bias",
        grid=(DEPTH, heads),
        in_specs=[pl.BlockSpec((1, 1, 16, LANES), lambda l, h: (l, h, 0, 0))],
        out_specs=pl.BlockSpec((1, len(NA_PATTERNS), 1, NA_QBLK, NA_KROWS * GRID_W), lambda l, h: (l, 0, h, 0, 0)),
        out_shape=jax.ShapeDtypeStruct((DEPTH, len(NA_PATTERNS), heads, NA_QBLK, NA_KROWS * GRID_W), F32),
        compiler_params=_params(("arbitrary", "arbitrary")),
    )(padded)


def _na_kernel(q_ref, k0_ref, k1_ref, k2_ref, v0_ref, v1_ref, v2_ref, ck_ref, cv_ref, bias_ref, o_ref):
    qb = (q_ref[...] * (HEAD_DIM ** -0.5)).astype(BF16)
    ks = [r[...].astype(BF16) for r in (k0_ref, k1_ref, k2_ref)] + [ck_ref[0, 0].astype(BF16)]
    vs = [r[...].astype(BF16) for r in (v0_ref, v1_ref, v2_ref)] + [cv_ref[0, 0].astype(BF16)]
    outs = []
    for h in range(2):
        qh = jnp.where(_head_select(h, qb.shape), qb, jnp.zeros_like(qb))
        ss = []
        for j in range(4):
            s = lax.dot_general(qh, ks[j], NT_DIMS, preferred_element_type=F32)
            if j < 3:
                s = s + bias_ref[0, h, :, j * NA_QBLK:(j + 1) * NA_QBLK]
            ss.append(s)
        m = functools.reduce(jnp.maximum, [jnp.max(s, axis=1, keepdims=True) for s in ss])
        ps = [jnp.exp(s - m) for s in ss]
        denom = functools.reduce(jnp.add, [jnp.sum(p, axis=1, keepdims=True) for p in ps])
        acc = functools.reduce(jnp.add, [jnp.dot(p.astype(BF16), v, preferred_element_type=F32)
                                         for p, v in zip(ps, vs)])
        outs.append(acc / denom)
    o_ref[...] = jnp.where(_head_select(0, outs[0].shape), outs[0], outs[1]).astype(BF16)


def _neighbourhood_attention(z, bias, cache_k, cache_v, layer):
    base = NP // NA_QBLK

    def qspec():
        return pl.BlockSpec((NA_QBLK, LANES), lambda b, i, p: (base + b * NA_NBLK + i, 20 + p))

    def kvspec(cb, j):
        return pl.BlockSpec((NA_QBLK, LANES),
                            lambda b, i, p: (base + b * NA_NBLK + _na_key_block(i) + j, cb + p))

    def cspec():
        return pl.BlockSpec((1, 1, SEQ, LANES), lambda b, i, p: (b, layer, 0, p))

    pattern = lambda i: jnp.where(i == 0, 0, jnp.where(i == NA_NBLK - 1, 2, 1))
    ck = cache_k.reshape(DEC_BATCH, DEPTH, SEQ, N_PAIRS * LANES)
    cv = cache_v.reshape(DEC_BATCH, DEPTH, SEQ, N_PAIRS * LANES)
    return pl.pallas_call(
        _na_kernel,
        name="nbr_attn",
        grid=(DEC_BATCH, NA_NBLK, N_PAIRS),
        in_specs=[qspec()] + [kvspec(24, j) for j in range(3)] + [kvspec(28, j) for j in range(3)]
                 + [cspec(), cspec(),
                    pl.BlockSpec((None, 1, 2, NA_QBLK, NA_KROWS * GRID_W),
                                 lambda b, i, p: (layer, pattern(i), p, 0, 0))],
        out_specs=pl.BlockSpec((NA_QBLK, LANES), lambda b, i, p: (b * NA_NBLK + i, p)),
        out_shape=jax.ShapeDtypeStruct((NS, N_PAIRS * LANES), BF16),
        compiler_params=_params(("arbitrary", "arbitrary", "arbitrary")),
    )(z, z, z, z, z, z, z, ck, cv, bias)


def _merge_kernel(rp_ref, rs_ref, pp_ref, ps_ref, ap_ref, as_ref, xp_ref, xs_ref, g0_ref, g1_ref, g2_ref,
                  wr_ref, wp_ref, wa_ref, wo_ref, m_ref, ln_ref, o_ref, wrb, wpb, wab, wob):
    @pl.when(pl.program_id(0) == 0)
    def _():
        wrb[...] = wr_ref[...].astype(BF16)
        wpb[...] = wp_ref[...].astype(BF16)
        wab[...] = wa_ref[...].astype(BF16)
        wob[...] = wo_ref[...].astype(BF16)

    branch = lambda p_ref, s_ref, w: jnp.dot(_pick(p_ref, s_ref), w[...], preferred_element_type=F32)
    merged = (jax.nn.sigmoid(g0_ref[...]) * branch(rp_ref, rs_ref, wrb)
              + jax.nn.sigmoid(g1_ref[...]) * branch(pp_ref, ps_ref, wpb)
              + jax.nn.sigmoid(g2_ref[...]) * branch(ap_ref, as_ref, wab))
    mix = jnp.dot(merged.astype(BF16), wob[...], preferred_element_type=F32)
    o_ref[...] = _pick(xp_ref, xs_ref) + m_ref[0, 2:3, :] * (_rms(mix) * ln_ref[...])


def _merge(ret_pair, pool_pair, na_pair, x_pair, z, w_ret_o, w_pool_o, w_na_o, w_o, mod_l, ln, layer):
    tm = 512
    half = N_PAIRS * LANES
    row = lambda i: (i, 0)
    const = lambda i: (0, 0)
    slab = lambda i: (layer, 0, 0)
    return pl.pallas_call(
        _merge_kernel,
        name="merge",
        grid=(NT // tm,),
        in_specs=_pair_specs(tm, half) * 3 + _pair_specs(tm, D)
                 + [pl.BlockSpec((tm, D), lambda i, c=c: (i, 4 + c)) for c in range(3)]
                 + [pl.BlockSpec((None, half, D), slab)] * 3
                 + [pl.BlockSpec((None, D, D), slab),
                    pl.BlockSpec((1, N_MOD, D), lambda i: (_mod_row(i * tm), 0, 0)),
                    pl.BlockSpec((1, D), const)],
        out_specs=pl.BlockSpec((tm, D), row),
        out_shape=jax.ShapeDtypeStruct((NT, D), F32),
        scratch_shapes=[pltpu.VMEM((half, D), BF16)] * 3 + [pltpu.VMEM((D, D), BF16)],
        compiler_params=_params(("arbitrary",), VMEM_LIMIT),
    )(*ret_pair, *pool_pair, *na_pair, *x_pair, z, z, z, w_ret_o, w_pool_o, w_na_o, w_o, mod_l, ln.reshape(1, D))


def _ffnprep_kernel(x_ref, m_ref, ln_ref, wr_ref, h_ref, aff_ref):
    y = _rms(x_ref[...]) * ln_ref[...]
    h = y * (1.0 + m_ref[0, 4:5, :]) + m_ref[0, 3:4, :]
    hb = h.astype(BF16)
    h_ref[...] = hb
    hl = (h - hb.astype(F32)).astype(BF16)
    w = wr_ref[...]
    wb = w.astype(BF16)
    wl = (w - wb.astype(F32)).astype(BF16)
    logits = (lax.dot_general(wb, hb, NT_DIMS, preferred_element_type=F32)
              + lax.dot_general(wb, hl, NT_DIMS, preferred_element_type=F32)
              + lax.dot_general(wl, hb, NT_DIMS, preferred_element_type=F32))
    e = jnp.exp(logits - jnp.max(logits, axis=0, keepdims=True))
    aff = e / jnp.sum(e, axis=0, keepdims=True)
    for j in range(aff_ref.shape[0]):
        aff_ref[j] = aff[:, j * LANES:(j + 1) * LANES]


def _ffnprep(x_all, mod_l, ln, w_router_t):
    tm = 512
    return pl.pallas_call(
        _ffnprep_kernel,
        name="ffn_prep",
        grid=(NT // tm,),
        in_specs=[pl.BlockSpec((tm, D), lambda i: (i, 0)),
                  pl.BlockSpec((1, N_MOD, D), lambda i: (_mod_row(i * tm), 0, 0)),
                  pl.BlockSpec((1, D), lambda i: (0, 0)),
                  pl.BlockSpec((N_EXPERTS, D), lambda i: (0, 0))],
        out_specs=[pl.BlockSpec((tm, D), lambda i: (i, 0)),
                   pl.BlockSpec((tm // LANES, N_EXPERTS, LANES), lambda i: (i, 0, 0))],
        out_shape=[jax.ShapeDtypeStruct((NT, D), BF16),
                   jax.ShapeDtypeStruct((NT // LANES, N_EXPERTS, LANES), F32)],
        compiler_params=_params(("arbitrary",)),
    )(x_all, mod_l, ln.reshape(1, D), w_router_t)


def _route_kernel(aff_ref, slot_ref, offs_ref, ceq_ref, csel_ref, *, cap, nblk):
    as_float = lambda bits: lax.bitcast_convert_type(bits, F32)

    def count(pred):
        return jnp.sum(jnp.sum(jnp.where(pred, 1.0, 0.0), axis=0), axis=1, keepdims=True)

    def search(_, lohi):
        lo, hi = lohi
        mid = lo + ((hi - lo + 1) >> 1)
        ok = count(aff_ref[...] >= as_float(mid)[None]) >= cap
        return jnp.where(ok, mid, lo), jnp.where(ok, hi, mid - 1)

    lo0 = jnp.zeros((N_EXPERTS, 1), I32)
    hi0 = jnp.full((N_EXPERTS, 1), 0x7F800000, I32)
    thr_bits, _ = lax.fori_loop(0, 31, search, (lo0, hi0))
    thr = as_float(thr_bits)
    need = cap - count(aff_ref[...] > thr[None])

    upper = (lax.broadcasted_iota(I32, (LANES, LANES), 0)
             < lax.broadcasted_iota(I32, (LANES, LANES), 1)).astype(BF16)
    lane = lax.broadcasted_iota(I32, (N_EXPERTS, LANES), 1)

    ceq_ref[...] = jnp.zeros_like(ceq_ref)
    csel_ref[...] = jnp.zeros_like(csel_ref)
    offs_ref[...] = jnp.zeros_like(offs_ref)

    def block(b, carry):
        c_eq = ceq_ref[...]
        c_sel = csel_ref[...]
        aff = aff_ref[b]
        eq = aff == thr
        eqf = jnp.where(eq, 1.0, 0.0)
        eq_rank = jnp.dot(eqf.astype(BF16), upper, preferred_element_type=F32) + c_eq
        sel = (aff > thr) | (eq & (eq_rank < need))
        self_ = jnp.where(sel, 1.0, 0.0)
        rank = jnp.dot(self_.astype(BF16), upper, preferred_element_type=F32) + c_sel
        slot_ref[b] = jnp.where(sel, rank, -1.0)
        offs_ref[...] = jnp.where(lane == b, c_sel.astype(I32), offs_ref[...])
        ceq_ref[...] = c_eq + jnp.sum(eqf, axis=1, keepdims=True)
        csel_ref[...] = c_sel + jnp.sum(self_, axis=1, keepdims=True)
        return carry

    lax.fori_loop(0, nblk, block, 0)
    offs_ref[...] = jnp.where(lane >= nblk, csel_ref[...].astype(I32), offs_ref[...])


def _route(aff_blocks, *, blk0, nblk, cap):
    return pl.pallas_call(
        functools.partial(_route_kernel, cap=cap, nblk=nblk),
        name="route",
        grid=(1,),
        in_specs=[pl.BlockSpec((nblk, N_EXPERTS, LANES), lambda i: (blk0 // nblk, 0, 0))],
        out_specs=[pl.BlockSpec((nblk, N_EXPERTS, LANES), lambda i: (0, 0, 0)),
                   pl.BlockSpec((N_EXPERTS, LANES), lambda i: (0, 0))],
        out_shape=[jax.ShapeDtypeStruct((nblk, N_EXPERTS, LANES), F32),
                   jax.ShapeDtypeStruct((N_EXPERTS, LANES), I32)],
        scratch_shapes=[pltpu.VMEM((N_EXPERTS, LANES), F32)] * 2,
        compiler_params=_params(("arbitrary",)),
    )(aff_blocks)


GATHER_TOKENS = 256
TILE_BLOCKS = GATHER_TOKENS // LANES
SLOT_ALIGN = 16
ROUND_SLOTS = 64
ROUND_SHIFT = ROUND_SLOTS.bit_length() - 1
WINDOW = 128
GATHER_GROUP = 8
GATE_TERMS = 3
ROW_W = D + LANES


def _round_bounds(offs_ref, e, t, r):
    off0 = offs_ref[e, TILE_BLOCKS * t]
    off1 = offs_ref[e, TILE_BLOCKS * t + TILE_BLOCKS]
    lo = jnp.minimum(off0 + ROUND_SLOTS * r, off1)
    hi = jnp.minimum(lo + ROUND_SLOTS, off1)
    return lo, hi, pl.multiple_of(lo & -SLOT_ALIGN, SLOT_ALIGN)


def _n_rounds(offs_ref, t):
    rounds = jnp.int32(0)
    for e in range(N_EXPERTS):
        cnt = offs_ref[e, TILE_BLOCKS * t + TILE_BLOCKS] - offs_ref[e, TILE_BLOCKS * t]
        rounds = jnp.maximum(rounds, (cnt + (ROUND_SLOTS - 1)) >> ROUND_SHIFT)
    return rounds


def _gate_terms(aff_cols):
    hi = aff_cols.astype(BF16)
    rest = aff_cols - hi.astype(F32)
    mid = rest.astype(BF16)
    lo = (rest - mid.astype(F32)).astype(BF16)
    terms = jnp.stack([hi, mid, lo], axis=-1).reshape(aff_cols.shape[0], GATE_TERMS * N_EXPERTS)
    return jnp.pad(terms, ((0, 0), (0, LANES - GATE_TERMS * N_EXPERTS)))


def _gather_kernel(offs_ref, slot_ref, h_ref, g_ref, xe_hbm, stage_ref, carry_ref, sem, nround_ref, *, n_tiles):
    t = pl.program_id(0)
    cap = xe_hbm.shape[1] - WINDOW

    def out_copy(buf, e, start):
        return pltpu.make_async_copy(stage_ref.at[buf, e], xe_hbm.at[e, pl.ds(start, WINDOW)], sem.at[buf])

    def wait_round(buf):
        for e in range(N_EXPERTS):
            out_copy(buf, e, 0).wait()

    @pl.when(t == 0)
    def _():
        carry_ref[...] = jnp.zeros_like(carry_ref)
        nround_ref[0] = 0
        stage_ref[0, 0] = jnp.zeros((WINDOW, ROW_W), BF16)
        for e in range(N_EXPERTS):
            pltpu.make_async_copy(stage_ref.at[0, 0], xe_hbm.at[e, pl.ds(cap, WINDOW)], sem.at[0]).start()
        wait_round(0)

    hb = jnp.concatenate([h_ref[...], g_ref[...]], axis=1)
    sub = lax.broadcasted_iota(I32, (WINDOW, GATHER_TOKENS), 0).astype(F32)

    def one_round(r, carry):
        done = nround_ref[0]
        buf = done & 1
        bounds = [_round_bounds(offs_ref, e, t, r) for e in range(N_EXPERTS)]
        for e0 in range(0, N_EXPERTS, GATHER_GROUP):
            group = range(e0, e0 + GATHER_GROUP)
            onehots = []
            for e in group:
                lo, hi, start = bounds[e]
                srow = jnp.concatenate([slot_ref[j, e:e + 1, :] for j in range(TILE_BLOCKS)], axis=1)
                hit = ((srow - start.astype(F32) == sub) & (srow >= lo.astype(F32)) & (srow < hi.astype(F32)))
                onehots.append(jnp.where(hit, 1.0, 0.0).astype(BF16))
            rows = jnp.dot(jnp.concatenate(onehots, axis=0), hb, preferred_element_type=F32)
            for i, e in enumerate(group):
                lo, hi, start = bounds[e]
                piece = rows[i * WINDOW:(i + 1) * WINDOW]
                head = piece[:SLOT_ALIGN] + carry_ref[e].astype(F32)
                stage_ref[buf, e, :SLOT_ALIGN, :] = head.astype(BF16)
                stage_ref[buf, e, SLOT_ALIGN:, :] = piece[SLOT_ALIGN:].astype(BF16)
                tail = pl.multiple_of((hi & -SLOT_ALIGN) - start, SLOT_ALIGN)
                carry_ref[e] = stage_ref[buf, e, pl.ds(tail, SLOT_ALIGN), :]

        @pl.when(done > 0)
        def _():
            wait_round(1 - buf)

        for e in range(N_EXPERTS):
            out_copy(buf, e, bounds[e][2]).start()
        nround_ref[0] = done + 1
        return carry

    lax.fori_loop(0, _n_rounds(offs_ref, t), one_round, 0)

    @pl.when((t == n_tiles - 1) & (nround_ref[0] > 0))
    def _():
        wait_round((nround_ref[0] - 1) & 1)


def _gather(offs, slot, h_all, gate_terms, *, row0, n, cap):
    n_tiles = n // GATHER_TOKENS
    tile0 = row0 // GATHER_TOKENS
    return pl.pallas_call(
        functools.partial(_gather_kernel, n_tiles=n_tiles),
        name="gather",
        grid_spec=pltpu.PrefetchScalarGridSpec(
            num_scalar_prefetch=1,
            grid=(n_tiles,),
            in_specs=[pl.BlockSpec((TILE_BLOCKS, N_EXPERTS, LANES), lambda t, o: (t, 0, 0)),
                      pl.BlockSpec((GATHER_TOKENS, D), lambda t, o: (tile0 + t, 0)),
                      pl.BlockSpec((GATHER_TOKENS, LANES), lambda t, o: (t, 0))],
            out_specs=pl.BlockSpec(memory_space=pl.ANY),
            scratch_shapes=[pltpu.VMEM((2, N_EXPERTS, WINDOW, ROW_W), BF16),
                            pltpu.VMEM((N_EXPERTS, SLOT_ALIGN, ROW_W), BF16),
                            pltpu.SemaphoreType.DMA((2,)),
                            pltpu.SMEM((1,), I32)]),
        out_shape=jax.ShapeDtypeStruct((N_EXPERTS, cap + WINDOW, ROW_W), BF16),
        compiler_params=_params(("arbitrary",), VMEM_LIMIT),
    )(offs, slot, h_all, gate_terms)


FF_CHUNK = 512


def _experts_kernel(xp_ref, xs_ref, wg_ref, wu_ref, wd_ref, yp_ref, ys_ref, accp_ref, accs_ref, *, n_f):
    f = pl.program_id(1)
    wg = wg_ref[0].astype(BF16)
    wu = wu_ref[0].astype(BF16)
    wd = wd_ref[0].astype(BF16)

    def ffn(x):
        a = jnp.dot(x, wg, preferred_element_type=F32)
        b = jnp.dot(x, wu, preferred_element_type=F32)
        return jnp.dot((_silu(a) * b).astype(BF16), wd, preferred_element_type=F32)

    yp = ffn(xp_ref[0, :, :D])
    ys = ffn(xs_ref[0, :, :D])

    @pl.when(f == 0)
    def _():
        accp_ref[...] = yp
        accs_ref[...] = ys

    @pl.when(f > 0)
    def _():
        accp_ref[...] += yp
        accs_ref[...] += ys

    @pl.when(f == n_f - 1)
    def _():
        first = GATE_TERMS * pl.program_id(0)
        for x_ref, y_ref, acc_ref in ((xp_ref, yp_ref, accp_ref), (xs_ref, ys_ref, accs_ref)):
            cap = acc_ref.shape[0]
            lane = lax.broadcasted_iota(I32, (cap, LANES), 1)
            mine = (lane >= first) & (lane < first + GATE_TERMS)
            gate = jnp.sum(jnp.where(mine, x_ref[0, :, D:].astype(F32), 0.0), axis=1, keepdims=True)
            y_ref[0, :cap, :] = (acc_ref[...] * gate).astype(BF16)
            y_ref[0, cap:, :] = jnp.zeros((WINDOW, D), BF16)


def _experts(xe_p, xe_s, w_gate, w_up, w_down, layer):
    n_f = EXPERT_FF // FF_CHUNK
    cap_p, cap_s = xe_p.shape[1] - WINDOW, xe_s.shape[1] - WINDOW
    spec = lambda rows, width: pl.BlockSpec((1, rows, width), lambda e, f: (e, 0, 0))
    out = lambda cap: jax.ShapeDtypeStruct((N_EXPERTS, cap + WINDOW, D), BF16)
    return pl.pallas_call(
        functools.partial(_experts_kernel, n_f=n_f),
        name="experts",
        grid=(N_EXPERTS, n_f),
        in_specs=[spec(cap_p, ROW_W), spec(cap_s, ROW_W),
                  pl.BlockSpec((None, 1, D, FF_CHUNK), lambda e, f: (layer, e, 0, f)),
                  pl.BlockSpec((None, 1, D, FF_CHUNK), lambda e, f: (layer, e, 0, f)),
                  pl.BlockSpec((None, 1, FF_CHUNK, D), lambda e, f: (layer, e, f, 0))],
        out_specs=[spec(cap_p + WINDOW, D), spec(cap_s + WINDOW, D)],
        out_shape=[out(cap_p), out(cap_s)],
        scratch_shapes=[pltpu.VMEM((cap_p, D), F32), pltpu.VMEM((cap_s, D), F32)],
        compiler_params=_params(("arbitrary", "arbitrary"), VMEM_LIMIT),
    )(xe_p, xe_s, w_gate, w_up, w_down)


def _combine_kernel(offs_ref, slot_ref, x_ref, m_ref, ln_ref, ye_hbm, o_ref, stage_ref, sem, *, n_tiles):
    t = pl.program_id(0)
    buf = t & 1

    def in_copy(b, e, start):
        return pltpu.make_async_copy(ye_hbm.at[e, pl.ds(start, WINDOW)],
                                     stage_ref.at[b, pl.ds(e * WINDOW, WINDOW)], sem.at[b])

    def start_round(b, tile, r):
        for e in range(N_EXPERTS):
            in_copy(b, e, _round_bounds(offs_ref, e, tile, r)[2]).start()

    def wait_round(b):
        for e in range(N_EXPERTS):
            in_copy(b, e, 0).wait()

    @pl.when(t == 0)
    def _():
        start_round(0, 0, 0)

    wait_round(buf)

    @pl.when(t + 1 < n_tiles)
    def _():
        start_round(1 - buf, t + 1, 0)

    slot = slot_ref[...]
    lane = lax.broadcasted_iota(I32, (GATHER_TOKENS, WINDOW), 1).astype(F32)

    def token_rows(b, r):
        onehots = []
        for e in range(N_EXPERTS):
            lo, hi, start = _round_bounds(offs_ref, e, t, r)
            s = slot[:, e:e + 1]
            hit = (s - start.astype(F32) == lane) & (s >= lo.astype(F32)) & (s < hi.astype(F32))
            onehots.append(jnp.where(hit, 1.0, 0.0).astype(BF16))
        return jnp.dot(jnp.concatenate(onehots, axis=1), stage_ref[b], preferred_element_type=F32)

    o_ref[...] = token_rows(buf, 0)

    def extra_round(r, carry):
        start_round(buf, t, r)
        wait_round(buf)
        o_ref[...] += token_rows(buf, r)
        return carry

    lax.fori_loop(1, _n_rounds(offs_ref, t), extra_round, 0)
    o_ref[...] = x_ref[...] + m_ref[0, 5:6, :] * (_rms(o_ref[...]) * ln_ref[...])


def _combine(offs, ye, slot_cols, x_all, mod_l, ln, *, row0, n):
    n_tiles = n // GATHER_TOKENS
    tile0 = row0 // GATHER_TOKENS
    return pl.pallas_call(
        functools.partial(_combine_kernel, n_tiles=n_tiles),
        name="combine",
        grid_spec=pltpu.PrefetchScalarGridSpec(
            num_scalar_prefetch=1,
            grid=(n_tiles,),
            in_specs=[pl.BlockSpec((GATHER_TOKENS, N_EXPERTS), lambda t, o: (t, 0)),
                      pl.BlockSpec((GATHER_TOKENS, D), lambda t, o: (tile0 + t, 0)),
                      pl.BlockSpec((1, N_MOD, D), lambda t, o: (_mod_row(row0 + t * GATHER_TOKENS), 0, 0)),
                      pl.BlockSpec((1, D), lambda t, o: (0, 0)),
                      pl.BlockSpec(memory_space=pl.ANY)],
            out_specs=pl.BlockSpec((GATHER_TOKENS, D), lambda t, o: (t, 0)),
            scratch_shapes=[pltpu.VMEM((2, N_EXPERTS * WINDOW, D), BF16),
                            pltpu.SemaphoreType.DMA((2,))]),
        out_shape=jax.ShapeDtypeStruct((n, D), F32),
        compiler_params=_params(("arbitrary",), VMEM_LIMIT),
    )(offs, slot_cols, x_all, mod_l, ln.reshape(1, D), ye)


def _token_major(blocks):
    return blocks.transpose(0, 2, 1).reshape(-1, N_EXPERTS)


def kernel(x_prompt, x_sample, cache_k, cache_v, state_ret, c, c_ctx, w_mod, b_mod, ln_pre_mix, ln_post_mix,
           ln_pre_ffn, ln_post_ffn, w_in, ret_decay, pool_w, pool_scale, na_rpb, w_ret_o, w_pool_o, w_na_o, w_o,
           w_router, w_gate, w_up, w_down):
    cvecs = jnp.zeros((8, D), F32).at[0].set(c_ctx).at[1:1 + DEC_BATCH].set(c)
    mod = _modulation(cvecs, w_mod, b_mod)
    rope = _rope_tables()
    na_bias = _na_bias(na_rpb)
    x_pair = (x_prompt.reshape(NP, D), x_sample.reshape(NS, D))
    h_layers, new_s = [], []
    groups = ((0, NP, NP // N_EXPERTS * 2), (NP, NS, NS // N_EXPERTS * 2))

    for l in range(DEPTH):
        mod_l = mod[l]
        h_all = _prenorm(x_pair, mod_l, ln_pre_mix[l])
        h_layers.append(h_all)
        z = _in_proj(h_all, w_in, l)
        log_g = jax.nn.log_sigmoid(ret_decay[l].astype(F32))

        ret_p, st = _retention_ctx(z, log_g)
        (ret_s,) = _retention(z, log_g, nb=DEC_BATCH, seq=DEC_SEQ, row_block0=NP // DEC_SEQ, pairs_per_step=1,
                              rope=rope, s0=state_ret, layer=l)
        pool_p = _pool(z, pool_w[l], pool_scale[l], nb=BATCH, seq=SEQ, row_block0=0)
        pool_s = _pool(z, pool_w[l], pool_scale[l], nb=DEC_BATCH, seq=DEC_SEQ, row_block0=NP // DEC_SEQ)
        na_p = _ctx_attention(z)
        na_s = _neighbourhood_attention(z, na_bias, cache_k, cache_v, l)
        x_mid = _merge((ret_p, ret_s), (pool_p, pool_s), (na_p, na_s), x_pair, z,
                       w_ret_o, w_pool_o, w_na_o, w_o, mod_l, ln_post_mix[l], l)
        new_s.append(st)

        h2, aff = _ffnprep(x_mid, mod_l, ln_pre_ffn[l], w_router[l].T)
        routed = []
        for row0, n, cap in groups:
            slot, offs = _route(aff, blk0=row0 // LANES, nblk=n // LANES, cap=cap)
            gate_terms = _gate_terms(_token_major(aff[row0 // LANES:(row0 + n) // LANES]))
            routed.append((slot, offs, _gather(offs, slot, h2, gate_terms, row0=row0, n=n, cap=cap)))
        ye = _experts(routed[0][2], routed[1][2], w_gate, w_up, w_down, l)
        outs = []
        for (row0, n, cap), (slot, offs, _), y in zip(groups, routed, ye):
            outs.append(_combine(offs, y, _token_major(slot), x_mid, mod_l, ln_post_ffn[l], row0=row0, n=n))
        x_pair = tuple(outs)

    y_prompt = x_pair[0].reshape(BATCH, SEQ, D)
    y_sample = x_pair[1].reshape(DEC_BATCH, DEC_SEQ, D)
    new_k, new_v = _kv_proj(h_layers, w_in)
    return (y_prompt, y_sample, new_k, new_v, jnp.stack(new_s, axis=1))
```

```python
import functools

import numpy as np
import jax
import jax.numpy as jnp
from jax import lax
from jax.experimental import pallas as pl
from jax.experimental.pallas import tpu as pltpu

F32 = jnp.float32
BF16 = jnp.bfloat16
I32 = jnp.int32

D = 1024
BATCH, SEQ = 32, 256
DEC_BATCH, DEC_SEQ = 2, 2048
DEPTH = 2
NP = BATCH * SEQ
NS = DEC_BATCH * DEC_SEQ
NT = NP + NS
GRID_W = 64
N_MOD = 6
EPS = 1e-6
ROPE_BASE = 10000.0
HEAD_DIM = 64
N_PAIRS = 4
CHUNK = 128
POOL_WINDOWS = (2, 4, 8, 16)
POOL_PAD = 16
NA_KH, NA_KW = 8, 16
N_EXPERTS = 16
EXPERT_FF = 2048
IN_COLS = 7168
LANES = 128
VMEM_LIMIT = 56 * 1024 * 1024

NT_DIMS = (((1,), (1,)), ((), ()))


def _params(sem, vmem=None):
    return pltpu.CompilerParams(dimension_semantics=sem, vmem_limit_bytes=vmem)


def _mod_row(row_start):
    return jnp.where(row_start < NP, 0, 1 + (row_start - NP) // DEC_SEQ)


def _silu(x):
    return x * jax.nn.sigmoid(x)


def _rms(x):
    return x * lax.rsqrt(jnp.mean(x * x, axis=-1, keepdims=True) + EPS)


def _pair_specs(tm, width):
    n_p = NP // tm
    return [pl.BlockSpec((tm, width), lambda i: (jnp.minimum(i, n_p - 1), 0)),
            pl.BlockSpec((tm, width), lambda i: (jnp.maximum(i - n_p, 0), 0))]


def _pick(p_ref, s_ref):
    return jnp.where(pl.program_id(0) < NP // p_ref.shape[0], p_ref[...], s_ref[...])


def _mod_kernel(c_ref, w_ref, b_ref, o_ref):
    a = _silu(c_ref[...]).astype(BF16)
    o_ref[0] = jnp.dot(a, w_ref[0].astype(BF16), preferred_element_type=F32) + b_ref[0]


def _modulation(cvecs, w_mod, b_mod):
    out = pl.pallas_call(
        _mod_kernel,
        name="modulation",
        grid=(DEPTH, N_MOD),
        in_specs=[pl.BlockSpec((8, D), lambda l, j: (0, 0)),
                  pl.BlockSpec((1, D, D), lambda l, j: (l, 0, j)),
                  pl.BlockSpec((1, 1, D), lambda l, j: (l, 0, j))],
        out_specs=pl.BlockSpec((1, 8, D), lambda l, j: (l, 0, j)),
        out_shape=jax.ShapeDtypeStruct((DEPTH, 8, N_MOD * D), F32),
        compiler_params=_params(("arbitrary", "arbitrary")),
    )(cvecs, w_mod, b_mod.reshape(DEPTH, 1, N_MOD * D))
    return out.reshape(DEPTH, 8, N_MOD, D)


def _prenorm_kernel(xp_ref, xs_ref, m_ref, ln_ref, h_ref):
    y = _rms(_pick(xp_ref, xs_ref)) * ln_ref[...]
    h_ref[...] = (y * (1.0 + m_ref[0, 1:2, :]) + m_ref[0, 0:1, :]).astype(BF16)


def _prenorm(x_pair, mod_l, ln):
    tm = 512
    return pl.pallas_call(
        _prenorm_kernel,
        name="prenorm",
        grid=(NT // tm,),
        in_specs=_pair_specs(tm, D)
                 + [pl.BlockSpec((1, N_MOD, D), lambda i: (_mod_row(i * tm), 0, 0)),
                  pl.BlockSpec((1, D), lambda i: (0, 0))],
        out_specs=pl.BlockSpec((tm, D), lambda i: (i, 0)),
        out_shape=jax.ShapeDtypeStruct((NT, D), BF16),
        compiler_params=_params(("arbitrary",)),
    )(*x_pair, mod_l, ln.reshape(1, D))


def _mm_kernel(a_ref, w_ref, o_ref, wb_ref):
    @pl.when(pl.program_id(1) == 0)
    def _():
        wb_ref[...] = w_ref[...].astype(BF16)

    o_ref[...] = jnp.dot(a_ref[...], wb_ref[...], preferred_element_type=F32)


def _kv_kernel(h0_ref, h1_ref, w_ref, k_ref, v_ref, wb_ref):
    @pl.when(pl.program_id(1) == 0)
    def _():
        wb_ref[...] = w_ref[0].astype(BF16)

    h = jnp.where(pl.program_id(0) == 0, h0_ref[...], h1_ref[...])
    kv = jnp.dot(h, wb_ref[...], preferred_element_type=F32)
    half = N_PAIRS * LANES
    k_ref[...] = kv[:, :half].reshape(k_ref.shape)
    v_ref[...] = kv[:, half:].reshape(v_ref.shape)


def _kv_proj(h_layers, w_in):
    per = 4
    kv_col = 3072 // D
    hspec = pl.BlockSpec((per * SEQ, D), lambda l, i: (i, 0))
    ospec = pl.BlockSpec((per, 1, SEQ, N_PAIRS * LANES), lambda l, i: (i, l, 0, 0))
    shape = jax.ShapeDtypeStruct((BATCH, DEPTH, SEQ, N_PAIRS * LANES), F32)
    k, v = pl.pallas_call(
        _kv_kernel,
        name="kv_proj",
        grid=(DEPTH, BATCH // per),
        in_specs=[hspec, hspec, pl.BlockSpec((1, D, D), lambda l, i: (l, 0, kv_col))],
        out_specs=[ospec, ospec],
        out_shape=[shape, shape],
        scratch_shapes=[pltpu.VMEM((D, D), BF16)],
        compiler_params=_params(("arbitrary", "arbitrary"), VMEM_LIMIT),
    )(*h_layers, w_in)
    cache_shape = (BATCH, DEPTH, SEQ, 2 * N_PAIRS, HEAD_DIM)
    return k.reshape(cache_shape), v.reshape(cache_shape)


def _in_proj(h_all, w_in, layer):
    tm, tn = 512, 1792
    return pl.pallas_call(
        _mm_kernel,
        name="in_proj",
        grid=(IN_COLS // tn, NT // tm),
        in_specs=[pl.BlockSpec((tm, D), lambda j, i: (i, 0)),
                  pl.BlockSpec((None, D, tn), lambda j, i: (layer, 0, j))],
        out_specs=pl.BlockSpec((tm, tn), lambda j, i: (i, j)),
        out_shape=jax.ShapeDtypeStruct((NT, IN_COLS), F32),
        scratch_shapes=[pltpu.VMEM((D, tn), BF16)],
        compiler_params=_params(("arbitrary", "arbitrary"), VMEM_LIMIT),
    )(h_all, w_in)


def _swap16(x):
    lane = lax.broadcasted_iota(I32, x.shape, 1)
    return jnp.where((lane // 16) % 2 == 0, pltpu.roll(x, LANES - 16, 1), pltpu.roll(x, 16, 1))


def _block_diag(top, bottom):
    z = jnp.zeros((HEAD_DIM, HEAD_DIM), F32)
    return jnp.concatenate([jnp.concatenate([top, z], axis=1),
                            jnp.concatenate([z, bottom], axis=1)], axis=0)


RET_PAIRS_PER_STEP = 2


def _retention_kernel(lg_ref, q_ref, k_ref, v_ref, g_ref, cos_ref, sin_ref, s0_ref, o_ref, sf_scr, sb_scr):
    for pp in range(RET_PAIRS_PER_STEP):
        _retention_pair(lg_ref, q_ref, k_ref, v_ref, g_ref, cos_ref, sin_ref, s0_ref, o_ref, sf_scr, sb_scr,
                        pp, pl.program_id(1) * RET_PAIRS_PER_STEP + pp)


def _retention_pair(lg_ref, q_ref, k_ref, v_ref, g_ref, cos_ref, sin_ref, s0_ref, o_ref, sf_scr, sb_scr, pp, pair):
    n_chunks = DEC_SEQ // CHUNK
    lanes = slice(pp * LANES, (pp + 1) * LANES)
    lane1 = lax.broadcasted_iota(I32, (1, LANES), 1)
    lo1 = lane1 < HEAD_DIM
    lgf = jnp.where(lo1, lg_ref[0, 2 * pair], lg_ref[0, 2 * pair + 1])
    lgb = jnp.where(lo1, lg_ref[1, 2 * pair], lg_ref[1, 2 * pair + 1])
    lg_heads = [(lg_ref[0, 2 * pair], lg_ref[1, 2 * pair]),
                (lg_ref[0, 2 * pair + 1], lg_ref[1, 2 * pair + 1])]

    row = lax.broadcasted_iota(I32, (CHUNK, CHUNK), 0)
    col = lax.broadcasted_iota(I32, (CHUNK, CHUNK), 1)
    rel = (row - col).astype(F32)
    lo_mask = col < HEAD_DIM
    blockdiag = (row < HEAD_DIM) == (col < HEAD_DIM)
    posf = row.astype(F32)
    dmat = []
    for hf, hb in lg_heads:
        dmat.append(jnp.where(rel >= 0, jnp.exp(jnp.where(rel >= 0, rel, 0.0) * hf), 0.0)
                    + jnp.where(rel <= 0, jnp.exp(jnp.where(rel <= 0, -rel, 0.0) * hb), 0.0))
    qdec_f = jnp.exp((posf + 1.0) * lgf)
    kdec_f = jnp.exp((CHUNK - 1.0 - posf) * lgf)
    qdec_b = jnp.exp((CHUNK - posf) * lgb)
    kdec_b = jnp.exp(posf * lgb)
    sdec_f = jnp.exp(CHUNK * lgf)
    sdec_b = jnp.exp(CHUNK * lgb)

    def load(c):
        rows = pl.ds(c * CHUNK, CHUNK)
        cs, sn = cos_ref[rows, :], sin_ref[rows, :]
        q = q_ref[rows, lanes]
        k = k_ref[rows, lanes]
        q = q * cs + _swap16(q) * sn
        k = k * cs + _swap16(k) * sn
        return q, k * (HEAD_DIM ** -0.5), v_ref[rows, lanes]

    def state_update(s, k, v, kdec, sdec):
        kd = (k * kdec).T.astype(BF16)
        u = jnp.dot(kd, v.astype(BF16), preferred_element_type=F32)
        return s * sdec + jnp.where(blockdiag, u, 0.0)

    h0, h1 = 2 * pp, 2 * pp + 1
    s_f = _block_diag(s0_ref[0, 0, 0, h0], s0_ref[0, 0, 0, h1])
    s_b = _block_diag(s0_ref[0, 0, 1, h0], s0_ref[0, 0, 1, h1])
    scr0 = pp * n_chunks
    for c in range(n_chunks):
        sf_scr[scr0 + c] = s_f
        _, k, v = load(c)
        s_f = state_update(s_f, k, v, kdec_f, sdec_f)
    for c in reversed(range(n_chunks)):
        sb_scr[scr0 + c] = s_b
        _, k, v = load(c)
        s_b = state_update(s_b, k, v, kdec_b, sdec_b)

    for c in range(n_chunks):
        q, k, v = load(c)
        qb, kb, vb = q.astype(BF16), k.astype(BF16), v.astype(BF16)
        outs = []
        for h in range(2):
            qh = jnp.where(lo_mask if h == 0 else ~lo_mask, qb, jnp.zeros_like(qb))
            a = lax.dot_general(qh, kb, NT_DIMS, preferred_element_type=F32) * dmat[h]
            outs.append(jnp.dot(a.astype(BF16), vb, preferred_element_type=F32))
        o = jnp.where(lo_mask, outs[0], outs[1])
        o = o + jnp.dot(qb, sf_scr[scr0 + c].astype(BF16), preferred_element_type=F32) * qdec_f
        o = o + jnp.dot(qb, sb_scr[scr0 + c].astype(BF16), preferred_element_type=F32) * qdec_b
        o2 = o * o
        ms0 = jnp.sum(jnp.where(lo_mask, o2, 0.0), axis=1, keepdims=True) * (1.0 / HEAD_DIM)
        ms1 = jnp.sum(jnp.where(lo_mask, 0.0, o2), axis=1, keepdims=True) * (1.0 / HEAD_DIM)
        inv = jnp.where(lo_mask, lax.rsqrt(ms0 + EPS), lax.rsqrt(ms1 + EPS))
        g = g_ref[pl.ds(c * CHUNK, CHUNK), lanes]
        o_ref[pl.ds(c * CHUNK, CHUNK), lanes] = (_silu(g) * (o * inv)).astype(BF16)


def _retention(z, log_g, rope, state_ret, layer):
    pps = RET_PAIRS_PER_STEP
    width = pps * LANES
    row_block0 = NP // DEC_SEQ
    scratch = pltpu.VMEM((pps * (DEC_SEQ // CHUNK), CHUNK, CHUNK), F32)

    def zspec(cb):
        return pl.BlockSpec((DEC_SEQ, width), lambda b, p: (row_block0 + b, cb // pps + p))

    table = pl.BlockSpec((DEC_SEQ, LANES), lambda b, p: (0, 0))
    return pl.pallas_call(
        _retention_kernel,
        name="retention",
        grid=(DEC_BATCH, N_PAIRS // pps),
        in_specs=[pl.BlockSpec(memory_space=pltpu.SMEM), zspec(0), zspec(4), zspec(8), zspec(12), table, table,
                  pl.BlockSpec((1, 1, 2, 2 * pps, HEAD_DIM, HEAD_DIM), lambda b, p: (b, layer, 0, p, 0, 0))],
        out_specs=pl.BlockSpec((DEC_SEQ, width), lambda b, p: (b, p)),
        out_shape=jax.ShapeDtypeStruct((NS, N_PAIRS * LANES), BF16),
        scratch_shapes=[scratch, scratch],
        compiler_params=_params(("arbitrary", "arbitrary"), VMEM_LIMIT),
    )(log_g, z, z, z, z, *rope, state_ret)


def _retention_ctx_kernel(lg_ref, q_ref, k_ref, v_ref, g_ref, o_ref, st_ref, decay_ref):
    heads = 2 * N_PAIRS

    @pl.when(pl.program_id(0) == 0)
    def _():
        rel = (lax.broadcasted_iota(I32, (SEQ, SEQ), 0) - lax.broadcasted_iota(I32, (SEQ, SEQ), 1)).astype(F32)
        for h in range(heads):
            decay_ref[h] = (jnp.where(rel >= 0, jnp.exp(jnp.where(rel >= 0, rel, 0.0) * lg_ref[0, h]), 0.0)
                            + jnp.where(rel <= 0, jnp.exp(jnp.where(rel <= 0, -rel, 0.0) * lg_ref[1, h]), 0.0))

    lane = lax.broadcasted_iota(I32, (SEQ, LANES), 1)
    lo_mask = lane < HEAD_DIM
    pos = lax.broadcasted_iota(I32, (SEQ, LANES), 0).astype(F32)
    blockdiag = ((lax.broadcasted_iota(I32, (LANES, LANES), 0) < HEAD_DIM)
                 == (lax.broadcasted_iota(I32, (LANES, LANES), 1) < HEAD_DIM))
    for pp in range(N_PAIRS):
        lanes = slice(pp * LANES, (pp + 1) * LANES)
        h0, h1 = 2 * pp, 2 * pp + 1
        k = k_ref[:, lanes] * (HEAD_DIM ** -0.5)
        qb, kb, vb = q_ref[:, lanes].astype(BF16), k.astype(BF16), v_ref[:, lanes].astype(BF16)
        outs = []
        for h, mask in ((h0, lo_mask), (h1, ~lo_mask)):
            qh = jnp.where(mask, qb, jnp.zeros_like(qb))
            a = lax.dot_general(qh, kb, NT_DIMS, preferred_element_type=F32) * decay_ref[h]
            outs.append(jnp.dot(a.astype(BF16), vb, preferred_element_type=F32))
        o = jnp.where(lo_mask, outs[0], outs[1])
        o2 = o * o
        ms0 = jnp.sum(jnp.where(lo_mask, o2, 0.0), axis=1, keepdims=True) * (1.0 / HEAD_DIM)
        ms1 = jnp.sum(jnp.where(lo_mask, 0.0, o2), axis=1, keepdims=True) * (1.0 / HEAD_DIM)
        inv = jnp.where(lo_mask, lax.rsqrt(ms0 + EPS), lax.rsqrt(ms1 + EPS))
        o_ref[:, lanes] = (_silu(g_ref[:, lanes]) * (o * inv)).astype(BF16)

        for d, age in ((0, SEQ - 1.0 - pos), (1, pos)):
            lg = jnp.where(lo_mask, lg_ref[d, h0], lg_ref[d, h1])
            kd = (k * jnp.exp(age * lg)).T.astype(BF16)
            s = jnp.where(blockdiag, jnp.dot(kd, vb, preferred_element_type=F32), 0.0)
            st_ref[0, d, h0] = s[:HEAD_DIM, :HEAD_DIM]
            st_ref[0, d, h1] = s[HEAD_DIM:, HEAD_DIM:]


def _retention_ctx(z, log_g):
    width = N_PAIRS * LANES
    heads = 2 * N_PAIRS
    zspec = lambda cb: pl.BlockSpec((SEQ, width), lambda b: (b, cb))
    return pl.pallas_call(
        _retention_ctx_kernel,
        name="retention_ctx",
        grid=(BATCH,),
        in_specs=[pl.BlockSpec(memory_space=pltpu.SMEM), zspec(0), zspec(1), zspec(2), zspec(3)],
        out_specs=[pl.BlockSpec((SEQ, width), lambda b: (b, 0)),
                   pl.BlockSpec((1, 2, heads, HEAD_DIM, HEAD_DIM), lambda b: (b, 0, 0, 0, 0))],
        out_shape=[jax.ShapeDtypeStruct((NP, width), BF16),
                   jax.ShapeDtypeStruct((BATCH, 2, heads, HEAD_DIM, HEAD_DIM), F32)],
        scratch_shapes=[pltpu.VMEM((heads, SEQ, SEQ), F32)],
        compiler_params=_params(("arbitrary",)),
    )(log_g, z, z, z, z)


def _rope_tables():
    t = np.arange(DEC_SEQ)
    posn = [(t // GRID_W).astype(np.float32), (t % GRID_W).astype(np.float32)]
    nf = HEAD_DIM // 4
    freqs = (1.0 / (np.float32(ROPE_BASE) ** (np.arange(nf, dtype=np.float32) / np.float32(nf)))).astype(np.float32)
    cos = np.zeros((DEC_SEQ, HEAD_DIM), np.float32)
    sin = np.zeros((DEC_SEQ, HEAD_DIM), np.float32)
    for half in range(2):
        ang = (posn[half][:, None] * freqs[None, :]).astype(np.float32)
        for grp in range(2):
            lo = half * 32 + grp * nf
            cos[:, lo:lo + nf] = np.cos(ang)
            sin[:, lo:lo + nf] = np.sin(ang) * (-1.0 if grp == 0 else 1.0)
    return jnp.asarray(np.tile(cos, (1, 2))), jnp.asarray(np.tile(sin, (1, 2)))


def _pool_kernel(u_ref, w_ref, sc_ref, o_ref, *, seq):
    padded = seq + 2 * POOL_PAD
    t = lax.broadcasted_iota(I32, (seq, 1), 0)
    zpad = jnp.zeros((POOL_PAD, LANES), F32)
    for gi, w in enumerate(POOL_WINDOWS):
        x = u_ref[:, gi * LANES:(gi + 1) * LANES]
        run = jnp.concatenate([zpad, x, zpad], axis=0)
        span = 1
        while span < w:
            run = run + pltpu.roll(run, padded - span, 0)
            span *= 2
        win = pltpu.roll(run, padded - (POOL_PAD - w // 2), 0)[:seq]
        cnt = (jnp.minimum(t + w // 2, seq) - jnp.maximum(t - w // 2, 0)).astype(F32)
        pooled = win / cnt - x
        mixed = jnp.dot(pooled.astype(BF16), w_ref[gi].astype(BF16), preferred_element_type=F32)
        o_ref[:, gi * LANES:(gi + 1) * LANES] = (mixed * sc_ref[:, gi * LANES:(gi + 1) * LANES]).astype(BF16)


def _pool(z, pool_w, pool_scale, *, nb, seq, row_block0):
    width = len(POOL_WINDOWS) * LANES
    return pl.pallas_call(
        functools.partial(_pool_kernel, seq=seq),
        name="pool",
        grid=(nb,),
        in_specs=[pl.BlockSpec((seq, width), lambda b: (row_block0 + b, 2048 // width)),
                  pl.BlockSpec((len(POOL_WINDOWS), LANES, LANES), lambda b: (0, 0, 0)),
                  pl.BlockSpec((1, width), lambda b: (0, 0))],
        out_specs=pl.BlockSpec((seq, width), lambda b: (b, 0)),
        out_shape=jax.ShapeDtypeStruct((nb * seq, width), BF16),
        compiler_params=_params(("arbitrary",), VMEM_LIMIT),
    )(z, pool_w, pool_scale.reshape(1, width))


def _head_select(h, shape):
    lane = lax.broadcasted_iota(I32, shape, 1)
    return (lane < HEAD_DIM) if h == 0 else (lane >= HEAD_DIM)


def _ctx_attn_kernel(q_ref, k_ref, v_ref, o_ref):
    for pp in range(N_PAIRS):
        lanes = slice(pp * LANES, (pp + 1) * LANES)
        qb = (q_ref[:, lanes] * (HEAD_DIM ** -0.5)).astype(BF16)
        kb = k_ref[:, lanes].astype(BF16)
        vb = v_ref[:, lanes].astype(BF16)
        outs = []
        for h in range(2):
            qh = jnp.where(_head_select(h, qb.shape), qb, jnp.zeros_like(qb))
            s = lax.dot_general(qh, kb, NT_DIMS, preferred_element_type=F32)
            p = jnp.exp(s - jnp.max(s, axis=1, keepdims=True))
            denom = jnp.sum(p, axis=1, keepdims=True)
            outs.append(jnp.dot(p.astype(BF16), vb, preferred_element_type=F32) / denom)
        o_ref[:, lanes] = jnp.where(_head_select(0, outs[0].shape), outs[0], outs[1]).astype(BF16)


def _ctx_attention(z):
    width = N_PAIRS * LANES

    def zspec(cb):
        return pl.BlockSpec((SEQ, width), lambda b: (b, cb))

    return pl.pallas_call(
        _ctx_attn_kernel,
        name="ctx_attn",
        grid=(BATCH,),
        in_specs=[zspec(5), zspec(6), zspec(7)],
        out_specs=pl.BlockSpec((SEQ, width), lambda b: (b, 0)),
        out_shape=jax.ShapeDtypeStruct((NP, width), BF16),
        compiler_params=_params(("arbitrary",)),
    )(z, z, z)


NA_QROWS = 4
NA_QBLK = NA_QROWS * GRID_W
NA_KROWS = 12
NA_NBLK = DEC_SEQ // NA_QBLK


def _na_key_block(i):
    return jnp.clip(i - 1, 0, NA_NBLK - 3)


NA_PATTERNS = ((0, 0), (NA_QROWS, 0), (DEC_SEQ // GRID_W - NA_QROWS, DEC_SEQ // GRID_W - NA_KROWS))
NA_DX_LANE = GRID_W - (NA_KW - 1)


def _na_bias_kernel(rpb_ref, o_ref):
    rows = DEC_SEQ // GRID_W
    q = lax.broadcasted_iota(I32, (GRID_W, LANES), 0)
    lane = lax.broadcasted_iota(I32, (GRID_W, LANES), 1)
    c = lane % GRID_W
    c_start = jnp.clip(q - NA_KW // 2, 0, GRID_W - NA_KW)
    col_ok = (c >= c_start) & (c < c_start + NA_KW)
    lower = lane < GRID_W
    for p, (r0, ks) in enumerate(NA_PATTERNS):
        for rr in range(NA_QROWS):
            r = r0 + rr
            start = min(max(r - NA_KH // 2, 0), rows - NA_KH)
            for kp in range(NA_KROWS // 2):
                halves = []
                for half in range(2):
                    kr = ks + 2 * kp + half
                    if start <= kr < start + NA_KH:
                        row = jnp.broadcast_to(rpb_ref[0, 0, pl.ds(kr - r + NA_KH - 1, 1), :], (GRID_W, LANES))
                        halves.append(pltpu.roll(row, GRID_W * (1 - half), 1, stride=1, stride_axis=0))
                    else:
                        halves.append(None)
                neg = jnp.full((GRID_W, LANES), -jnp.inf, F32)
                lo_half = neg if halves[0] is None else jnp.where(col_ok, halves[0], neg)
                hi_half = neg if halves[1] is None else jnp.where(col_ok, halves[1], neg)
                o_ref[0, p, 0, rr * GRID_W:(rr + 1) * GRID_W, kp * LANES:(kp + 1) * LANES] = (
                    jnp.where(lower, lo_half, hi_half))


def _na_bias(na_rpb):
    ny, nx = 2 * NA_KH - 1, 2 * NA_KW - 1
    padded = jnp.pad(na_rpb.astype(F32), ((0, 0), (0, 0), (0, 16 - ny), (NA_DX_LANE, LANES - NA_DX_LANE - nx)))
    heads = 2 * N_PAIRS
    return pl.pallas_call(
        _na_bias_kernel,
        name="nbr_bias",
        grid=(DEPTH, heads),
        in_specs=[pl.BlockSpec((1, 1, 16, LANES), lambda l, h: (l, h, 0, 0))],
        out_specs=pl.BlockSpec((1, len(NA_PATTERNS), 1, NA_QBLK, NA_KROWS * GRID_W), lambda l, h: (l, 0, h, 0, 0)),
        out_shape=jax.ShapeDtypeStruct((DEPTH, len(NA_PATTERNS), heads, NA_QBLK, NA_KROWS * GRID_W), F32),
        compiler_params=_params(("arbitrary", "arbitrary")),
    )(padded)


def _na_kernel(q_ref, k0_ref, k1_ref, k2_ref, v0_ref, v1_ref, v2_ref, ck_ref, cv_ref, bias_ref, o_ref):
    qb = (q_ref[...] * (HEAD_DIM ** -0.5)).astype(BF16)
    ks = [r[...].astype(BF16) for r in (k0_ref, k1_ref, k2_ref)] + [ck_ref[0, 0].astype(BF16)]
    vs = [r[...].astype(BF16) for r in (v0_ref, v1_ref, v2_ref)] + [cv_ref[0, 0].astype(BF16)]
    outs = []
    for h in range(2):
        qh = jnp.where(_head_select(h, qb.shape), qb, jnp.zeros_like(qb))
        ss = []
        for j in range(4):
            s = lax.dot_general(qh, ks[j], NT_DIMS, preferred_element_type=F32)
            if j < 3:
                s = s + bias_ref[0, h, :, j * NA_QBLK:(j + 1) * NA_QBLK]
            ss.append(s)
        m = functools.reduce(jnp.maximum, [jnp.max(s, axis=1, keepdims=True) for s in ss])
        ps = [jnp.exp(s - m) for s in ss]
        denom = functools.reduce(jnp.add, [jnp.sum(p, axis=1, keepdims=True) for p in ps])
        acc = functools.reduce(jnp.add, [jnp.dot(p.astype(BF16), v, preferred_element_type=F32)
                                         for p, v in zip(ps, vs)])
        outs.append(acc / denom)
    o_ref[...] = jnp.where(_head_select(0, outs[0].shape), outs[0], outs[1]).astype(BF16)


def _neighbourhood_attention(z, bias, cache_k, cache_v, layer):
    base = NP // NA_QBLK

    def qspec():
        return pl.BlockSpec((NA_QBLK, LANES), lambda b, i, p: (base + b * NA_NBLK + i, 20 + p))

    def kvspec(cb, j):
        return pl.BlockSpec((NA_QBLK, LANES),
                            lambda b, i, p: (base + b * NA_NBLK + _na_key_block(i) + j, cb + p))

    def cspec():
        return pl.BlockSpec((1, 1, SEQ, LANES), lambda b, i, p: (b, layer, 0, p))

    pattern = lambda i: jnp.where(i == 0, 0, jnp.where(i == NA_NBLK - 1, 2, 1))
    ck = cache_k.reshape(DEC_BATCH, DEPTH, SEQ, N_PAIRS * LANES)
    cv = cache_v.reshape(DEC_BATCH, DEPTH, SEQ, N_PAIRS * LANES)
    return pl.pallas_call(
        _na_kernel,
        name="nbr_attn",
        grid=(DEC_BATCH, NA_NBLK, N_PAIRS),
        in_specs=[qspec()] + [kvspec(24, j) for j in range(3)] + [kvspec(28, j) for j in range(3)]
                 + [cspec(), cspec(),
                    pl.BlockSpec((None, 1, 2, NA_QBLK, NA_KROWS * GRID_W),
                                 lambda b, i, p: (layer, pattern(i), p, 0, 0))],
        out_specs=pl.BlockSpec((NA_QBLK, LANES), lambda b, i, p: (b * NA_NBLK + i, p)),
        out_shape=jax.ShapeDtypeStruct((NS, N_PAIRS * LANES), BF16),
        compiler_params=_params(("arbitrary", "arbitrary", "arbitrary")),
    )(z, z, z, z, z, z, z, ck, cv, bias)


def _merge_kernel(rp_ref, rs_ref, pp_ref, ps_ref, ap_ref, as_ref, xp_ref, xs_ref, g0_ref, g1_ref, g2_ref,
                  wr_ref, wp_ref, wa_ref, wo_ref, m_ref, ln_ref, o_ref, wrb, wpb, wab, wob):
    @pl.when(pl.program_id(0) == 0)
    def _():
        wrb[...] = wr_ref[...].astype(BF16)
        wpb[...] = wp_ref[...].astype(BF16)
        wab[...] = wa_ref[...].astype(BF16)
        wob[...] = wo_ref[...].astype(BF16)

    branch = lambda p_ref, s_ref, w: jnp.dot(_pick(p_ref, s_ref), w[...], preferred_element_type=F32)
    merged = (jax.nn.sigmoid(g0_ref[...]) * branch(rp_ref, rs_ref, wrb)
              + jax.nn.sigmoid(g1_ref[...]) * branch(pp_ref, ps_ref, wpb)
              + jax.nn.sigmoid(g2_ref[...]) * branch(ap_ref, as_ref, wab))
    mix = jnp.dot(merged.astype(BF16), wob[...], preferred_element_type=F32)
    o_ref[...] = _pick(xp_ref, xs_ref) + m_ref[0, 2:3, :] * (_rms(mix) * ln_ref[...])


def _merge(ret_pair, pool_pair, na_pair, x_pair, z, w_ret_o, w_pool_o, w_na_o, w_o, mod_l, ln, layer):
    tm = 512
    half = N_PAIRS * LANES
    row = lambda i: (i, 0)
    const = lambda i: (0, 0)
    slab = lambda i: (layer, 0, 0)
    return pl.pallas_call(
        _merge_kernel,
        name="merge",
        grid=(NT // tm,),
        in_specs=_pair_specs(tm, half) * 3 + _pair_specs(tm, D)
                 + [pl.BlockSpec((tm, D), lambda i, c=c: (i, 4 + c)) for c in range(3)]
                 + [pl.BlockSpec((None, half, D), slab)] * 3
                 + [pl.BlockSpec((None, D, D), slab),
                    pl.BlockSpec((1, N_MOD, D), lambda i: (_mod_row(i * tm), 0, 0)),
                    pl.BlockSpec((1, D), const)],
        out_specs=pl.BlockSpec((tm, D), row),
        out_shape=jax.ShapeDtypeStruct((NT, D), F32),
        scratch_shapes=[pltpu.VMEM((half, D), BF16)] * 3 + [pltpu.VMEM((D, D), BF16)],
        compiler_params=_params(("arbitrary",), VMEM_LIMIT),
    )(*ret_pair, *pool_pair, *na_pair, *x_pair, z, z, z, w_ret_o, w_pool_o, w_na_o, w_o, mod_l, ln.reshape(1, D))


def _ffnprep_kernel(x_ref, m_ref, ln_ref, wr_ref, h_ref, aff_ref):
    y = _rms(x_ref[...]) * ln_ref[...]
    h = y * (1.0 + m_ref[0, 4:5, :]) + m_ref[0, 3:4, :]
    hb = h.astype(BF16)
    h_ref[...] = hb
    hl = (h - hb.astype(F32)).astype(BF16)
    w = wr_ref[...]
    wb = w.astype(BF16)
    wl = (w - wb.astype(F32)).astype(BF16)
    logits = (lax.dot_general(wb, hb, NT_DIMS, preferred_element_type=F32)
              + lax.dot_general(wb, hl, NT_DIMS, preferred_element_type=F32)
              + lax.dot_general(wl, hb, NT_DIMS, preferred_element_type=F32))
    e = jnp.exp(logits - jnp.max(logits, axis=0, keepdims=True))
    aff = e / jnp.sum(e, axis=0, keepdims=True)
    for j in range(aff_ref.shape[0]):
        aff_ref[j] = aff[:, j * LANES:(j + 1) * LANES]


def _ffnprep(x_all, mod_l, ln, w_router_t):
    tm = 512
    return pl.pallas_call(
        _ffnprep_kernel,
        name="ffn_prep",
        grid=(NT // tm,),
        in_specs=[pl.BlockSpec((tm, D), lambda i: (i, 0)),
                  pl.BlockSpec((1, N_MOD, D), lambda i: (_mod_row(i * tm), 0, 0)),
                  pl.BlockSpec((1, D), lambda i: (0, 0)),
                  pl.BlockSpec((N_EXPERTS, D), lambda i: (0, 0))],
        out_specs=[pl.BlockSpec((tm, D), lambda i: (i, 0)),
                   pl.BlockSpec((tm // LANES, N_EXPERTS, LANES), lambda i: (i, 0, 0))],
        out_shape=[jax.ShapeDtypeStruct((NT, D), BF16),
                   jax.ShapeDtypeStruct((NT // LANES, N_EXPERTS, LANES), F32)],
        compiler_params=_params(("arbitrary",)),
    )(x_all, mod_l, ln.reshape(1, D), w_router_t)


def _route_kernel(aff_ref, slot_ref, offs_ref, ceq_ref, csel_ref, *, cap, nblk):
    as_float = lambda bits: lax.bitcast_convert_type(bits, F32)

    def count(pred):
        return jnp.sum(jnp.sum(jnp.where(pred, 1.0, 0.0), axis=0), axis=1, keepdims=True)

    def search(_, lohi):
        lo, hi = lohi
        mid = lo + ((hi - lo + 1) >> 1)
        ok = count(aff_ref[...] >= as_float(mid)[None]) >= cap
        return jnp.where(ok, mid, lo), jnp.where(ok, hi, mid - 1)

    lo0 = jnp.zeros((N_EXPERTS, 1), I32)
    hi0 = jnp.full((N_EXPERTS, 1), 0x7F800000, I32)
    thr_bits, _ = lax.fori_loop(0, 31, search, (lo0, hi0))
    thr = as_float(thr_bits)
    need = cap - count(aff_ref[...] > thr[None])

    upper = (lax.broadcasted_iota(I32, (LANES, LANES), 0)
             < lax.broadcasted_iota(I32, (LANES, LANES), 1)).astype(BF16)
    lane = lax.broadcasted_iota(I32, (N_EXPERTS, LANES), 1)

    ceq_ref[...] = jnp.zeros_like(ceq_ref)
    csel_ref[...] = jnp.zeros_like(csel_ref)
    offs_ref[...] = jnp.zeros_like(offs_ref)

    def block(b, carry):
        c_eq = ceq_ref[...]
        c_sel = csel_ref[...]
        aff = aff_ref[b]
        eq = aff == thr
        eqf = jnp.where(eq, 1.0, 0.0)
        eq_rank = jnp.dot(eqf.astype(BF16), upper, preferred_element_type=F32) + c_eq
        sel = (aff > thr) | (eq & (eq_rank < need))
        self_ = jnp.where(sel, 1.0, 0.0)
        rank = jnp.dot(self_.astype(BF16), upper, preferred_element_type=F32) + c_sel
        slot_ref[b] = jnp.where(sel, rank, -1.0)
        offs_ref[...] = jnp.where(lane == b, c_sel.astype(I32), offs_ref[...])
        ceq_ref[...] = c_eq + jnp.sum(eqf, axis=1, keepdims=True)
        csel_ref[...] = c_sel + jnp.sum(self_, axis=1, keepdims=True)
        return carry

    lax.fori_loop(0, nblk, block, 0)
    offs_ref[...] = jnp.where(lane >= nblk, csel_ref[...].astype(I32), offs_ref[...])


def _route(aff_blocks, *, blk0, nblk, cap):
    return pl.pallas_call(
        functools.partial(_route_kernel, cap=cap, nblk=nblk),
        name="route",
        grid=(1,),
        in_specs=[pl.BlockSpec((nblk, N_EXPERTS, LANES), lambda i: (blk0 // nblk, 0, 0))],
        out_specs=[pl.BlockSpec((nblk, N_EXPERTS, LANES), lambda i: (0, 0, 0)),
                   pl.BlockSpec((N_EXPERTS, LANES), lambda i: (0, 0))],
        out_shape=[jax.ShapeDtypeStruct((nblk, N_EXPERTS, LANES), F32),
                   jax.ShapeDtypeStruct((N_EXPERTS, LANES), I32)],
        scratch_shapes=[pltpu.VMEM((N_EXPERTS, LANES), F32)] * 2,
        compiler_params=_params(("arbitrary",)),
    )(aff_blocks)


GATHER_TOKENS = 256
TILE_BLOCKS = GATHER_TOKENS // LANES
SLOT_ALIGN = 16
WINDOW = LANES // 2
ROUND_SLOTS = WINDOW - SLOT_ALIGN
GATE_TERMS = 3
ROW_W = D + LANES


def _round_bounds(offs_ref, e, t, r):
    off0 = offs_ref[e, TILE_BLOCKS * t]
    off1 = offs_ref[e, TILE_BLOCKS * t + TILE_BLOCKS]
    lo = jnp.minimum(off0 + ROUND_SLOTS * r, off1)
    hi = jnp.minimum(lo + ROUND_SLOTS, off1)
    return lo, hi, pl.multiple_of(lo & -SLOT_ALIGN, SLOT_ALIGN)


def _n_rounds(offs_ref, t):
    most = jnp.int32(0)
    for e in range(N_EXPERTS):
        most = jnp.maximum(most, offs_ref[e, TILE_BLOCKS * t + TILE_BLOCKS] - offs_ref[e, TILE_BLOCKS * t])
    rounds = jnp.int32(0)
    for filled in range(0, GATHER_TOKENS, ROUND_SLOTS):
        rounds = rounds + (most > filled).astype(I32)
    return rounds


def _gate_terms(aff_cols):
    hi = aff_cols.astype(BF16)
    rest = aff_cols - hi.astype(F32)
    mid = rest.astype(BF16)
    lo = (rest - mid.astype(F32)).astype(BF16)
    terms = jnp.stack([hi, mid, lo], axis=-1).reshape(aff_cols.shape[0], GATE_TERMS * N_EXPERTS)
    return jnp.pad(terms, ((0, 0), (0, LANES - GATE_TERMS * N_EXPERTS)))


def _gather_kernel(offs_ref, slot_ref, h_ref, g_ref, xe_hbm, stage_ref, carry_ref, sem, nround_ref, *, n_tiles):
    t = pl.program_id(0)
    cap = xe_hbm.shape[1] - WINDOW

    def out_copy(buf, e, start):
        return pltpu.make_async_copy(stage_ref.at[buf, e], xe_hbm.at[e, pl.ds(start, WINDOW)], sem.at[buf])

    def wait_round(buf):
        for e in range(N_EXPERTS):
            out_copy(buf, e, 0).wait()

    @pl.when(t == 0)
    def _():
        carry_ref[...] = jnp.zeros_like(carry_ref)
        nround_ref[0] = 0
        stage_ref[0, 0] = jnp.zeros((WINDOW, ROW_W), BF16)
        for e in range(N_EXPERTS):
            pltpu.make_async_copy(stage_ref.at[0, 0], xe_hbm.at[e, pl.ds(cap, WINDOW)], sem.at[0]).start()
        wait_round(0)

    hb = jnp.concatenate([h_ref[...], g_ref[...]], axis=1)
    sub = lax.broadcasted_iota(I32, (WINDOW, GATHER_TOKENS), 0).astype(F32)

    def one_round(r, carry):
        done = nround_ref[0]
        buf = done & 1
        bounds = [_round_bounds(offs_ref, e, t, r) for e in range(N_EXPERTS)]
        onehots = []
        for e in range(N_EXPERTS):
            lo, hi, start = bounds[e]
            srow = jnp.concatenate([slot_ref[j, e:e + 1, :] for j in range(TILE_BLOCKS)], axis=1)
            hit = ((srow - start.astype(F32) == sub) & (srow >= lo.astype(F32)) & (srow < hi.astype(F32)))
            onehots.append(jnp.where(hit, 1.0, 0.0).astype(BF16))
        rows = jnp.dot(jnp.concatenate(onehots, axis=0), hb, preferred_element_type=F32)
        for e in range(N_EXPERTS):
            lo, hi, start = bounds[e]
            piece = rows[e * WINDOW:(e + 1) * WINDOW]
            head = piece[:SLOT_ALIGN] + carry_ref[e].astype(F32)
            stage_ref[buf, e, :SLOT_ALIGN, :] = head.astype(BF16)
            stage_ref[buf, e, SLOT_ALIGN:, :] = piece[SLOT_ALIGN:].astype(BF16)
            tail = pl.multiple_of((hi & -SLOT_ALIGN) - start, SLOT_ALIGN)
            carry_ref[e] = stage_ref[buf, e, pl.ds(tail, SLOT_ALIGN), :]

        @pl.when(done > 0)
        def _():
            wait_round(1 - buf)

        for e in range(N_EXPERTS):
            out_copy(buf, e, bounds[e][2]).start()
        nround_ref[0] = done + 1
        return carry

    lax.fori_loop(0, _n_rounds(offs_ref, t), one_round, 0)

    @pl.when((t == n_tiles - 1) & (nround_ref[0] > 0))
    def _():
        wait_round((nround_ref[0] - 1) & 1)


def _gather(offs, slot, h_all, gate_terms, *, row0, n, cap):
    n_tiles = n // GATHER_TOKENS
    tile0 = row0 // GATHER_TOKENS
    return pl.pallas_call(
        functools.partial(_gather_kernel, n_tiles=n_tiles),
        name="gather",
        grid_spec=pltpu.PrefetchScalarGridSpec(
            num_scalar_prefetch=1,
            grid=(n_tiles,),
            in_specs=[pl.BlockSpec((TILE_BLOCKS, N_EXPERTS, LANES), lambda t, o: (t, 0, 0)),
                      pl.BlockSpec((GATHER_TOKENS, D), lambda t, o: (tile0 + t, 0)),
                      pl.BlockSpec((GATHER_TOKENS, LANES), lambda t, o: (t, 0))],
            out_specs=pl.BlockSpec(memory_space=pl.ANY),
            scratch_shapes=[pltpu.VMEM((2, N_EXPERTS, WINDOW, ROW_W), BF16),
                            pltpu.VMEM((N_EXPERTS, SLOT_ALIGN, ROW_W), BF16),
                            pltpu.SemaphoreType.DMA((2,)),
                            pltpu.SMEM((1,), I32)]),
        out_shape=jax.ShapeDtypeStruct((N_EXPERTS, cap + WINDOW, ROW_W), BF16),
        compiler_params=_params(("arbitrary",), VMEM_LIMIT),
    )(offs, slot, h_all, gate_terms)


FF_CHUNK = 512


def _experts_kernel(xp_ref, xs_ref, wg_ref, wu_ref, wd_ref, yp_ref, ys_ref, accp_ref, accs_ref, *, n_f):
    f = pl.program_id(1)
    wg = wg_ref[0].astype(BF16)
    wu = wu_ref[0].astype(BF16)
    wd = wd_ref[0].astype(BF16)

    def ffn(x):
        a = jnp.dot(x, wg, preferred_element_type=F32)
        b = jnp.dot(x, wu, preferred_element_type=F32)
        return jnp.dot((_silu(a) * b).astype(BF16), wd, preferred_element_type=F32)

    yp = ffn(xp_ref[0, :, :D])
    ys = ffn(xs_ref[0, :, :D])

    @pl.when(f == 0)
    def _():
        accp_ref[...] = yp
        accs_ref[...] = ys

    @pl.when(f > 0)
    def _():
        accp_ref[...] += yp
        accs_ref[...] += ys

    @pl.when(f == n_f - 1)
    def _():
        first = GATE_TERMS * pl.program_id(0)
        for x_ref, y_ref, acc_ref in ((xp_ref, yp_ref, accp_ref), (xs_ref, ys_ref, accs_ref)):
            cap = acc_ref.shape[0]
            lane = lax.broadcasted_iota(I32, (cap, LANES), 1)
            mine = (lane >= first) & (lane < first + GATE_TERMS)
            gate = jnp.sum(jnp.where(mine, x_ref[0, :, D:].astype(F32), 0.0), axis=1, keepdims=True)
            y_ref[0, :cap, :] = (acc_ref[...] * gate).astype(BF16)
            y_ref[0, cap:, :] = jnp.zeros((WINDOW, D), BF16)


def _experts(xe_p, xe_s, w_gate, w_up, w_down, layer):
    n_f = EXPERT_FF // FF_CHUNK
    cap_p, cap_s = xe_p.shape[1] - WINDOW, xe_s.shape[1] - WINDOW
    spec = lambda rows, width: pl.BlockSpec((1, rows, width), lambda e, f: (e, 0, 0))
    out = lambda cap: jax.ShapeDtypeStruct((N_EXPERTS, cap + WINDOW, D), BF16)
    return pl.pallas_call(
        functools.partial(_experts_kernel, n_f=n_f),
        name="experts",
        grid=(N_EXPERTS, n_f),
        in_specs=[spec(cap_p, ROW_W), spec(cap_s, ROW_W),
                  pl.BlockSpec((None, 1, D, FF_CHUNK), lambda e, f: (layer, e, 0, f)),
                  pl.BlockSpec((None, 1, D, FF_CHUNK), lambda e, f: (layer, e, 0, f)),
                  pl.BlockSpec((None, 1, FF_CHUNK, D), lambda e, f: (layer, e, f, 0))],
        out_specs=[spec(cap_p + WINDOW, D), spec(cap_s + WINDOW, D)],
        out_shape=[out(cap_p), out(cap_s)],
        scratch_shapes=[pltpu.VMEM((cap_p, D), F32), pltpu.VMEM((cap_s, D), F32)],
        compiler_params=_params(("arbitrary", "arbitrary"), VMEM_LIMIT),
    )(xe_p, xe_s, w_gate, w_up, w_down)


def _combine_kernel(offs_ref, slot_ref, x_ref, m_ref, ln_ref, ye_hbm, o_ref, stage_ref, sem, *, n_tiles):
    t = pl.program_id(0)
    buf = t & 1

    def in_copy(b, e, start):
        return pltpu.make_async_copy(ye_hbm.at[e, pl.ds(start, WINDOW)],
                                     stage_ref.at[b, pl.ds(e * WINDOW, WINDOW)], sem.at[b])

    def start_round(b, tile, r):
        for e in range(N_EXPERTS):
            in_copy(b, e, _round_bounds(offs_ref, e, tile, r)[2]).start()

    def wait_round(b):
        for e in range(N_EXPERTS):
            in_copy(b, e, 0).wait()

    @pl.when(t == 0)
    def _():
        start_round(0, 0, 0)

    wait_round(buf)

    @pl.when(t + 1 < n_tiles)
    def _():
        start_round(1 - buf, t + 1, 0)

    slot = slot_ref[...]
    lane = lax.broadcasted_iota(I32, (GATHER_TOKENS, LANES), 1)
    first = lane < WINDOW
    row_in_window = (lane % WINDOW).astype(F32)

    def token_rows(b, r):
        onehots = []
        for e in range(0, N_EXPERTS, 2):
            (lo0, hi0, st0), (lo1, hi1, st1) = (_round_bounds(offs_ref, e + i, t, r) for i in range(2))
            s = jnp.where(first, slot[:, e:e + 1], slot[:, e + 1:e + 2])
            lo = jnp.where(first, lo0, lo1).astype(F32)
            hi = jnp.where(first, hi0, hi1).astype(F32)
            start = jnp.where(first, st0, st1).astype(F32)
            hit = (s - start == row_in_window) & (s >= lo) & (s < hi)
            onehots.append(jnp.where(hit, 1.0, 0.0).astype(BF16))
        return jnp.dot(jnp.concatenate(onehots, axis=1), stage_ref[b], preferred_element_type=F32)

    o_ref[...] = token_rows(buf, 0)

    def extra_round(r, carry):
        start_round(buf, t, r)
        wait_round(buf)
        o_ref[...] += token_rows(buf, r)
        return carry

    lax.fori_loop(1, _n_rounds(offs_ref, t), extra_round, 0)
    o_ref[...] = x_ref[...] + m_ref[0, 5:6, :] * (_rms(o_ref[...]) * ln_ref[...])


def _combine(offs, ye, slot_cols, x_all, mod_l, ln, *, row0, n):
    n_tiles = n // GATHER_TOKENS
    tile0 = row0 // GATHER_TOKENS
    return pl.pallas_call(
        functools.partial(_combine_kernel, n_tiles=n_tiles),
        name="combine",
        grid_spec=pltpu.PrefetchScalarGridSpec(
            num_scalar_prefetch=1,
            grid=(n_tiles,),
            in_specs=[pl.BlockSpec((GATHER_TOKENS, N_EXPERTS), lambda t, o: (t, 0)),
                      pl.BlockSpec((GATHER_TOKENS, D), lambda t, o: (tile0 + t, 0)),
                      pl.BlockSpec((1, N_MOD, D), lambda t, o: (_mod_row(row0 + t * GATHER_TOKENS), 0, 0)),
                      pl.BlockSpec((1, D), lambda t, o: (0, 0)),
                      pl.BlockSpec(memory_space=pl.ANY)],
            out_specs=pl.BlockSpec((GATHER_TOKENS, D), lambda t, o: (t, 0)),
            scratch_shapes=[pltpu.VMEM((2, N_EXPERTS * WINDOW, D), BF16),
                            pltpu.SemaphoreType.DMA((2,))]),
        out_shape=jax.ShapeDtypeStruct((n, D), F32),
        compiler_params=_params(("arbitrary",), VMEM_LIMIT),
    )(offs, slot_cols, x_all, mod_l, ln.reshape(1, D), ye)


def _token_major(blocks):
    return blocks.transpose(0, 2, 1).reshape(-1, N_EXPERTS)


def kernel(x_prompt, x_sample, cache_k, cache_v, state_ret, c, c_ctx, w_mod, b_mod, ln_pre_mix, ln_post_mix,
           ln_pre_ffn, ln_post_ffn, w_in, ret_decay, pool_w, pool_scale, na_rpb, w_ret_o, w_pool_o, w_na_o, w_o,
           w_router, w_gate, w_up, w_down):
    cvecs = jnp.zeros((8, D), F32).at[0].set(c_ctx).at[1:1 + DEC_BATCH].set(c)
    mod = _modulation(cvecs, w_mod, b_mod)
    rope = _rope_tables()
    na_bias = _na_bias(na_rpb)
    x_pair = (x_prompt.reshape(NP, D), x_sample.reshape(NS, D))
    h_layers, new_s = [], []
    groups = ((0, NP, NP // N_EXPERTS * 2), (NP, NS, NS // N_EXPERTS * 2))

    for l in range(DEPTH):
        mod_l = mod[l]
        h_all = _prenorm(x_pair, mod_l, ln_pre_mix[l])
        h_layers.append(h_all)
        z = _in_proj(h_all, w_in, l)
        log_g = jax.nn.log_sigmoid(ret_decay[l].astype(F32))

        ret_p, st = _retention_ctx(z, log_g)
        ret_s = _retention(z, log_g, rope, state_ret, l)
        pool_p = _pool(z, pool_w[l], pool_scale[l], nb=BATCH, seq=SEQ, row_block0=0)
        pool_s = _pool(z, pool_w[l], pool_scale[l], nb=DEC_BATCH, seq=DEC_SEQ, row_block0=NP // DEC_SEQ)
        na_p = _ctx_attention(z)
        na_s = _neighbourhood_attention(z, na_bias, cache_k, cache_v, l)
        x_mid = _merge((ret_p, ret_s), (pool_p, pool_s), (na_p, na_s), x_pair, z,
                       w_ret_o, w_pool_o, w_na_o, w_o, mod_l, ln_post_mix[l], l)
        new_s.append(st)

        h2, aff = _ffnprep(x_mid, mod_l, ln_pre_ffn[l], w_router[l].T)
        routed = []
        for row0, n, cap in groups:
            slot, offs = _route(aff, blk0=row0 // LANES, nblk=n // LANES, cap=cap)
            gate_terms = _gate_terms(_token_major(aff[row0 // LANES:(row0 + n) // LANES]))
            routed.append((slot, offs, _gather(offs, slot, h2, gate_terms, row0=row0, n=n, cap=cap)))
        ye = _experts(routed[0][2], routed[1][2], w_gate, w_up, w_down, l)
        outs = []
        for (row0, n, cap), (slot, offs, _), y in zip(groups, routed, ye):
            outs.append(_combine(offs, y, _token_major(slot), x_mid, mod_l, ln_post_ffn[l], row0=row0, n=n))
        x_pair = tuple(outs)

    y_prompt = x_pair[0].reshape(BATCH, SEQ, D)
    y_sample = x_pair[1].reshape(DEC_BATCH, DEC_SEQ, D)
    new_k, new_v = _kv_proj(h_layers, w_in)
    return (y_prompt, y_sample, new_k, new_v, jnp.stack(new_s, axis=1))
```

```python
import functools

import numpy as np
import jax
import jax.numpy as jnp
from jax import lax
from jax.experimental import pallas as pl
from jax.experimental.pallas import tpu as pltpu

F32 = jnp.float32
BF16 = jnp.bfloat16
I32 = jnp.int32

D = 1024
BATCH, SEQ = 32, 256
DEC_BATCH, DEC_SEQ = 2, 2048
DEPTH = 2
NP = BATCH * SEQ
NS = DEC_BATCH * DEC_SEQ
NT = NP + NS
GRID_W = 64
N_MOD = 6
EPS = 1e-6
ROPE_BASE = 10000.0
HEAD_DIM = 64
N_PAIRS = 4
CHUNK = 128
POOL_WINDOWS = (2, 4, 8, 16)
POOL_PAD = 16
NA_KH, NA_KW = 8, 16
N_EXPERTS = 16
EXPERT_FF = 2048
IN_COLS = 7168
LANES = 128
VMEM_LIMIT = 56 * 1024 * 1024

NT_DIMS = (((1,), (1,)), ((), ()))


def _params(sem, vmem=None):
    return pltpu.CompilerParams(dimension_semantics=sem, vmem_limit_bytes=vmem)


def _mod_row(row_start):
    return jnp.where(row_start < NP, 0, 1 + (row_start - NP) // DEC_SEQ)


def _silu(x):
    return x * jax.nn.sigmoid(x)


def _rms(x):
    return x * lax.rsqrt(jnp.mean(x * x, axis=-1, keepdims=True) + EPS)


def _pair_specs(tm, width):
    n_p = NP // tm
    return [pl.BlockSpec((tm, width), lambda i: (jnp.minimum(i, n_p - 1), 0)),
            pl.BlockSpec((tm, width), lambda i: (jnp.maximum(i - n_p, 0), 0))]


def _pick(p_ref, s_ref):
    return jnp.where(pl.program_id(0) < NP // p_ref.shape[0], p_ref[...], s_ref[...])


def _mod_kernel(c_ref, w_ref, b_ref, o_ref):
    a = _silu(c_ref[...]).astype(BF16)
    o_ref[0] = jnp.dot(a, w_ref[0].astype(BF16), preferred_element_type=F32) + b_ref[0]


def _modulation(cvecs, w_mod, b_mod):
    out = pl.pallas_call(
        _mod_kernel,
        name="modulation",
        grid=(DEPTH, N_MOD),
        in_specs=[pl.BlockSpec((8, D), lambda l, j: (0, 0)),
                  pl.BlockSpec((1, D, D), lambda l, j: (l, 0, j)),
                  pl.BlockSpec((1, 1, D), lambda l, j: (l, 0, j))],
        out_specs=pl.BlockSpec((1, 8, D), lambda l, j: (l, 0, j)),
        out_shape=jax.ShapeDtypeStruct((DEPTH, 8, N_MOD * D), F32),
        compiler_params=_params(("arbitrary", "arbitrary")),
    )(cvecs, w_mod, b_mod.reshape(DEPTH, 1, N_MOD * D))
    return out.reshape(DEPTH, 8, N_MOD, D)


def _prenorm_kernel(xp_ref, xs_ref, m_ref, ln_ref, h_ref):
    y = _rms(_pick(xp_ref, xs_ref)) * ln_ref[...]
    h_ref[...] = (y * (1.0 + m_ref[0, 1:2, :]) + m_ref[0, 0:1, :]).astype(BF16)


def _prenorm(x_pair, mod_l, ln):
    tm = 512
    return pl.pallas_call(
        _prenorm_kernel,
        name="prenorm",
        grid=(NT // tm,),
        in_specs=_pair_specs(tm, D)
                 + [pl.BlockSpec((1, N_MOD, D), lambda i: (_mod_row(i * tm), 0, 0)),
                  pl.BlockSpec((1, D), lambda i: (0, 0))],
        out_specs=pl.BlockSpec((tm, D), lambda i: (i, 0)),
        out_shape=jax.ShapeDtypeStruct((NT, D), BF16),
        compiler_params=_params(("arbitrary",)),
    )(*x_pair, mod_l, ln.reshape(1, D))


def _mm_kernel(a_ref, w_ref, o_ref, wb_ref):
    @pl.when(pl.program_id(1) == 0)
    def _():
        wb_ref[...] = w_ref[...].astype(BF16)

    o_ref[...] = jnp.dot(a_ref[...], wb_ref[...], preferred_element_type=F32)


def _kv_kernel(h0_ref, h1_ref, w_ref, k_ref, v_ref, wb_ref):
    @pl.when(pl.program_id(1) == 0)
    def _():
        wb_ref[...] = w_ref[0].astype(BF16)

    h = jnp.where(pl.program_id(0) == 0, h0_ref[...], h1_ref[...])
    kv = jnp.dot(h, wb_ref[...], preferred_element_type=F32)
    half = N_PAIRS * LANES
    k_ref[...] = kv[:, :half].reshape(k_ref.shape)
    v_ref[...] = kv[:, half:].reshape(v_ref.shape)


def _kv_proj(h_layers, w_in):
    per = 4
    kv_col = 3072 // D
    hspec = pl.BlockSpec((per * SEQ, D), lambda l, i: (i, 0))
    ospec = pl.BlockSpec((per, 1, SEQ, N_PAIRS * LANES), lambda l, i: (i, l, 0, 0))
    shape = jax.ShapeDtypeStruct((BATCH, DEPTH, SEQ, N_PAIRS * LANES), F32)
    k, v = pl.pallas_call(
        _kv_kernel,
        name="kv_proj",
        grid=(DEPTH, BATCH // per),
        in_specs=[hspec, hspec, pl.BlockSpec((1, D, D), lambda l, i: (l, 0, kv_col))],
        out_specs=[ospec, ospec],
        out_shape=[shape, shape],
        scratch_shapes=[pltpu.VMEM((D, D), BF16)],
        compiler_params=_params(("arbitrary", "arbitrary"), VMEM_LIMIT),
    )(*h_layers, w_in)
    cache_shape = (BATCH, DEPTH, SEQ, 2 * N_PAIRS, HEAD_DIM)
    return k.reshape(cache_shape), v.reshape(cache_shape)


def _in_proj(h_all, w_in, layer):
    tm, tn = 512, 1792
    return pl.pallas_call(
        _mm_kernel,
        name="in_proj",
        grid=(IN_COLS // tn, NT // tm),
        in_specs=[pl.BlockSpec((tm, D), lambda j, i: (i, 0)),
                  pl.BlockSpec((None, D, tn), lambda j, i: (layer, 0, j))],
        out_specs=pl.BlockSpec((tm, tn), lambda j, i: (i, j)),
        out_shape=jax.ShapeDtypeStruct((NT, IN_COLS), F32),
        scratch_shapes=[pltpu.VMEM((D, tn), BF16)],
        compiler_params=_params(("arbitrary", "arbitrary"), VMEM_LIMIT),
    )(h_all, w_in)


def _swap16(x):
    lane = lax.broadcasted_iota(I32, x.shape, 1)
    return jnp.where((lane // 16) % 2 == 0, pltpu.roll(x, LANES - 16, 1), pltpu.roll(x, 16, 1))


def _block_diag(top, bottom):
    z = jnp.zeros((HEAD_DIM, HEAD_DIM), F32)
    return jnp.concatenate([jnp.concatenate([top, z], axis=1),
                            jnp.concatenate([z, bottom], axis=1)], axis=0)


RET_PAIRS_PER_STEP = 2


def _retention_kernel(lg_ref, q_ref, k_ref, v_ref, g_ref, cos_ref, sin_ref, s0_ref, o_ref, sf_scr, sb_scr):
    for pp in range(RET_PAIRS_PER_STEP):
        _retention_pair(lg_ref, q_ref, k_ref, v_ref, g_ref, cos_ref, sin_ref, s0_ref, o_ref, sf_scr, sb_scr,
                        pp, pl.program_id(1) * RET_PAIRS_PER_STEP + pp)


def _retention_pair(lg_ref, q_ref, k_ref, v_ref, g_ref, cos_ref, sin_ref, s0_ref, o_ref, sf_scr, sb_scr, pp, pair):
    n_chunks = DEC_SEQ // CHUNK
    lanes = slice(pp * LANES, (pp + 1) * LANES)
    lane1 = lax.broadcasted_iota(I32, (1, LANES), 1)
    lo1 = lane1 < HEAD_DIM
    lgf = jnp.where(lo1, lg_ref[0, 2 * pair], lg_ref[0, 2 * pair + 1])
    lgb = jnp.where(lo1, lg_ref[1, 2 * pair], lg_ref[1, 2 * pair + 1])
    lg_heads = [(lg_ref[0, 2 * pair], lg_ref[1, 2 * pair]),
                (lg_ref[0, 2 * pair + 1], lg_ref[1, 2 * pair + 1])]

    row = lax.broadcasted_iota(I32, (CHUNK, CHUNK), 0)
    col = lax.broadcasted_iota(I32, (CHUNK, CHUNK), 1)
    rel = (row - col).astype(F32)
    lo_mask = col < HEAD_DIM
    blockdiag = (row < HEAD_DIM) == (col < HEAD_DIM)
    posf = row.astype(F32)
    dmat = []
    for hf, hb in lg_heads:
        dmat.append(jnp.where(rel >= 0, jnp.exp(jnp.where(rel >= 0, rel, 0.0) * hf), 0.0)
                    + jnp.where(rel <= 0, jnp.exp(jnp.where(rel <= 0, -rel, 0.0) * hb), 0.0))
    qdec_f = jnp.exp((posf + 1.0) * lgf)
    kdec_f = jnp.exp((CHUNK - 1.0 - posf) * lgf)
    qdec_b = jnp.exp((CHUNK - posf) * lgb)
    kdec_b = jnp.exp(posf * lgb)
    sdec_f = jnp.exp(CHUNK * lgf)
    sdec_b = jnp.exp(CHUNK * lgb)

    def load(c):
        rows = pl.ds(c * CHUNK, CHUNK)
        cs, sn = cos_ref[rows, :], sin_ref[rows, :]
        q = q_ref[rows, lanes]
        k = k_ref[rows, lanes]
        q = q * cs + _swap16(q) * sn
        k = k * cs + _swap16(k) * sn
        return q, k * (HEAD_DIM ** -0.5), v_ref[rows, lanes]

    def state_update(s, k, v, kdec, sdec):
        kd = (k * kdec).T.astype(BF16)
        u = jnp.dot(kd, v.astype(BF16), preferred_element_type=F32)
        return s * sdec + jnp.where(blockdiag, u, 0.0)

    h0, h1 = 2 * pp, 2 * pp + 1
    s_f = _block_diag(s0_ref[0, 0, 0, h0], s0_ref[0, 0, 0, h1])
    s_b = _block_diag(s0_ref[0, 0, 1, h0], s0_ref[0, 0, 1, h1])
    scr0 = pp * n_chunks
    for c in range(n_chunks):
        sf_scr[scr0 + c] = s_f
        _, k, v = load(c)
        s_f = state_update(s_f, k, v, kdec_f, sdec_f)
    for c in reversed(range(n_chunks)):
        sb_scr[scr0 + c] = s_b
        _, k, v = load(c)
        s_b = state_update(s_b, k, v, kdec_b, sdec_b)

    for c in range(n_chunks):
        q, k, v = load(c)
        qb, kb, vb = q.astype(BF16), k.astype(BF16), v.astype(BF16)
        outs = []
        for h in range(2):
            qh = jnp.where(lo_mask if h == 0 else ~lo_mask, qb, jnp.zeros_like(qb))
            a = lax.dot_general(qh, kb, NT_DIMS, preferred_element_type=F32) * dmat[h]
            outs.append(jnp.dot(a.astype(BF16), vb, preferred_element_type=F32))
        o = jnp.where(lo_mask, outs[0], outs[1])
        o = o + jnp.dot(qb, sf_scr[scr0 + c].astype(BF16), preferred_element_type=F32) * qdec_f
        o = o + jnp.dot(qb, sb_scr[scr0 + c].astype(BF16), preferred_element_type=F32) * qdec_b
        o2 = o * o
        ms0 = jnp.sum(jnp.where(lo_mask, o2, 0.0), axis=1, keepdims=True) * (1.0 / HEAD_DIM)
        ms1 = jnp.sum(jnp.where(lo_mask, 0.0, o2), axis=1, keepdims=True) * (1.0 / HEAD_DIM)
        inv = jnp.where(lo_mask, lax.rsqrt(ms0 + EPS), lax.rsqrt(ms1 + EPS))
        g = g_ref[pl.ds(c * CHUNK, CHUNK), lanes]
        o_ref[pl.ds(c * CHUNK, CHUNK), lanes] = (_silu(g) * (o * inv)).astype(BF16)


def _retention(z, log_g, rope, state_ret, layer):
    pps = RET_PAIRS_PER_STEP
    width = pps * LANES
    row_block0 = NP // DEC_SEQ
    scratch = pltpu.VMEM((pps * (DEC_SEQ // CHUNK), CHUNK, CHUNK), F32)

    def zspec(cb):
        return pl.BlockSpec((DEC_SEQ, width), lambda b, p: (row_block0 + b, cb // pps + p))

    table = pl.BlockSpec((DEC_SEQ, LANES), lambda b, p: (0, 0))
    return pl.pallas_call(
        _retention_kernel,
        name="retention",
        grid=(DEC_BATCH, N_PAIRS // pps),
        in_specs=[pl.BlockSpec(memory_space=pltpu.SMEM), zspec(0), zspec(4), zspec(8), zspec(12), table, table,
                  pl.BlockSpec((1, 1, 2, 2 * pps, HEAD_DIM, HEAD_DIM), lambda b, p: (b, layer, 0, p, 0, 0))],
        out_specs=pl.BlockSpec((DEC_SEQ, width), lambda b, p: (b, p)),
        out_shape=jax.ShapeDtypeStruct((NS, N_PAIRS * LANES), BF16),
        scratch_shapes=[scratch, scratch],
        compiler_params=_params(("arbitrary", "arbitrary"), VMEM_LIMIT),
    )(log_g, z, z, z, z, *rope, state_ret)


def _retention_ctx_kernel(lg_ref, q_ref, k_ref, v_ref, g_ref, o_ref, st_ref, decay_ref):
    heads = 2 * N_PAIRS

    @pl.when(pl.program_id(0) == 0)
    def _():
        rel = (lax.broadcasted_iota(I32, (SEQ, SEQ), 0) - lax.broadcasted_iota(I32, (SEQ, SEQ), 1)).astype(F32)
        for h in range(heads):
            decay_ref[h] = (jnp.where(rel >= 0, jnp.exp(jnp.where(rel >= 0, rel, 0.0) * lg_ref[0, h]), 0.0)
                            + jnp.where(rel <= 0, jnp.exp(jnp.where(rel <= 0, -rel, 0.0) * lg_ref[1, h]), 0.0))

    lane = lax.broadcasted_iota(I32, (SEQ, LANES), 1)
    lo_mask = lane < HEAD_DIM
    pos = lax.broadcasted_iota(I32, (SEQ, LANES), 0).astype(F32)
    blockdiag = ((lax.broadcasted_iota(I32, (LANES, LANES), 0) < HEAD_DIM)
                 == (lax.broadcasted_iota(I32, (LANES, LANES), 1) < HEAD_DIM))
    for pp in range(N_PAIRS):
        lanes = slice(pp * LANES, (pp + 1) * LANES)
        h0, h1 = 2 * pp, 2 * pp + 1
        k = k_ref[:, lanes] * (HEAD_DIM ** -0.5)
        qb, kb, vb = q_ref[:, lanes].astype(BF16), k.astype(BF16), v_ref[:, lanes].astype(BF16)
        outs = []
        for h, mask in ((h0, lo_mask), (h1, ~lo_mask)):
            qh = jnp.where(mask, qb, jnp.zeros_like(qb))
            a = lax.dot_general(qh, kb, NT_DIMS, preferred_element_type=F32) * decay_ref[h]
            outs.append(jnp.dot(a.astype(BF16), vb, preferred_element_type=F32))
        o = jnp.where(lo_mask, outs[0], outs[1])
        o2 = o * o
        ms0 = jnp.sum(jnp.where(lo_mask, o2, 0.0), axis=1, keepdims=True) * (1.0 / HEAD_DIM)
        ms1 = jnp.sum(jnp.where(lo_mask, 0.0, o2), axis=1, keepdims=True) * (1.0 / HEAD_DIM)
        inv = jnp.where(lo_mask, lax.rsqrt(ms0 + EPS), lax.rsqrt(ms1 + EPS))
        o_ref[:, lanes] = (_silu(g_ref[:, lanes]) * (o * inv)).astype(BF16)

        for d, age in ((0, SEQ - 1.0 - pos), (1, pos)):
            lg = jnp.where(lo_mask, lg_ref[d, h0], lg_ref[d, h1])
            kd = (k * jnp.exp(age * lg)).T.astype(BF16)
            s = jnp.where(blockdiag, jnp.dot(kd, vb, preferred_element_type=F32), 0.0)
            st_ref[0, d, h0] = s[:HEAD_DIM, :HEAD_DIM]
            st_ref[0, d, h1] = s[HEAD_DIM:, HEAD_DIM:]


def _retention_ctx(z, log_g):
    width = N_PAIRS * LANES
    heads = 2 * N_PAIRS
    zspec = lambda cb: pl.BlockSpec((SEQ, width), lambda b: (b, cb))
    return pl.pallas_call(
        _retention_ctx_kernel,
        name="retention_ctx",
        grid=(BATCH,),
        in_specs=[pl.BlockSpec(memory_space=pltpu.SMEM), zspec(0), zspec(1), zspec(2), zspec(3)],
        out_specs=[pl.BlockSpec((SEQ, width), lambda b: (b, 0)),
                   pl.BlockSpec((1, 2, heads, HEAD_DIM, HEAD_DIM), lambda b: (b, 0, 0, 0, 0))],
        out_shape=[jax.ShapeDtypeStruct((NP, width), BF16),
                   jax.ShapeDtypeStruct((BATCH, 2, heads, HEAD_DIM, HEAD_DIM), F32)],
        scratch_shapes=[pltpu.VMEM((heads, SEQ, SEQ), F32)],
        compiler_params=_params(("arbitrary",)),
    )(log_g, z, z, z, z)


def _rope_tables():
    t = np.arange(DEC_SEQ)
    posn = [(t // GRID_W).astype(np.float32), (t % GRID_W).astype(np.float32)]
    nf = HEAD_DIM // 4
    freqs = (1.0 / (np.float32(ROPE_BASE) ** (np.arange(nf, dtype=np.float32) / np.float32(nf)))).astype(np.float32)
    cos = np.zeros((DEC_SEQ, HEAD_DIM), np.float32)
    sin = np.zeros((DEC_SEQ, HEAD_DIM), np.float32)
    for half in range(2):
        ang = (posn[half][:, None] * freqs[None, :]).astype(np.float32)
        for grp in range(2):
            lo = half * 32 + grp * nf
            cos[:, lo:lo + nf] = np.cos(ang)
            sin[:, lo:lo + nf] = np.sin(ang) * (-1.0 if grp == 0 else 1.0)
    return jnp.asarray(np.tile(cos, (1, 2))), jnp.asarray(np.tile(sin, (1, 2)))


def _pool_kernel(u_ref, w_ref, sc_ref, o_ref, *, seq):
    padded = seq + 2 * POOL_PAD
    t = lax.broadcasted_iota(I32, (seq, 1), 0)
    zpad = jnp.zeros((POOL_PAD, LANES), F32)
    for gi, w in enumerate(POOL_WINDOWS):
        x = u_ref[:, gi * LANES:(gi + 1) * LANES]
        run = jnp.concatenate([zpad, x, zpad], axis=0)
        span = 1
        while span < w:
            run = run + pltpu.roll(run, padded - span, 0)
            span *= 2
        win = pltpu.roll(run, padded - (POOL_PAD - w // 2), 0)[:seq]
        cnt = (jnp.minimum(t + w // 2, seq) - jnp.maximum(t - w // 2, 0)).astype(F32)
        pooled = win / cnt - x
        mixed = jnp.dot(pooled.astype(BF16), w_ref[gi].astype(BF16), preferred_element_type=F32)
        o_ref[:, gi * LANES:(gi + 1) * LANES] = (mixed * sc_ref[:, gi * LANES:(gi + 1) * LANES]).astype(BF16)


def _pool(z, pool_w, pool_scale, *, nb, seq, row_block0):
    width = len(POOL_WINDOWS) * LANES
    return pl.pallas_call(
        functools.partial(_pool_kernel, seq=seq),
        name="pool",
        grid=(nb,),
        in_specs=[pl.BlockSpec((seq, width), lambda b: (row_block0 + b, 2048 // width)),
                  pl.BlockSpec((len(POOL_WINDOWS), LANES, LANES), lambda b: (0, 0, 0)),
                  pl.BlockSpec((1, width), lambda b: (0, 0))],
        out_specs=pl.BlockSpec((seq, width), lambda b: (b, 0)),
        out_shape=jax.ShapeDtypeStruct((nb * seq, width), BF16),
        compiler_params=_params(("arbitrary",), VMEM_LIMIT),
    )(z, pool_w, pool_scale.reshape(1, width))


def _head_select(h, shape):
    lane = lax.broadcasted_iota(I32, shape, 1)
    return (lane < HEAD_DIM) if h == 0 else (lane >= HEAD_DIM)


def _ctx_attn_kernel(q_ref, k_ref, v_ref, o_ref):
    for pp in range(N_PAIRS):
        lanes = slice(pp * LANES, (pp + 1) * LANES)
        qb = (q_ref[:, lanes] * (HEAD_DIM ** -0.5)).astype(BF16)
        kb = k_ref[:, lanes].astype(BF16)
        vb = v_ref[:, lanes].astype(BF16)
        outs = []
        for h in range(2):
            qh = jnp.where(_head_select(h, qb.shape), qb, jnp.zeros_like(qb))
            s = lax.dot_general(qh, kb, NT_DIMS, preferred_element_type=F32)
            p = jnp.exp(s - jnp.max(s, axis=1, keepdims=True))
            denom = jnp.sum(p, axis=1, keepdims=True)
            outs.append(jnp.dot(p.astype(BF16), vb, preferred_element_type=F32) / denom)
        o_ref[:, lanes] = jnp.where(_head_select(0, outs[0].shape), outs[0], outs[1]).astype(BF16)


def _ctx_attention(z):
    width = N_PAIRS * LANES

    def zspec(cb):
        return pl.BlockSpec((SEQ, width), lambda b: (b, cb))

    return pl.pallas_call(
        _ctx_attn_kernel,
        name="ctx_attn",
        grid=(BATCH,),
        in_specs=[zspec(5), zspec(6), zspec(7)],
        out_specs=pl.BlockSpec((SEQ, width), lambda b: (b, 0)),
        out_shape=jax.ShapeDtypeStruct((NP, width), BF16),
        compiler_params=_params(("arbitrary",)),
    )(z, z, z)


NA_QROWS = 4
NA_QBLK = NA_QROWS * GRID_W
NA_KROWS = 12
NA_NBLK = DEC_SEQ // NA_QBLK


def _na_key_block(i):
    return jnp.clip(i - 1, 0, NA_NBLK - 3)


NA_PATTERNS = ((0, 0), (NA_QROWS, 0), (DEC_SEQ // GRID_W - NA_QROWS, DEC_SEQ // GRID_W - NA_KROWS))
NA_DX_LANE = GRID_W - (NA_KW - 1)


def _na_bias_kernel(rpb_ref, o_ref):
    rows = DEC_SEQ // GRID_W
    q = lax.broadcasted_iota(I32, (GRID_W, LANES), 0)
    lane = lax.broadcasted_iota(I32, (GRID_W, LANES), 1)
    c = lane % GRID_W
    c_start = jnp.clip(q - NA_KW // 2, 0, GRID_W - NA_KW)
    col_ok = (c >= c_start) & (c < c_start + NA_KW)
    lower = lane < GRID_W
    for p, (r0, ks) in enumerate(NA_PATTERNS):
        for rr in range(NA_QROWS):
            r = r0 + rr
            start = min(max(r - NA_KH // 2, 0), rows - NA_KH)
            for kp in range(NA_KROWS // 2):
                halves = []
                for half in range(2):
                    kr = ks + 2 * kp + half
                    if start <= kr < start + NA_KH:
                        row = jnp.broadcast_to(rpb_ref[0, 0, pl.ds(kr - r + NA_KH - 1, 1), :], (GRID_W, LANES))
                        halves.append(pltpu.roll(row, GRID_W * (1 - half), 1, stride=1, stride_axis=0))
                    else:
                        halves.append(None)
                neg = jnp.full((GRID_W, LANES), -jnp.inf, F32)
                lo_half = neg if halves[0] is None else jnp.where(col_ok, halves[0], neg)
                hi_half = neg if halves[1] is None else jnp.where(col_ok, halves[1], neg)
                o_ref[0, p, 0, rr * GRID_W:(rr + 1) * GRID_W, kp * LANES:(kp + 1) * LANES] = (
                    jnp.where(lower, lo_half, hi_half))


def _na_bias(na_rpb):
    ny, nx = 2 * NA_KH - 1, 2 * NA_KW - 1
    padded = jnp.pad(na_rpb.astype(F32), ((0, 0), (0, 0), (0, 16 - ny), (NA_DX_LANE, LANES - NA_DX_LANE - nx)))
    heads = 2 * N_PAIRS
    return pl.pallas_call(
        _na_bias_kernel,
        name="nbr_bias",
        grid=(DEPTH, heads),
        in_specs=[pl.BlockSpec((1, 1, 16, LANES), lambda l, h: (l, h, 0, 0))],
        out_specs=pl.BlockSpec((1, len(NA_PATTERNS), 1, NA_QBLK, NA_KROWS * GRID_W), lambda l, h: (l, 0, h, 0, 0)),
        out_shape=jax.ShapeDtypeStruct((DEPTH, len(NA_PATTERNS), heads, NA_QBLK, NA_KROWS * GRID_W), F32),
        compiler_params=_params(("arbitrary", "arbitrary")),
    )(padded)


def _na_kernel(q_ref, k0_ref, k1_ref, k2_ref, v0_ref, v1_ref, v2_ref, ck_ref, cv_ref, bias_ref, o_ref):
    for pp in range(N_PAIRS):
        lanes = slice(pp * LANES, (pp + 1) * LANES)
        qb = (q_ref[:, lanes] * (HEAD_DIM ** -0.5)).astype(BF16)
        ks = [r[:, lanes].astype(BF16) for r in (k0_ref, k1_ref, k2_ref)] + [ck_ref[0, 0, :, lanes].astype(BF16)]
        vs = [r[:, lanes].astype(BF16) for r in (v0_ref, v1_ref, v2_ref)] + [cv_ref[0, 0, :, lanes].astype(BF16)]
        outs = []
        for h in range(2):
            qh = jnp.where(_head_select(h, qb.shape), qb, jnp.zeros_like(qb))
            ss = []
            for j in range(4):
                s = lax.dot_general(qh, ks[j], NT_DIMS, preferred_element_type=F32)
                if j < 3:
                    s = s + bias_ref[0, 2 * pp + h, :, j * NA_QBLK:(j + 1) * NA_QBLK]
                ss.append(s)
            m = functools.reduce(jnp.maximum, [jnp.max(s, axis=1, keepdims=True) for s in ss])
            ps = [jnp.exp(s - m) for s in ss]
            denom = functools.reduce(jnp.add, [jnp.sum(p, axis=1, keepdims=True) for p in ps])
            acc = functools.reduce(jnp.add, [jnp.dot(p.astype(BF16), v, preferred_element_type=F32)
                                             for p, v in zip(ps, vs)])
            outs.append(acc / denom)
        o_ref[:, lanes] = jnp.where(_head_select(0, outs[0].shape), outs[0], outs[1]).astype(BF16)


def _neighbourhood_attention(z, bias, cache_k, cache_v, layer):
    base = NP // NA_QBLK
    width = N_PAIRS * LANES
    heads = 2 * N_PAIRS

    def kvspec(cb, j):
        return pl.BlockSpec((NA_QBLK, width), lambda b, i: (base + b * NA_NBLK + _na_key_block(i) + j, cb))

    cspec = pl.BlockSpec((1, 1, SEQ, width), lambda b, i: (b, layer, 0, 0))
    pattern = lambda i: jnp.where(i == 0, 0, jnp.where(i == NA_NBLK - 1, 2, 1))
    ck = cache_k.reshape(DEC_BATCH, DEPTH, SEQ, width)
    cv = cache_v.reshape(DEC_BATCH, DEPTH, SEQ, width)
    return pl.pallas_call(
        _na_kernel,
        name="nbr_attn",
        grid=(DEC_BATCH, NA_NBLK),
        in_specs=[pl.BlockSpec((NA_QBLK, width), lambda b, i: (base + b * NA_NBLK + i, 5))]
                 + [kvspec(6, j) for j in range(3)] + [kvspec(7, j) for j in range(3)]
                 + [cspec, cspec,
                    pl.BlockSpec((None, 1, heads, NA_QBLK, NA_KROWS * GRID_W),
                                 lambda b, i: (layer, pattern(i), 0, 0, 0))],
        out_specs=pl.BlockSpec((NA_QBLK, width), lambda b, i: (b * NA_NBLK + i, 0)),
        out_shape=jax.ShapeDtypeStruct((NS, width), BF16),
        compiler_params=_params(("arbitrary", "arbitrary"), VMEM_LIMIT),
    )(z, z, z, z, z, z, z, ck, cv, bias)


def _merge_kernel(rp_ref, rs_ref, pp_ref, ps_ref, ap_ref, as_ref, xp_ref, xs_ref, g0_ref, g1_ref, g2_ref,
                  wr_ref, wp_ref, wa_ref, wo_ref, m_ref, ln_ref, ln_ffn_ref, wrt_ref,
                  o_ref, h_ref, aff_ref, wrb, wpb, wab, wob):
    @pl.when(pl.program_id(0) == 0)
    def _():
        wrb[...] = wr_ref[...].astype(BF16)
        wpb[...] = wp_ref[...].astype(BF16)
        wab[...] = wa_ref[...].astype(BF16)
        wob[...] = wo_ref[...].astype(BF16)

    branch = lambda p_ref, s_ref, w: jnp.dot(_pick(p_ref, s_ref), w[...], preferred_element_type=F32)
    merged = (jax.nn.sigmoid(g0_ref[...]) * branch(rp_ref, rs_ref, wrb)
              + jax.nn.sigmoid(g1_ref[...]) * branch(pp_ref, ps_ref, wpb)
              + jax.nn.sigmoid(g2_ref[...]) * branch(ap_ref, as_ref, wab))
    mix = jnp.dot(merged.astype(BF16), wob[...], preferred_element_type=F32)
    x = _pick(xp_ref, xs_ref) + m_ref[0, 2:3, :] * (_rms(mix) * ln_ref[...])
    o_ref[...] = x

    h = (_rms(x) * ln_ffn_ref[...]) * (1.0 + m_ref[0, 4:5, :]) + m_ref[0, 3:4, :]
    hb = h.astype(BF16)
    h_ref[...] = hb
    hl = (h - hb.astype(F32)).astype(BF16)
    w = wrt_ref[...]
    wb = w.astype(BF16)
    wl = (w - wb.astype(F32)).astype(BF16)
    logits = (lax.dot_general(wb, hb, NT_DIMS, preferred_element_type=F32)
              + lax.dot_general(wb, hl, NT_DIMS, preferred_element_type=F32)
              + lax.dot_general(wl, hb, NT_DIMS, preferred_element_type=F32))
    e = jnp.exp(logits - jnp.max(logits, axis=0, keepdims=True))
    aff = e / jnp.sum(e, axis=0, keepdims=True)
    for j in range(aff_ref.shape[0]):
        aff_ref[j] = aff[:, j * LANES:(j + 1) * LANES]


def _merge(ret_pair, pool_pair, na_pair, x_pair, z, w_ret_o, w_pool_o, w_na_o, w_o, mod_l, ln, ln_ffn, w_router_t,
           layer):
    tm = 512
    half = N_PAIRS * LANES
    row = lambda i: (i, 0)
    const = lambda i: (0, 0)
    slab = lambda i: (layer, 0, 0)
    return pl.pallas_call(
        _merge_kernel,
        name="merge",
        grid=(NT // tm,),
        in_specs=_pair_specs(tm, half) * 3 + _pair_specs(tm, D)
                 + [pl.BlockSpec((tm, D), lambda i, c=c: (i, 4 + c)) for c in range(3)]
                 + [pl.BlockSpec((None, half, D), slab)] * 3
                 + [pl.BlockSpec((None, D, D), slab),
                    pl.BlockSpec((1, N_MOD, D), lambda i: (_mod_row(i * tm), 0, 0)),
                    pl.BlockSpec((1, D), const),
                    pl.BlockSpec((1, D), const),
                    pl.BlockSpec((N_EXPERTS, D), const)],
        out_specs=[pl.BlockSpec((tm, D), row),
                   pl.BlockSpec((tm, D), row),
                   pl.BlockSpec((tm // LANES, N_EXPERTS, LANES), lambda i: (i, 0, 0))],
        out_shape=[jax.ShapeDtypeStruct((NT, D), F32),
                   jax.ShapeDtypeStruct((NT, D), BF16),
                   jax.ShapeDtypeStruct((NT // LANES, N_EXPERTS, LANES), F32)],
        scratch_shapes=[pltpu.VMEM((half, D), BF16)] * 3 + [pltpu.VMEM((D, D), BF16)],
        compiler_params=_params(("arbitrary",), VMEM_LIMIT),
    )(*ret_pair, *pool_pair, *na_pair, *x_pair, z, z, z, w_ret_o, w_pool_o, w_na_o, w_o, mod_l, ln.reshape(1, D),
      ln_ffn.reshape(1, D), w_router_t)


def _route_kernel(aff_ref, slot_ref, offs_ref, ceq_ref, csel_ref, *, cap, nblk):
    as_float = lambda bits: lax.bitcast_convert_type(bits, F32)

    def count(pred):
        return jnp.sum(jnp.sum(jnp.where(pred, 1.0, 0.0), axis=0), axis=1, keepdims=True)

    def search(_, lohi):
        lo, hi = lohi
        mid = lo + ((hi - lo + 1) >> 1)
        ok = count(aff_ref[...] >= as_float(mid)[None]) >= cap
        return jnp.where(ok, mid, lo), jnp.where(ok, hi, mid - 1)

    lo0 = jnp.zeros((N_EXPERTS, 1), I32)
    hi0 = jnp.full((N_EXPERTS, 1), 0x7F800000, I32)
    thr_bits, _ = lax.fori_loop(0, 31, search, (lo0, hi0))
    thr = as_float(thr_bits)
    need = cap - count(aff_ref[...] > thr[None])

    upper = (lax.broadcasted_iota(I32, (LANES, LANES), 0)
             < lax.broadcasted_iota(I32, (LANES, LANES), 1)).astype(BF16)
    lane = lax.broadcasted_iota(I32, (N_EXPERTS, LANES), 1)

    ceq_ref[...] = jnp.zeros_like(ceq_ref)
    csel_ref[...] = jnp.zeros_like(csel_ref)
    offs_ref[...] = jnp.zeros_like(offs_ref)

    def block(b, carry):
        c_eq = ceq_ref[...]
        c_sel = csel_ref[...]
        aff = aff_ref[b]
        eq = aff == thr
        eqf = jnp.where(eq, 1.0, 0.0)
        eq_rank = jnp.dot(eqf.astype(BF16), upper, preferred_element_type=F32) + c_eq
        sel = (aff > thr) | (eq & (eq_rank < need))
        self_ = jnp.where(sel, 1.0, 0.0)
        rank = jnp.dot(self_.astype(BF16), upper, preferred_element_type=F32) + c_sel
        slot_ref[b] = jnp.where(sel, rank, -1.0)
        offs_ref[...] = jnp.where(lane == b, c_sel.astype(I32), offs_ref[...])
        ceq_ref[...] = c_eq + jnp.sum(eqf, axis=1, keepdims=True)
        csel_ref[...] = c_sel + jnp.sum(self_, axis=1, keepdims=True)
        return carry

    lax.fori_loop(0, nblk, block, 0)
    offs_ref[...] = jnp.where(lane >= nblk, csel_ref[...].astype(I32), offs_ref[...])


def _route(aff_blocks, *, blk0, nblk, cap):
    return pl.pallas_call(
        functools.partial(_route_kernel, cap=cap, nblk=nblk),
        name="route",
        grid=(1,),
        in_specs=[pl.BlockSpec((nblk, N_EXPERTS, LANES), lambda i: (blk0 // nblk, 0, 0))],
        out_specs=[pl.BlockSpec((nblk, N_EXPERTS, LANES), lambda i: (0, 0, 0)),
                   pl.BlockSpec((N_EXPERTS, LANES), lambda i: (0, 0))],
        out_shape=[jax.ShapeDtypeStruct((nblk, N_EXPERTS, LANES), F32),
                   jax.ShapeDtypeStruct((N_EXPERTS, LANES), I32)],
        scratch_shapes=[pltpu.VMEM((N_EXPERTS, LANES), F32)] * 2,
        compiler_params=_params(("arbitrary",)),
    )(aff_blocks)


GATHER_TOKENS = 256
TILE_BLOCKS = GATHER_TOKENS // LANES
SLOT_ALIGN = 16
WINDOW = LANES // 2
ROUND_SLOTS = WINDOW - SLOT_ALIGN
GATE_TERMS = 3
ROW_W = D + LANES


def _round_bounds(offs_ref, e, t, r):
    off0 = offs_ref[e, TILE_BLOCKS * t]
    off1 = offs_ref[e, TILE_BLOCKS * t + TILE_BLOCKS]
    lo = jnp.minimum(off0 + ROUND_SLOTS * r, off1)
    hi = jnp.minimum(lo + ROUND_SLOTS, off1)
    return lo, hi, pl.multiple_of(lo & -SLOT_ALIGN, SLOT_ALIGN)


def _n_rounds(offs_ref, t):
    most = jnp.int32(0)
    for e in range(N_EXPERTS):
        most = jnp.maximum(most, offs_ref[e, TILE_BLOCKS * t + TILE_BLOCKS] - offs_ref[e, TILE_BLOCKS * t])
    rounds = jnp.int32(0)
    for filled in range(0, GATHER_TOKENS, ROUND_SLOTS):
        rounds = rounds + (most > filled).astype(I32)
    return rounds


def _gate_terms(aff_cols):
    hi = aff_cols.astype(BF16)
    rest = aff_cols - hi.astype(F32)
    mid = rest.astype(BF16)
    lo = (rest - mid.astype(F32)).astype(BF16)
    terms = jnp.stack([hi, mid, lo], axis=-1).reshape(aff_cols.shape[0], GATE_TERMS * N_EXPERTS)
    return jnp.pad(terms, ((0, 0), (0, LANES - GATE_TERMS * N_EXPERTS)))


def _gather_kernel(offs_ref, slot_ref, h_ref, g_ref, xe_hbm, stage_ref, carry_ref, sem, nround_ref, *, n_tiles):
    t = pl.program_id(0)
    cap = xe_hbm.shape[1] - WINDOW

    def out_copy(buf, e, start):
        return pltpu.make_async_copy(stage_ref.at[buf, e], xe_hbm.at[e, pl.ds(start, WINDOW)], sem.at[buf])

    def wait_round(buf):
        for e in range(N_EXPERTS):
            out_copy(buf, e, 0).wait()

    @pl.when(t == 0)
    def _():
        carry_ref[...] = jnp.zeros_like(carry_ref)
        nround_ref[0] = 0
        stage_ref[0, 0] = jnp.zeros((WINDOW, ROW_W), BF16)
        for e in range(N_EXPERTS):
            pltpu.make_async_copy(stage_ref.at[0, 0], xe_hbm.at[e, pl.ds(cap, WINDOW)], sem.at[0]).start()
        wait_round(0)

    hb = jnp.concatenate([h_ref[...], g_ref[...]], axis=1)
    sub = lax.broadcasted_iota(I32, (WINDOW, GATHER_TOKENS), 0).astype(F32)

    def one_round(r, carry):
        done = nround_ref[0]
        buf = done & 1
        bounds = [_round_bounds(offs_ref, e, t, r) for e in range(N_EXPERTS)]
        onehots = []
        for e in range(N_EXPERTS):
            lo, hi, start = bounds[e]
            srow = jnp.concatenate([slot_ref[j, e:e + 1, :] for j in range(TILE_BLOCKS)], axis=1)
            hit = ((srow - start.astype(F32) == sub) & (srow >= lo.astype(F32)) & (srow < hi.astype(F32)))
            onehots.append(jnp.where(hit, 1.0, 0.0).astype(BF16))
        rows = jnp.dot(jnp.concatenate(onehots, axis=0), hb, preferred_element_type=F32)
        for e in range(N_EXPERTS):
            lo, hi, start = bounds[e]
            piece = rows[e * WINDOW:(e + 1) * WINDOW]
            head = piece[:SLOT_ALIGN] + carry_ref[e].astype(F32)
            stage_ref[buf, e, :SLOT_ALIGN, :] = head.astype(BF16)
            stage_ref[buf, e, SLOT_ALIGN:, :] = piece[SLOT_ALIGN:].astype(BF16)
            tail = pl.multiple_of((hi & -SLOT_ALIGN) - start, SLOT_ALIGN)
            carry_ref[e] = stage_ref[buf, e, pl.ds(tail, SLOT_ALIGN), :]

        @pl.when(done > 0)
        def _():
            wait_round(1 - buf)

        for e in range(N_EXPERTS):
            out_copy(buf, e, bounds[e][2]).start()
        nround_ref[0] = done + 1
        return carry

    lax.fori_loop(0, _n_rounds(offs_ref, t), one_round, 0)

    @pl.when((t == n_tiles - 1) & (nround_ref[0] > 0))
    def _():
        wait_round((nround_ref[0] - 1) & 1)


def _gather(offs, slot, h_all, gate_terms, *, row0, n, cap):
    n_tiles = n // GATHER_TOKENS
    tile0 = row0 // GATHER_TOKENS
    return pl.pallas_call(
        functools.partial(_gather_kernel, n_tiles=n_tiles),
        name="gather",
        grid_spec=pltpu.PrefetchScalarGridSpec(
            num_scalar_prefetch=1,
            grid=(n_tiles,),
            in_specs=[pl.BlockSpec((TILE_BLOCKS, N_EXPERTS, LANES), lambda t, o: (t, 0, 0)),
                      pl.BlockSpec((GATHER_TOKENS, D), lambda t, o: (tile0 + t, 0)),
                      pl.BlockSpec((GATHER_TOKENS, LANES), lambda t, o: (t, 0))],
            out_specs=pl.BlockSpec(memory_space=pl.ANY),
            scratch_shapes=[pltpu.VMEM((2, N_EXPERTS, WINDOW, ROW_W), BF16),
                            pltpu.VMEM((N_EXPERTS, SLOT_ALIGN, ROW_W), BF16),
                            pltpu.SemaphoreType.DMA((2,)),
                            pltpu.SMEM((1,), I32)]),
        out_shape=jax.ShapeDtypeStruct((N_EXPERTS, cap + WINDOW, ROW_W), BF16),
        compiler_params=_params(("arbitrary",), VMEM_LIMIT),
    )(offs, slot, h_all, gate_terms)


FF_CHUNK = 512


def _experts_kernel(xp_ref, xs_ref, wg_ref, wu_ref, wd_ref, yp_ref, ys_ref, accp_ref, accs_ref, *, n_f):
    f = pl.program_id(1)
    wg = wg_ref[0].astype(BF16)
    wu = wu_ref[0].astype(BF16)
    wd = wd_ref[0].astype(BF16)

    def ffn(x):
        a = jnp.dot(x, wg, preferred_element_type=F32)
        b = jnp.dot(x, wu, preferred_element_type=F32)
        return jnp.dot((_silu(a) * b).astype(BF16), wd, preferred_element_type=F32)

    @pl.when(f == 0)
    def _():
        accp_ref[...] = ffn(xp_ref[0, :, :D])
        accs_ref[...] = ffn(xs_ref[0, :, :D])

    @pl.when(f > 0)
    def _():
        accp_ref[...] += ffn(xp_ref[0, :, :D])
        accs_ref[...] += ffn(xs_ref[0, :, :D])

    @pl.when(f == n_f - 1)
    def _():
        first = GATE_TERMS * pl.program_id(0)
        for x_ref, y_ref, acc_ref in ((xp_ref, yp_ref, accp_ref), (xs_ref, ys_ref, accs_ref)):
            cap = acc_ref.shape[0]
            lane = lax.broadcasted_iota(I32, (cap, LANES), 1)
            mine = (lane >= first) & (lane < first + GATE_TERMS)
            gate = jnp.sum(jnp.where(mine, x_ref[0, :, D:].astype(F32), 0.0), axis=1, keepdims=True)
            y_ref[0, :cap, :] = (acc_ref[...] * gate).astype(BF16)
            y_ref[0, cap:, :] = jnp.zeros((WINDOW, D), BF16)


def _experts(xe_p, xe_s, w_gate, w_up, w_down, layer):
    n_f = EXPERT_FF // FF_CHUNK
    cap_p, cap_s = xe_p.shape[1] - WINDOW, xe_s.shape[1] - WINDOW
    spec = lambda rows, width: pl.BlockSpec((1, rows, width), lambda e, f: (e, 0, 0))
    out = lambda cap: jax.ShapeDtypeStruct((N_EXPERTS, cap + WINDOW, D), BF16)
    return pl.pallas_call(
        functools.partial(_experts_kernel, n_f=n_f),
        name="experts",
        grid=(N_EXPERTS, n_f),
        in_specs=[spec(cap_p, ROW_W), spec(cap_s, ROW_W),
                  pl.BlockSpec((None, 1, D, FF_CHUNK), lambda e, f: (layer, e, 0, f)),
                  pl.BlockSpec((None, 1, D, FF_CHUNK), lambda e, f: (layer, e, 0, f)),
                  pl.BlockSpec((None, 1, FF_CHUNK, D), lambda e, f: (layer, e, f, 0))],
        out_specs=[spec(cap_p + WINDOW, D), spec(cap_s + WINDOW, D)],
        out_shape=[out(cap_p), out(cap_s)],
        scratch_shapes=[pltpu.VMEM((cap_p, D), F32), pltpu.VMEM((cap_s, D), F32)],
        compiler_params=_params(("arbitrary", "arbitrary"), VMEM_LIMIT),
    )(xe_p, xe_s, w_gate, w_up, w_down)


def _combine_kernel(offs_ref, slot_ref, x_ref, m_ref, ln_ref, ye_hbm, o_ref, stage_ref, sem, *, n_tiles):
    t = pl.program_id(0)
    buf = t & 1

    def in_copy(b, e, start):
        return pltpu.make_async_copy(ye_hbm.at[e, pl.ds(start, WINDOW)],
                                     stage_ref.at[b, pl.ds(e * WINDOW, WINDOW)], sem.at[b])

    def start_round(b, tile, r):
        for e in range(N_EXPERTS):
            in_copy(b, e, _round_bounds(offs_ref, e, tile, r)[2]).start()

    def wait_round(b):
        for e in range(N_EXPERTS):
            in_copy(b, e, 0).wait()

    @pl.when(t == 0)
    def _():
        start_round(0, 0, 0)

    wait_round(buf)

    @pl.when(t + 1 < n_tiles)
    def _():
        start_round(1 - buf, t + 1, 0)

    slot = slot_ref[...]
    lane = lax.broadcasted_iota(I32, (GATHER_TOKENS, LANES), 1)
    first = lane < WINDOW
    row_in_window = (lane % WINDOW).astype(F32)

    def token_rows(b, r):
        onehots = []
        for e in range(0, N_EXPERTS, 2):
            (lo0, hi0, st0), (lo1, hi1, st1) = (_round_bounds(offs_ref, e + i, t, r) for i in range(2))
            s = jnp.where(first, slot[:, e:e + 1], slot[:, e + 1:e + 2])
            lo = jnp.where(first, lo0, lo1).astype(F32)
            hi = jnp.where(first, hi0, hi1).astype(F32)
            start = jnp.where(first, st0, st1).astype(F32)
            hit = (s - start == row_in_window) & (s >= lo) & (s < hi)
            onehots.append(jnp.where(hit, 1.0, 0.0).astype(BF16))
        return jnp.dot(jnp.concatenate(onehots, axis=1), stage_ref[b], preferred_element_type=F32)

    o_ref[...] = token_rows(buf, 0)

    def extra_round(r, carry):
        start_round(buf, t, r)
        wait_round(buf)
        o_ref[...] += token_rows(buf, r)
        return carry

    lax.fori_loop(1, _n_rounds(offs_ref, t), extra_round, 0)
    o_ref[...] = x_ref[...] + m_ref[0, 5:6, :] * (_rms(o_ref[...]) * ln_ref[...])


def _combine(offs, ye, slot_cols, x_all, mod_l, ln, *, row0, n):
    n_tiles = n // GATHER_TOKENS
    tile0 = row0 // GATHER_TOKENS
    return pl.pallas_call(
        functools.partial(_combine_kernel, n_tiles=n_tiles),
        name="combine",
        grid_spec=pltpu.PrefetchScalarGridSpec(
            num_scalar_prefetch=1,
            grid=(n_tiles,),
            in_specs=[pl.BlockSpec((GATHER_TOKENS, N_EXPERTS), lambda t, o: (t, 0)),
                      pl.BlockSpec((GATHER_TOKENS, D), lambda t, o: (tile0 + t, 0)),
                      pl.BlockSpec((1, N_MOD, D), lambda t, o: (_mod_row(row0 + t * GATHER_TOKENS), 0, 0)),
                      pl.BlockSpec((1, D), lambda t, o: (0, 0)),
                      pl.BlockSpec(memory_space=pl.ANY)],
            out_specs=pl.BlockSpec((GATHER_TOKENS, D), lambda t, o: (t, 0)),
            scratch_shapes=[pltpu.VMEM((2, N_EXPERTS * WINDOW, D), BF16),
                            pltpu.SemaphoreType.DMA((2,))]),
        out_shape=jax.ShapeDtypeStruct((n, D), F32),
        compiler_params=_params(("arbitrary",), VMEM_LIMIT),
    )(offs, slot_cols, x_all, mod_l, ln.reshape(1, D), ye)


def _token_major(blocks):
    return blocks.transpose(0, 2, 1).reshape(-1, N_EXPERTS)


def kernel(x_prompt, x_sample, cache_k, cache_v, state_ret, c, c_ctx, w_mod, b_mod, ln_pre_mix, ln_post_mix,
           ln_pre_ffn, ln_post_ffn, w_in, ret_decay, pool_w, pool_scale, na_rpb, w_ret_o, w_pool_o, w_na_o, w_o,
           w_router, w_gate, w_up, w_down):
    cvecs = jnp.zeros((8, D), F32).at[0].set(c_ctx).at[1:1 + DEC_BATCH].set(c)
    mod = _modulation(cvecs, w_mod, b_mod)
    rope = _rope_tables()
    na_bias = _na_bias(na_rpb)
    x_pair = (x_prompt.reshape(NP, D), x_sample.reshape(NS, D))
    h_layers, new_s = [], []
    groups = ((0, NP, NP // N_EXPERTS * 2), (NP, NS, NS // N_EXPERTS * 2))

    for l in range(DEPTH):
        mod_l = mod[l]
        h_all = _prenorm(x_pair, mod_l, ln_pre_mix[l])
        h_layers.append(h_all)
        z = _in_proj(h_all, w_in, l)
        log_g = jax.nn.log_sigmoid(ret_decay[l].astype(F32))

        ret_p, st = _retention_ctx(z, log_g)
        ret_s = _retention(z, log_g, rope, state_ret, l)
        pool_p = _pool(z, pool_w[l], pool_scale[l], nb=BATCH, seq=SEQ, row_block0=0)
        pool_s = _pool(z, pool_w[l], pool_scale[l], nb=DEC_BATCH, seq=DEC_SEQ, row_block0=NP // DEC_SEQ)
        na_p = _ctx_attention(z)
        na_s = _neighbourhood_attention(z, na_bias, cache_k, cache_v, l)
        x_mid, h2, aff = _merge((ret_p, ret_s), (pool_p, pool_s), (na_p, na_s), x_pair, z, w_ret_o, w_pool_o, w_na_o,
                                w_o, mod_l, ln_post_mix[l], ln_pre_ffn[l], w_router[l].T, l)
        new_s.append(st)
        routed = []
        for row0, n, cap in groups:
            slot, offs = _route(aff, blk0=row0 // LANES, nblk=n // LANES, cap=cap)
            gate_terms = _gate_terms(_token_major(aff[row0 // LANES:(row0 + n) // LANES]))
            routed.append((slot, offs, _gather(offs, slot, h2, gate_terms, row0=row0, n=n, cap=cap)))
        ye = _experts(routed[0][2], routed[1][2], w_gate, w_up, w_down, l)
        outs = []
        for (row0, n, cap), (slot, offs, _), y in zip(groups, routed, ye):
            outs.append(_combine(offs, y, _token_major(slot), x_mid, mod_l, ln_post_ffn[l], row0=row0, n=n))
        x_pair = tuple(outs)

    y_prompt = x_pair[0].reshape(BATCH, SEQ, D)
    y_sample = x_pair[1].reshape(DEC_BATCH, DEC_SEQ, D)
    new_k, new_v = _kv_proj(h_layers, w_in)
    return (y_prompt, y_sample, new_k, new_v, jnp.stack(new_s, axis=1))
```

```python
import functools

import numpy as np
import jax
import jax.numpy as jnp
from jax import lax
from jax.experimental import pallas as pl
from jax.experimental.pallas import tpu as pltpu

F32 = jnp.float32
BF16 = jnp.bfloat16
I32 = jnp.int32

D = 1024
BATCH, SEQ = 32, 256
DEC_BATCH, DEC_SEQ = 2, 2048
DEPTH = 2
NP = BATCH * SEQ
NS = DEC_BATCH * DEC_SEQ
NT = NP + NS
GRID_W = 64
N_MOD = 6
EPS = 1e-6
ROPE_BASE = 10000.0
HEAD_DIM = 64
N_PAIRS = 4
CHUNK = 256
POOL_WINDOWS = (2, 4, 8, 16)
POOL_PAD = 16
NA_KH, NA_KW = 8, 16
N_EXPERTS = 16
EXPERT_FF = 2048
IN_COLS = 7168
LANES = 128
VMEM_LIMIT = 56 * 1024 * 1024

NT_DIMS = (((1,), (1,)), ((), ()))


def _params(sem, vmem=None):
    return pltpu.CompilerParams(dimension_semantics=sem, vmem_limit_bytes=vmem)


def _mod_row(row_start):
    return jnp.where(row_start < NP, 0, 1 + (row_start - NP) // DEC_SEQ)


def _silu(x):
    return x * jax.nn.sigmoid(x)


def _rms(x):
    return x * lax.rsqrt(jnp.mean(x * x, axis=-1, keepdims=True) + EPS)


def _pair_specs(tm, width):
    n_p = NP // tm
    return [pl.BlockSpec((tm, width), lambda i: (jnp.minimum(i, n_p - 1), 0)),
            pl.BlockSpec((tm, width), lambda i: (jnp.maximum(i - n_p, 0), 0))]


def _pick(p_ref, s_ref):
    return jnp.where(pl.program_id(0) < NP // p_ref.shape[0], p_ref[...], s_ref[...])


def _mod_kernel(c_ref, w_ref, b_ref, o_ref):
    a = _silu(c_ref[...]).astype(BF16)
    o_ref[0] = jnp.dot(a, w_ref[0].astype(BF16), preferred_element_type=F32) + b_ref[0]


def _modulation(cvecs, w_mod, b_mod):
    out = pl.pallas_call(
        _mod_kernel,
        name="modulation",
        grid=(DEPTH, N_MOD),
        in_specs=[pl.BlockSpec((8, D), lambda l, j: (0, 0)),
                  pl.BlockSpec((1, D, D), lambda l, j: (l, 0, j)),
                  pl.BlockSpec((1, 1, D), lambda l, j: (l, 0, j))],
        out_specs=pl.BlockSpec((1, 8, D), lambda l, j: (l, 0, j)),
        out_shape=jax.ShapeDtypeStruct((DEPTH, 8, N_MOD * D), F32),
        compiler_params=_params(("arbitrary", "arbitrary")),
    )(cvecs, w_mod, b_mod.reshape(DEPTH, 1, N_MOD * D))
    return out.reshape(DEPTH, 8, N_MOD, D)


def _prenorm_kernel(xp_ref, xs_ref, m_ref, ln_ref, h_ref):
    y = _rms(_pick(xp_ref, xs_ref)) * ln_ref[...]
    h_ref[...] = (y * (1.0 + m_ref[0, 1:2, :]) + m_ref[0, 0:1, :]).astype(BF16)


def _prenorm(x_pair, mod_l, ln):
    tm = 512
    return pl.pallas_call(
        _prenorm_kernel,
        name="prenorm",
        grid=(NT // tm,),
        in_specs=_pair_specs(tm, D)
                 + [pl.BlockSpec((1, N_MOD, D), lambda i: (_mod_row(i * tm), 0, 0)),
                  pl.BlockSpec((1, D), lambda i: (0, 0))],
        out_specs=pl.BlockSpec((tm, D), lambda i: (i, 0)),
        out_shape=jax.ShapeDtypeStruct((NT, D), BF16),
        compiler_params=_params(("arbitrary",)),
    )(*x_pair, mod_l, ln.reshape(1, D))


def _mm_kernel(a_ref, w_ref, o_ref, wb_ref):
    @pl.when(pl.program_id(1) == 0)
    def _():
        wb_ref[...] = w_ref[...].astype(BF16)

    o_ref[...] = jnp.dot(a_ref[...], wb_ref[...], preferred_element_type=F32)


def _kv_kernel(h0_ref, h1_ref, w_ref, k_ref, v_ref, wb_ref):
    @pl.when(pl.program_id(1) == 0)
    def _():
        wb_ref[...] = w_ref[0].astype(BF16)

    h = jnp.where(pl.program_id(0) == 0, h0_ref[...], h1_ref[...])
    kv = jnp.dot(h, wb_ref[...], preferred_element_type=F32)
    half = N_PAIRS * LANES
    k_ref[...] = kv[:, :half].reshape(k_ref.shape)
    v_ref[...] = kv[:, half:].reshape(v_ref.shape)


def _kv_proj(h_layers, w_in):
    per = 4
    kv_col = 3072 // D
    hspec = pl.BlockSpec((per * SEQ, D), lambda l, i: (i, 0))
    ospec = pl.BlockSpec((per, 1, SEQ, N_PAIRS * LANES), lambda l, i: (i, l, 0, 0))
    shape = jax.ShapeDtypeStruct((BATCH, DEPTH, SEQ, N_PAIRS * LANES), F32)
    k, v = pl.pallas_call(
        _kv_kernel,
        name="kv_proj",
        grid=(DEPTH, BATCH // per),
        in_specs=[hspec, hspec, pl.BlockSpec((1, D, D), lambda l, i: (l, 0, kv_col))],
        out_specs=[ospec, ospec],
        out_shape=[shape, shape],
        scratch_shapes=[pltpu.VMEM((D, D), BF16)],
        compiler_params=_params(("arbitrary", "arbitrary"), VMEM_LIMIT),
    )(*h_layers, w_in)
    cache_shape = (BATCH, DEPTH, SEQ, 2 * N_PAIRS, HEAD_DIM)
    return k.reshape(cache_shape), v.reshape(cache_shape)


def _in_proj(h_all, w_in, layer):
    tm, tn = 1024, 1792
    return pl.pallas_call(
        _mm_kernel,
        name="in_proj",
        grid=(IN_COLS // tn, NT // tm),
        in_specs=[pl.BlockSpec((tm, D), lambda j, i: (i, 0)),
                  pl.BlockSpec((None, D, tn), lambda j, i: (layer, 0, j))],
        out_specs=pl.BlockSpec((tm, tn), lambda j, i: (i, j)),
        out_shape=jax.ShapeDtypeStruct((NT, IN_COLS), F32),
        scratch_shapes=[pltpu.VMEM((D, tn), BF16)],
        compiler_params=_params(("arbitrary", "arbitrary"), VMEM_LIMIT),
    )(h_all, w_in)


def _swap16(x):
    lane = lax.broadcasted_iota(I32, x.shape, 1)
    return jnp.where((lane // 16) % 2 == 0, pltpu.roll(x, LANES - 16, 1), pltpu.roll(x, 16, 1))


def _block_diag(top, bottom):
    z = jnp.zeros((HEAD_DIM, HEAD_DIM), F32)
    return jnp.concatenate([jnp.concatenate([top, z], axis=1),
                            jnp.concatenate([z, bottom], axis=1)], axis=0)


RET_PAIRS_PER_STEP = 2


def _retention_kernel(lg_ref, q_ref, k_ref, v_ref, g_ref, cos_ref, sin_ref, s0_ref, o_ref, sf_scr, sb_scr):
    for pp in range(RET_PAIRS_PER_STEP):
        _retention_pair(lg_ref, q_ref, k_ref, v_ref, g_ref, cos_ref, sin_ref, s0_ref, o_ref, sf_scr, sb_scr,
                        pp, pl.program_id(1) * RET_PAIRS_PER_STEP + pp)


def _retention_pair(lg_ref, q_ref, k_ref, v_ref, g_ref, cos_ref, sin_ref, s0_ref, o_ref, sf_scr, sb_scr, pp, pair):
    n_chunks = DEC_SEQ // CHUNK
    lanes = slice(pp * LANES, (pp + 1) * LANES)
    lane1 = lax.broadcasted_iota(I32, (1, LANES), 1)
    lo1 = lane1 < HEAD_DIM
    lgf = jnp.where(lo1, lg_ref[0, 2 * pair], lg_ref[0, 2 * pair + 1])
    lgb = jnp.where(lo1, lg_ref[1, 2 * pair], lg_ref[1, 2 * pair + 1])
    lg_heads = [(lg_ref[0, 2 * pair], lg_ref[1, 2 * pair]),
                (lg_ref[0, 2 * pair + 1], lg_ref[1, 2 * pair + 1])]

    rel = (lax.broadcasted_iota(I32, (CHUNK, CHUNK), 0) - lax.broadcasted_iota(I32, (CHUNK, CHUNK), 1)).astype(F32)
    lo_mask = lax.broadcasted_iota(I32, (CHUNK, LANES), 1) < HEAD_DIM
    blockdiag = ((lax.broadcasted_iota(I32, (LANES, LANES), 0) < HEAD_DIM)
                 == (lax.broadcasted_iota(I32, (LANES, LANES), 1) < HEAD_DIM))
    posf = lax.broadcasted_iota(I32, (CHUNK, LANES), 0).astype(F32)
    dmat = []
    for hf, hb in lg_heads:
        dmat.append(jnp.where(rel >= 0, jnp.exp(jnp.where(rel >= 0, rel, 0.0) * hf), 0.0)
                    + jnp.where(rel <= 0, jnp.exp(jnp.where(rel <= 0, -rel, 0.0) * hb), 0.0))
    qdec_f = jnp.exp((posf + 1.0) * lgf)
    kdec_f = jnp.exp((CHUNK - 1.0 - posf) * lgf)
    qdec_b = jnp.exp((CHUNK - posf) * lgb)
    kdec_b = jnp.exp(posf * lgb)
    sdec_f = jnp.exp(CHUNK * lgf)
    sdec_b = jnp.exp(CHUNK * lgb)

    def load(c):
        rows = pl.ds(c * CHUNK, CHUNK)
        cs, sn = cos_ref[rows, :], sin_ref[rows, :]
        q = q_ref[rows, lanes]
        k = k_ref[rows, lanes]
        q = q * cs + _swap16(q) * sn
        k = k * cs + _swap16(k) * sn
        return q, k * (HEAD_DIM ** -0.5), v_ref[rows, lanes]

    def state_update(s, k, v, kdec, sdec):
        kd = (k * kdec).T.astype(BF16)
        u = jnp.dot(kd, v.astype(BF16), preferred_element_type=F32)
        return s * sdec + jnp.where(blockdiag, u, 0.0)

    h0, h1 = 2 * pp, 2 * pp + 1
    s_f = _block_diag(s0_ref[0, 0, 0, h0], s0_ref[0, 0, 0, h1])
    s_b = _block_diag(s0_ref[0, 0, 1, h0], s0_ref[0, 0, 1, h1])
    scr0 = pp * n_chunks
    for c in range(n_chunks):
        sf_scr[scr0 + c] = s_f
        _, k, v = load(c)
        s_f = state_update(s_f, k, v, kdec_f, sdec_f)
    for c in reversed(range(n_chunks)):
        sb_scr[scr0 + c] = s_b
        _, k, v = load(c)
        s_b = state_update(s_b, k, v, kdec_b, sdec_b)

    for c in range(n_chunks):
        q, k, v = load(c)
        qb, kb, vb = q.astype(BF16), k.astype(BF16), v.astype(BF16)
        outs = []
        for h in range(2):
            qh = jnp.where(lo_mask if h == 0 else ~lo_mask, qb, jnp.zeros_like(qb))
            a = lax.dot_general(qh, kb, NT_DIMS, preferred_element_type=F32) * dmat[h]
            outs.append(jnp.dot(a.astype(BF16), vb, preferred_element_type=F32))
        o = jnp.where(lo_mask, outs[0], outs[1])
        o = o + jnp.dot(qb, sf_scr[scr0 + c].astype(BF16), preferred_element_type=F32) * qdec_f
        o = o + jnp.dot(qb, sb_scr[scr0 + c].astype(BF16), preferred_element_type=F32) * qdec_b
        o2 = o * o
        ms0 = jnp.sum(jnp.where(lo_mask, o2, 0.0), axis=1, keepdims=True) * (1.0 / HEAD_DIM)
        ms1 = jnp.sum(jnp.where(lo_mask, 0.0, o2), axis=1, keepdims=True) * (1.0 / HEAD_DIM)
        inv = jnp.where(lo_mask, lax.rsqrt(ms0 + EPS), lax.rsqrt(ms1 + EPS))
        g = g_ref[pl.ds(c * CHUNK, CHUNK), lanes]
        o_ref[pl.ds(c * CHUNK, CHUNK), lanes] = (_silu(g) * (o * inv)).astype(BF16)


def _retention(z, log_g, rope, state_ret, layer):
    pps = RET_PAIRS_PER_STEP
    width = pps * LANES
    row_block0 = NP // DEC_SEQ
    scratch = pltpu.VMEM((pps * (DEC_SEQ // CHUNK), LANES, LANES), F32)

    def zspec(cb):
        return pl.BlockSpec((DEC_SEQ, width), lambda b, p: (row_block0 + b, cb // pps + p))

    table = pl.BlockSpec((DEC_SEQ, LANES), lambda b, p: (0, 0))
    return pl.pallas_call(
        _retention_kernel,
        name="retention",
        grid=(DEC_BATCH, N_PAIRS // pps),
        in_specs=[pl.BlockSpec(memory_space=pltpu.SMEM), zspec(0), zspec(4), zspec(8), zspec(12), table, table,
                  pl.BlockSpec((1, 1, 2, 2 * pps, HEAD_DIM, HEAD_DIM), lambda b, p: (b, layer, 0, p, 0, 0))],
        out_specs=pl.BlockSpec((DEC_SEQ, width), lambda b, p: (b, p)),
        out_shape=jax.ShapeDtypeStruct((NS, N_PAIRS * LANES), BF16),
        scratch_shapes=[scratch, scratch],
        compiler_params=_params(("arbitrary", "arbitrary"), VMEM_LIMIT),
    )(log_g, z, z, z, z, *rope, state_ret)


def _retention_ctx_kernel(lg_ref, q_ref, k_ref, v_ref, g_ref, o_ref, st_ref, decay_ref):
    heads = 2 * N_PAIRS

    @pl.when(pl.program_id(0) == 0)
    def _():
        rel = (lax.broadcasted_iota(I32, (SEQ, SEQ), 0) - lax.broadcasted_iota(I32, (SEQ, SEQ), 1)).astype(F32)
        for h in range(heads):
            decay_ref[h] = (jnp.where(rel >= 0, jnp.exp(jnp.where(rel >= 0, rel, 0.0) * lg_ref[0, h]), 0.0)
                            + jnp.where(rel <= 0, jnp.exp(jnp.where(rel <= 0, -rel, 0.0) * lg_ref[1, h]), 0.0))

    lane = lax.broadcasted_iota(I32, (SEQ, LANES), 1)
    lo_mask = lane < HEAD_DIM
    pos = lax.broadcasted_iota(I32, (SEQ, LANES), 0).astype(F32)
    blockdiag = ((lax.broadcasted_iota(I32, (LANES, LANES), 0) < HEAD_DIM)
                 == (lax.broadcasted_iota(I32, (LANES, LANES), 1) < HEAD_DIM))
    for pp in range(N_PAIRS):
        lanes = slice(pp * LANES, (pp + 1) * LANES)
        h0, h1 = 2 * pp, 2 * pp + 1
        k = k_ref[:, lanes] * (HEAD_DIM ** -0.5)
        qb, kb, vb = q_ref[:, lanes].astype(BF16), k.astype(BF16), v_ref[:, lanes].astype(BF16)
        outs = []
        for h, mask in ((h0, lo_mask), (h1, ~lo_mask)):
            qh = jnp.where(mask, qb, jnp.zeros_like(qb))
            a = lax.dot_general(qh, kb, NT_DIMS, preferred_element_type=F32) * decay_ref[h]
            outs.append(jnp.dot(a.astype(BF16), vb, preferred_element_type=F32))
        o = jnp.where(lo_mask, outs[0], outs[1])
        o2 = o * o
        ms0 = jnp.sum(jnp.where(lo_mask, o2, 0.0), axis=1, keepdims=True) * (1.0 / HEAD_DIM)
        ms1 = jnp.sum(jnp.where(lo_mask, 0.0, o2), axis=1, keepdims=True) * (1.0 / HEAD_DIM)
        inv = jnp.where(lo_mask, lax.rsqrt(ms0 + EPS), lax.rsqrt(ms1 + EPS))
        o_ref[:, lanes] = (_silu(g_ref[:, lanes]) * (o * inv)).astype(BF16)

        for d, age in ((0, SEQ - 1.0 - pos), (1, pos)):
            lg = jnp.where(lo_mask, lg_ref[d, h0], lg_ref[d, h1])
            kd = (k * jnp.exp(age * lg)).T.astype(BF16)
            s = jnp.where(blockdiag, jnp.dot(kd, vb, preferred_element_type=F32), 0.0)
            st_ref[0, d, h0] = s[:HEAD_DIM, :HEAD_DIM]
            st_ref[0, d, h1] = s[HEAD_DIM:, HEAD_DIM:]


def _retention_ctx(z, log_g):
    width = N_PAIRS * LANES
    heads = 2 * N_PAIRS
    zspec = lambda cb: pl.BlockSpec((SEQ, width), lambda b: (b, cb))
    return pl.pallas_call(
        _retention_ctx_kernel,
        name="retention_ctx",
        grid=(BATCH,),
        in_specs=[pl.BlockSpec(memory_space=pltpu.SMEM), zspec(0), zspec(1), zspec(2), zspec(3)],
        out_specs=[pl.BlockSpec((SEQ, width), lambda b: (b, 0)),
                   pl.BlockSpec((1, 2, heads, HEAD_DIM, HEAD_DIM), lambda b: (b, 0, 0, 0, 0))],
        out_shape=[jax.ShapeDtypeStruct((NP, width), BF16),
                   jax.ShapeDtypeStruct((BATCH, 2, heads, HEAD_DIM, HEAD_DIM), F32)],
        scratch_shapes=[pltpu.VMEM((heads, SEQ, SEQ), F32)],
        compiler_params=_params(("arbitrary",)),
    )(log_g, z, z, z, z)


def _rope_tables():
    t = np.arange(DEC_SEQ)
    posn = [(t // GRID_W).astype(np.float32), (t % GRID_W).astype(np.float32)]
    nf = HEAD_DIM // 4
    freqs = (1.0 / (np.float32(ROPE_BASE) ** (np.arange(nf, dtype=np.float32) / np.float32(nf)))).astype(np.float32)
    cos = np.zeros((DEC_SEQ, HEAD_DIM), np.float32)
    sin = np.zeros((DEC_SEQ, HEAD_DIM), np.float32)
    for half in range(2):
        ang = (posn[half][:, None] * freqs[None, :]).astype(np.float32)
        for grp in range(2):
            lo = half * 32 + grp * nf
            cos[:, lo:lo + nf] = np.cos(ang)
            sin[:, lo:lo + nf] = np.sin(ang) * (-1.0 if grp == 0 else 1.0)
    return jnp.asarray(np.tile(cos, (1, 2))), jnp.asarray(np.tile(sin, (1, 2)))


def _pool_kernel(u_ref, w_ref, sc_ref, o_ref, *, seq):
    padded = seq + 2 * POOL_PAD
    t = lax.broadcasted_iota(I32, (seq, 1), 0)
    zpad = jnp.zeros((POOL_PAD, LANES), F32)
    for gi, w in enumerate(POOL_WINDOWS):
        x = u_ref[:, gi * LANES:(gi + 1) * LANES]
        run = jnp.concatenate([zpad, x, zpad], axis=0)
        span = 1
        while span < w:
            run = run + pltpu.roll(run, padded - span, 0)
            span *= 2
        win = pltpu.roll(run, padded - (POOL_PAD - w // 2), 0)[:seq]
        cnt = (jnp.minimum(t + w // 2, seq) - jnp.maximum(t - w // 2, 0)).astype(F32)
        pooled = win / cnt - x
        mixed = jnp.dot(pooled.astype(BF16), w_ref[gi].astype(BF16), preferred_element_type=F32)
        o_ref[:, gi * LANES:(gi + 1) * LANES] = (mixed * sc_ref[:, gi * LANES:(gi + 1) * LANES]).astype(BF16)


def _pool(z, pool_w, pool_scale, *, nb, seq, row_block0):
    width = len(POOL_WINDOWS) * LANES
    return pl.pallas_call(
        functools.partial(_pool_kernel, seq=seq),
        name="pool",
        grid=(nb,),
        in_specs=[pl.BlockSpec((seq, width), lambda b: (row_block0 + b, 2048 // width)),
                  pl.BlockSpec((len(POOL_WINDOWS), LANES, LANES), lambda b: (0, 0, 0)),
                  pl.BlockSpec((1, width), lambda b: (0, 0))],
        out_specs=pl.BlockSpec((seq, width), lambda b: (b, 0)),
        out_shape=jax.ShapeDtypeStruct((nb * seq, width), BF16),
        compiler_params=_params(("arbitrary",), VMEM_LIMIT),
    )(z, pool_w, pool_scale.reshape(1, width))


def _head_select(h, shape):
    lane = lax.broadcasted_iota(I32, shape, 1)
    return (lane < HEAD_DIM) if h == 0 else (lane >= HEAD_DIM)


def _ctx_attn_kernel(q_ref, k_ref, v_ref, o_ref):
    for pp in range(N_PAIRS):
        lanes = slice(pp * LANES, (pp + 1) * LANES)
        qb = (q_ref[:, lanes] * (HEAD_DIM ** -0.5)).astype(BF16)
        kb = k_ref[:, lanes].astype(BF16)
        vb = v_ref[:, lanes].astype(BF16)
        outs = []
        for h in range(2):
            qh = jnp.where(_head_select(h, qb.shape), qb, jnp.zeros_like(qb))
            s = lax.dot_general(qh, kb, NT_DIMS, preferred_element_type=F32)
            p = jnp.exp(s - jnp.max(s, axis=1, keepdims=True))
            denom = jnp.sum(p, axis=1, keepdims=True)
            outs.append(jnp.dot(p.astype(BF16), vb, preferred_element_type=F32) / denom)
        o_ref[:, lanes] = jnp.where(_head_select(0, outs[0].shape), outs[0], outs[1]).astype(BF16)


def _ctx_attention(z):
    width = N_PAIRS * LANES

    def zspec(cb):
        return pl.BlockSpec((SEQ, width), lambda b: (b, cb))

    return pl.pallas_call(
        _ctx_attn_kernel,
        name="ctx_attn",
        grid=(BATCH,),
        in_specs=[zspec(5), zspec(6), zspec(7)],
        out_specs=pl.BlockSpec((SEQ, width), lambda b: (b, 0)),
        out_shape=jax.ShapeDtypeStruct((NP, width), BF16),
        compiler_params=_params(("arbitrary",)),
    )(z, z, z)


NA_QROWS = 4
NA_QBLK = NA_QROWS * GRID_W
NA_KROWS = 12
NA_NBLK = DEC_SEQ // NA_QBLK


def _na_key_block(i):
    return jnp.clip(i - 1, 0, NA_NBLK - 3)


NA_PATTERNS = ((0, 0), (NA_QROWS, 0), (DEC_SEQ // GRID_W - NA_QROWS, DEC_SEQ // GRID_W - NA_KROWS))
NA_DX_LANE = GRID_W - (NA_KW - 1)


def _na_bias_kernel(rpb_ref, o_ref):
    rows = DEC_SEQ // GRID_W
    q = lax.broadcasted_iota(I32, (GRID_W, LANES), 0)
    lane = lax.broadcasted_iota(I32, (GRID_W, LANES), 1)
    c = lane % GRID_W
    c_start = jnp.clip(q - NA_KW // 2, 0, GRID_W - NA_KW)
    col_ok = (c >= c_start) & (c < c_start + NA_KW)
    lower = lane < GRID_W
    for p, (r0, ks) in enumerate(NA_PATTERNS):
        for rr in range(NA_QROWS):
            r = r0 + rr
            start = min(max(r - NA_KH // 2, 0), rows - NA_KH)
            for kp in range(NA_KROWS // 2):
                halves = []
                for half in range(2):
                    kr = ks + 2 * kp + half
                    if start <= kr < start + NA_KH:
                        row = jnp.broadcast_to(rpb_ref[0, 0, pl.ds(kr - r + NA_KH - 1, 1), :], (GRID_W, LANES))
                        halves.append(pltpu.roll(row, GRID_W * (1 - half), 1, stride=1, stride_axis=0))
                    else:
                        halves.append(None)
                neg = jnp.full((GRID_W, LANES), -jnp.inf, F32)
                lo_half = neg if halves[0] is None else jnp.where(col_ok, halves[0], neg)
                hi_half = neg if halves[1] is None else jnp.where(col_ok, halves[1], neg)
                o_ref[0, p, 0, rr * GRID_W:(rr + 1) * GRID_W, kp * LANES:(kp + 1) * LANES] = (
                    jnp.where(lower, lo_half, hi_half))


def _na_bias(na_rpb):
    ny, nx = 2 * NA_KH - 1, 2 * NA_KW - 1
    padded = jnp.pad(na_rpb.astype(F32), ((0, 0), (0, 0), (0, 16 - ny), (NA_DX_LANE, LANES - NA_DX_LANE - nx)))
    heads = 2 * N_PAIRS
    return pl.pallas_call(
        _na_bias_kernel,
        name="nbr_bias",
        grid=(DEPTH, heads),
        in_specs=[pl.BlockSpec((1, 1, 16, LANES), lambda l, h: (l, h, 0, 0))],
        out_specs=pl.BlockSpec((1, len(NA_PATTERNS), 1, NA_QBLK, NA_KROWS * GRID_W), lambda l, h: (l, 0, h, 0, 0)),
        out_shape=jax.ShapeDtypeStruct((DEPTH, len(NA_PATTERNS), heads, NA_QBLK, NA_KROWS * GRID_W), F32),
        compiler_params=_params(("arbitrary", "arbitrary")),
    )(padded)


def _na_kernel(q_ref, k0_ref, k1_ref, k2_ref, v0_ref, v1_ref, v2_ref, ck_ref, cv_ref, bias_ref, o_ref):
    for pp in range(N_PAIRS):
        lanes = slice(pp * LANES, (pp + 1) * LANES)
        qb = (q_ref[:, lanes] * (HEAD_DIM ** -0.5)).astype(BF16)
        ks = [r[:, lanes].astype(BF16) for r in (k0_ref, k1_ref, k2_ref)] + [ck_ref[0, 0, :, lanes].astype(BF16)]
        vs = [r[:, lanes].astype(BF16) for r in (v0_ref, v1_ref, v2_ref)] + [cv_ref[0, 0, :, lanes].astype(BF16)]
        outs = []
        for h in range(2):
            qh = jnp.where(_head_select(h, qb.shape), qb, jnp.zeros_like(qb))
            ss = []
            for j in range(4):
                s = lax.dot_general(qh, ks[j], NT_DIMS, preferred_element_type=F32)
                if j < 3:
                    s = s + bias_ref[0, 2 * pp + h, :, j * NA_QBLK:(j + 1) * NA_QBLK]
                ss.append(s)
            m = functools.reduce(jnp.maximum, [jnp.max(s, axis=1, keepdims=True) for s in ss])
            ps = [jnp.exp(s - m) for s in ss]
            denom = functools.reduce(jnp.add, [jnp.sum(p, axis=1, keepdims=True) for p in ps])
            acc = functools.reduce(jnp.add, [jnp.dot(p.astype(BF16), v, preferred_element_type=F32)
                                             for p, v in zip(ps, vs)])
            outs.append(acc / denom)
        o_ref[:, lanes] = jnp.where(_head_select(0, outs[0].shape), outs[0], outs[1]).astype(BF16)


def _neighbourhood_attention(z, bias, cache_k, cache_v, layer):
    base = NP // NA_QBLK
    width = N_PAIRS * LANES
    heads = 2 * N_PAIRS

    def kvspec(cb, j):
        return pl.BlockSpec((NA_QBLK, width), lambda b, i: (base + b * NA_NBLK + _na_key_block(i) + j, cb))

    cspec = pl.BlockSpec((1, 1, SEQ, width), lambda b, i: (b, layer, 0, 0))
    pattern = lambda i: jnp.where(i == 0, 0, jnp.where(i == NA_NBLK - 1, 2, 1))
    ck = cache_k.reshape(DEC_BATCH, DEPTH, SEQ, width)
    cv = cache_v.reshape(DEC_BATCH, DEPTH, SEQ, width)
    return pl.pallas_call(
        _na_kernel,
        name="nbr_attn",
        grid=(DEC_BATCH, NA_NBLK),
        in_specs=[pl.BlockSpec((NA_QBLK, width), lambda b, i: (base + b * NA_NBLK + i, 5))]
                 + [kvspec(6, j) for j in range(3)] + [kvspec(7, j) for j in range(3)]
                 + [cspec, cspec,
                    pl.BlockSpec((None, 1, heads, NA_QBLK, NA_KROWS * GRID_W),
                                 lambda b, i: (layer, pattern(i), 0, 0, 0))],
        out_specs=pl.BlockSpec((NA_QBLK, width), lambda b, i: (b * NA_NBLK + i, 0)),
        out_shape=jax.ShapeDtypeStruct((NS, width), BF16),
        compiler_params=_params(("arbitrary", "arbitrary"), VMEM_LIMIT),
    )(z, z, z, z, z, z, z, ck, cv, bias)


def _merge_kernel(rp_ref, rs_ref, pp_ref, ps_ref, ap_ref, as_ref, xp_ref, xs_ref, g0_ref, g1_ref, g2_ref,
                  wr_ref, wp_ref, wa_ref, wo_ref, m_ref, ln_ref, ln_ffn_ref, wrt_ref,
                  o_ref, h_ref, aff_ref, wrb, wpb, wab, wob):
    @pl.when(pl.program_id(0) == 0)
    def _():
        wrb[...] = wr_ref[...].astype(BF16)
        wpb[...] = wp_ref[...].astype(BF16)
        wab[...] = wa_ref[...].astype(BF16)
        wob[...] = wo_ref[...].astype(BF16)

    branch = lambda p_ref, s_ref, w: jnp.dot(_pick(p_ref, s_ref), w[...], preferred_element_type=F32)
    merged = (jax.nn.sigmoid(g0_ref[...]) * branch(rp_ref, rs_ref, wrb)
              + jax.nn.sigmoid(g1_ref[...]) * branch(pp_ref, ps_ref, wpb)
              + jax.nn.sigmoid(g2_ref[...]) * branch(ap_ref, as_ref, wab))
    mix = jnp.dot(merged.astype(BF16), wob[...], preferred_element_type=F32)
    x = _pick(xp_ref, xs_ref) + m_ref[0, 2:3, :] * (_rms(mix) * ln_ref[...])
    o_ref[...] = x

    h = (_rms(x) * ln_ffn_ref[...]) * (1.0 + m_ref[0, 4:5, :]) + m_ref[0, 3:4, :]
    hb = h.astype(BF16)
    h_ref[...] = hb
    hl = (h - hb.astype(F32)).astype(BF16)
    w = wrt_ref[...]
    wb = w.astype(BF16)
    wl = (w - wb.astype(F32)).astype(BF16)
    logits = (lax.dot_general(wb, hb, NT_DIMS, preferred_element_type=F32)
              + lax.dot_general(wb, hl, NT_DIMS, preferred_element_type=F32)
              + lax.dot_general(wl, hb, NT_DIMS, preferred_element_type=F32))
    e = jnp.exp(logits - jnp.max(logits, axis=0, keepdims=True))
    aff = e / jnp.sum(e, axis=0, keepdims=True)
    for j in range(aff_ref.shape[0]):
        aff_ref[j] = aff[:, j * LANES:(j + 1) * LANES]


def _merge(ret_pair, pool_pair, na_pair, x_pair, z, w_ret_o, w_pool_o, w_na_o, w_o, mod_l, ln, ln_ffn, w_router_t,
           layer):
    tm = 512
    half = N_PAIRS * LANES
    row = lambda i: (i, 0)
    const = lambda i: (0, 0)
    slab = lambda i: (layer, 0, 0)
    return pl.pallas_call(
        _merge_kernel,
        name="merge",
        grid=(NT // tm,),
        in_specs=_pair_specs(tm, half) * 3 + _pair_specs(tm, D)
                 + [pl.BlockSpec((tm, D), lambda i, c=c: (i, 4 + c)) for c in range(3)]
                 + [pl.BlockSpec((None, half, D), slab)] * 3
                 + [pl.BlockSpec((None, D, D), slab),
                    pl.BlockSpec((1, N_MOD, D), lambda i: (_mod_row(i * tm), 0, 0)),
                    pl.BlockSpec((1, D), const),
                    pl.BlockSpec((1, D), const),
                    pl.BlockSpec((N_EXPERTS, D), const)],
        out_specs=[pl.BlockSpec((tm, D), row),
                   pl.BlockSpec((tm, D), row),
                   pl.BlockSpec((tm // LANES, N_EXPERTS, LANES), lambda i: (i, 0, 0))],
        out_shape=[jax.ShapeDtypeStruct((NT, D), F32),
                   jax.ShapeDtypeStruct((NT, D), BF16),
                   jax.ShapeDtypeStruct((NT // LANES, N_EXPERTS, LANES), F32)],
        scratch_shapes=[pltpu.VMEM((half, D), BF16)] * 3 + [pltpu.VMEM((D, D), BF16)],
        compiler_params=_params(("arbitrary",), VMEM_LIMIT),
    )(*ret_pair, *pool_pair, *na_pair, *x_pair, z, z, z, w_ret_o, w_pool_o, w_na_o, w_o, mod_l, ln.reshape(1, D),
      ln_ffn.reshape(1, D), w_router_t)


def _route_kernel(aff_ref, slot_ref, offs_ref, *, cap, nblk):
    as_float = lambda bits: lax.bitcast_convert_type(bits, F32)

    def count(pred):
        return jnp.sum(jnp.sum(jnp.where(pred, 1.0, 0.0), axis=0), axis=1, keepdims=True)

    def search(_, lohi):
        lo, hi = lohi
        mid = lo + ((hi - lo + 1) >> 1)
        ok = count(aff_ref[...] >= as_float(mid)[None]) >= cap
        return jnp.where(ok, mid, lo), jnp.where(ok, hi, mid - 1)

    lo0 = jnp.zeros((N_EXPERTS, 1), I32)
    hi0 = jnp.full((N_EXPERTS, 1), 0x7F800000, I32)
    thr_bits, _ = lax.fori_loop(0, 31, search, (lo0, hi0))
    thr = as_float(thr_bits)
    need = cap - count(aff_ref[...] > thr[None])

    upper = (lax.broadcasted_iota(I32, (LANES, LANES), 0)
             < lax.broadcasted_iota(I32, (LANES, LANES), 1)).astype(BF16)

    def running_count(flags):
        inside = jnp.dot(flags.reshape(nblk * N_EXPERTS, LANES).astype(BF16), upper,
                         preferred_element_type=F32).reshape(nblk, N_EXPERTS, LANES)
        totals = jnp.sum(flags, axis=2, keepdims=True)
        before, run = [], jnp.zeros((N_EXPERTS, 1), F32)
        for b in range(nblk):
            before.append(run)
            run = run + totals[b]
        return inside + jnp.stack(before), before, run

    aff = aff_ref[...]
    tied = aff == thr[None]
    tied_rank, _, _ = running_count(jnp.where(tied, 1.0, 0.0))
    chosen = (aff > thr[None]) | (tied & (tied_rank < need[None]))
    rank, before, total = running_count(jnp.where(chosen, 1.0, 0.0))
    slot_ref[...] = jnp.where(chosen, rank, -1.0)

    lane = lax.broadcasted_iota(I32, (N_EXPERTS, LANES), 1)
    offs = jnp.where(lane >= nblk, total, 0.0)
    for b in range(nblk):
        offs = jnp.where(lane == b, before[b], offs)
    offs_ref[...] = offs.astype(I32)


def _route(aff_blocks, *, blk0, nblk, cap):
    return pl.pallas_call(
        functools.partial(_route_kernel, cap=cap, nblk=nblk),
        name="route",
        grid=(1,),
        in_specs=[pl.BlockSpec((nblk, N_EXPERTS, LANES), lambda i: (blk0 // nblk, 0, 0))],
        out_specs=[pl.BlockSpec((nblk, N_EXPERTS, LANES), lambda i: (0, 0, 0)),
                   pl.BlockSpec((N_EXPERTS, LANES), lambda i: (0, 0))],
        out_shape=[jax.ShapeDtypeStruct((nblk, N_EXPERTS, LANES), F32),
                   jax.ShapeDtypeStruct((N_EXPERTS, LANES), I32)],
        compiler_params=_params(("arbitrary",)),
    )(aff_blocks)


GATHER_TOKENS = 256
TILE_BLOCKS = GATHER_TOKENS // LANES
SLOT_ALIGN = 16
WINDOW = LANES // 2
ROUND_SLOTS = WINDOW - SLOT_ALIGN
GATE_TERMS = 3
ROW_W = D + LANES


def _round_bounds(offs_ref, e, t, r):
    off0 = offs_ref[e, TILE_BLOCKS * t]
    off1 = offs_ref[e, TILE_BLOCKS * t + TILE_BLOCKS]
    lo = jnp.minimum(off0 + ROUND_SLOTS * r, off1)
    hi = jnp.minimum(lo + ROUND_SLOTS, off1)
    return lo, hi, pl.multiple_of(lo & -SLOT_ALIGN, SLOT_ALIGN)


def _n_rounds(offs_ref, t):
    most = jnp.int32(0)
    for e in range(N_EXPERTS):
        most = jnp.maximum(most, offs_ref[e, TILE_BLOCKS * t + TILE_BLOCKS] - offs_ref[e, TILE_BLOCKS * t])
    rounds = jnp.int32(0)
    for filled in range(0, GATHER_TOKENS, ROUND_SLOTS):
        rounds = rounds + (most > filled).astype(I32)
    return rounds


def _gate_terms(aff_cols):
    hi = aff_cols.astype(BF16)
    rest = aff_cols - hi.astype(F32)
    mid = rest.astype(BF16)
    lo = (rest - mid.astype(F32)).astype(BF16)
    terms = jnp.stack([hi, mid, lo], axis=-1).reshape(aff_cols.shape[0], GATE_TERMS * N_EXPERTS)
    return jnp.pad(terms, ((0, 0), (0, LANES - GATE_TERMS * N_EXPERTS)))


def _gather_kernel(offs_ref, slot_ref, h_ref, g_ref, xe_hbm, stage_ref, carry_ref, sem, nround_ref, *, n_tiles):
    t = pl.program_id(0)
    cap = xe_hbm.shape[1] - WINDOW

    def out_copy(buf, e, start):
        return pltpu.make_async_copy(stage_ref.at[buf, e], xe_hbm.at[e, pl.ds(start, WINDOW)], sem.at[buf])

    def wait_round(buf):
        for e in range(N_EXPERTS):
            out_copy(buf, e, 0).wait()

    @pl.when(t == 0)
    def _():
        carry_ref[...] = jnp.zeros_like(carry_ref)
        nround_ref[0] = 0
        stage_ref[0, 0] = jnp.zeros((WINDOW, ROW_W), BF16)
        for e in range(N_EXPERTS):
            pltpu.make_async_copy(stage_ref.at[0, 0], xe_hbm.at[e, pl.ds(cap, WINDOW)], sem.at[0]).start()
        wait_round(0)

    hb = jnp.concatenate([h_ref[...], g_ref[...]], axis=1)
    sub = lax.broadcasted_iota(I32, (WINDOW, GATHER_TOKENS), 0).astype(F32)

    def one_round(r, carry):
        done = nround_ref[0]
        buf = done & 1
        bounds = [_round_bounds(offs_ref, e, t, r) for e in range(N_EXPERTS)]
        onehots = []
        for e in range(N_EXPERTS):
            lo, hi, start = bounds[e]
            srow = jnp.concatenate([slot_ref[j, e:e + 1, :] for j in range(TILE_BLOCKS)], axis=1)
            hit = ((srow - start.astype(F32) == sub) & (srow >= lo.astype(F32)) & (srow < hi.astype(F32)))
            onehots.append(jnp.where(hit, 1.0, 0.0).astype(BF16))
        rows = jnp.dot(jnp.concatenate(onehots, axis=0), hb, preferred_element_type=F32)
        for e in range(N_EXPERTS):
            lo, hi, start = bounds[e]
            piece = rows[e * WINDOW:(e + 1) * WINDOW]
            head = piece[:SLOT_ALIGN] + carry_ref[e].astype(F32)
            stage_ref[buf, e, :SLOT_ALIGN, :] = head.astype(BF16)
            stage_ref[buf, e, SLOT_ALIGN:, :] = piece[SLOT_ALIGN:].astype(BF16)
            tail = pl.multiple_of((hi & -SLOT_ALIGN) - start, SLOT_ALIGN)
            carry_ref[e] = stage_ref[buf, e, pl.ds(tail, SLOT_ALIGN), :]

        @pl.when(done > 0)
        def _():
            wait_round(1 - buf)

        for e in range(N_EXPERTS):
            out_copy(buf, e, bounds[e][2]).start()
        nround_ref[0] = done + 1
        return carry

    lax.fori_loop(0, _n_rounds(offs_ref, t), one_round, 0)

    @pl.when((t == n_tiles - 1) & (nround_ref[0] > 0))
    def _():
        wait_round((nround_ref[0] - 1) & 1)


def _gather(offs, slot, h_all, gate_terms, *, row0, n, cap):
    n_tiles = n // GATHER_TOKENS
    tile0 = row0 // GATHER_TOKENS
    return pl.pallas_call(
        functools.partial(_gather_kernel, n_tiles=n_tiles),
        name="gather",
        grid_spec=pltpu.PrefetchScalarGridSpec(
            num_scalar_prefetch=1,
            grid=(n_tiles,),
            in_specs=[pl.BlockSpec((TILE_BLOCKS, N_EXPERTS, LANES), lambda t, o: (t, 0, 0)),
                      pl.BlockSpec((GATHER_TOKENS, D), lambda t, o: (tile0 + t, 0)),
                      pl.BlockSpec((GATHER_TOKENS, LANES), lambda t, o: (t, 0))],
            out_specs=pl.BlockSpec(memory_space=pl.ANY),
            scratch_shapes=[pltpu.VMEM((2, N_EXPERTS, WINDOW, ROW_W), BF16),
                            pltpu.VMEM((N_EXPERTS, SLOT_ALIGN, ROW_W), BF16),
                            pltpu.SemaphoreType.DMA((2,)),
                            pltpu.SMEM((1,), I32)]),
        out_shape=jax.ShapeDtypeStruct((N_EXPERTS, cap + WINDOW, ROW_W), BF16),
        compiler_params=_params(("arbitrary",), VMEM_LIMIT),
    )(offs, slot, h_all, gate_terms)


FF_CHUNK = 512


def _experts_kernel(xp_ref, xs_ref, wg_ref, wu_ref, wd_ref, yp_ref, ys_ref, accp_ref, accs_ref, *, n_f):
    f = pl.program_id(1)
    wg = wg_ref[0].astype(BF16)
    wu = wu_ref[0].astype(BF16)
    wd = wd_ref[0].astype(BF16)

    def ffn(x):
        a = jnp.dot(x, wg, preferred_element_type=F32)
        b = jnp.dot(x, wu, preferred_element_type=F32)
        return jnp.dot((_silu(a) * b).astype(BF16), wd, preferred_element_type=F32)

    @pl.when(f == 0)
    def _():
        accp_ref[...] = ffn(xp_ref[0, :, :D])
        accs_ref[...] = ffn(xs_ref[0, :, :D])

    @pl.when(f > 0)
    def _():
        accp_ref[...] += ffn(xp_ref[0, :, :D])
        accs_ref[...] += ffn(xs_ref[0, :, :D])

    @pl.when(f == n_f - 1)
    def _():
        first = GATE_TERMS * pl.program_id(0)
        for x_ref, y_ref, acc_ref in ((xp_ref, yp_ref, accp_ref), (xs_ref, ys_ref, accs_ref)):
            cap = acc_ref.shape[0]
            lane = lax.broadcasted_iota(I32, (cap, LANES), 1)
            mine = (lane >= first) & (lane < first + GATE_TERMS)
            gate = jnp.sum(jnp.where(mine, x_ref[0, :, D:].astype(F32), 0.0), axis=1, keepdims=True)
            y_ref[0, :cap, :] = (acc_ref[...] * gate).astype(BF16)
            y_ref[0, cap:, :] = jnp.zeros((WINDOW, D), BF16)


def _experts(xe_p, xe_s, w_gate, w_up, w_down, layer):
    n_f = EXPERT_FF // FF_CHUNK
    cap_p, cap_s = xe_p.shape[1] - WINDOW, xe_s.shape[1] - WINDOW
    spec = lambda rows, width: pl.BlockSpec((1, rows, width), lambda e, f: (e, 0, 0))
    out = lambda cap: jax.ShapeDtypeStruct((N_EXPERTS, cap + WINDOW, D), BF16)
    return pl.pallas_call(
        functools.partial(_experts_kernel, n_f=n_f),
        name="experts",
        grid=(N_EXPERTS, n_f),
        in_specs=[spec(cap_p, ROW_W), spec(cap_s, ROW_W),
                  pl.BlockSpec((None, 1, D, FF_CHUNK), lambda e, f: (layer, e, 0, f)),
                  pl.BlockSpec((None, 1, D, FF_CHUNK), lambda e, f: (layer, e, 0, f)),
                  pl.BlockSpec((None, 1, FF_CHUNK, D), lambda e, f: (layer, e, f, 0))],
        out_specs=[spec(cap_p + WINDOW, D), spec(cap_s + WINDOW, D)],
        out_shape=[out(cap_p), out(cap_s)],
        scratch_shapes=[pltpu.VMEM((cap_p, D), F32), pltpu.VMEM((cap_s, D), F32)],
        compiler_params=_params(("arbitrary", "arbitrary"), VMEM_LIMIT),
    )(xe_p, xe_s, w_gate, w_up, w_down)


def _combine_kernel(offs_ref, slot_ref, x_ref, m_ref, ln_ref, ye_hbm, o_ref, stage_ref, sem, *, n_tiles):
    t = pl.program_id(0)
    buf = t & 1

    def in_copy(b, e, start):
        return pltpu.make_async_copy(ye_hbm.at[e, pl.ds(start, WINDOW)],
                                     stage_ref.at[b, pl.ds(e * WINDOW, WINDOW)], sem.at[b])

    def start_round(b, tile, r):
        for e in range(N_EXPERTS):
            in_copy(b, e, _round_bounds(offs_ref, e, tile, r)[2]).start()

    def wait_round(b):
        for e in range(N_EXPERTS):
            in_copy(b, e, 0).wait()

    @pl.when(t == 0)
    def _():
        start_round(0, 0, 0)

    wait_round(buf)

    @pl.when(t + 1 < n_tiles)
    def _():
        start_round(1 - buf, t + 1, 0)

    slot = slot_ref[...]
    lane = lax.broadcasted_iota(I32, (GATHER_TOKENS, LANES), 1)
    first = lane < WINDOW
    row_in_window = (lane % WINDOW).astype(F32)

    def token_rows(b, r):
        onehots = []
        for e in range(0, N_EXPERTS, 2):
            (lo0, hi0, st0), (lo1, hi1, st1) = (_round_bounds(offs_ref, e + i, t, r) for i in range(2))
            s = jnp.where(first, slot[:, e:e + 1], slot[:, e + 1:e + 2])
            lo = jnp.where(first, lo0, lo1).astype(F32)
            hi = jnp.where(first, hi0, hi1).astype(F32)
            start = jnp.where(first, st0, st1).astype(F32)
            hit = (s - start == row_in_window) & (s >= lo) & (s < hi)
            onehots.append(jnp.where(hit, 1.0, 0.0).astype(BF16))
        return jnp.dot(jnp.concatenate(onehots, axis=1), stage_ref[b], preferred_element_type=F32)

    o_ref[...] = token_rows(buf, 0)

    def extra_round(r, carry):
        start_round(buf, t, r)
        wait_round(buf)
        o_ref[...] += token_rows(buf, r)
        return carry

    lax.fori_loop(1, _n_rounds(offs_ref, t), extra_round, 0)
    o_ref[...] = x_ref[...] + m_ref[0, 5:6, :] * (_rms(o_ref[...]) * ln_ref[...])


def _combine(offs, ye, slot_cols, x_all, mod_l, ln, *, row0, n):
    n_tiles = n // GATHER_TOKENS
    tile0 = row0 // GATHER_TOKENS
    return pl.pallas_call(
        functools.partial(_combine_kernel, n_tiles=n_tiles),
        name="combine",
        grid_spec=pltpu.PrefetchScalarGridSpec(
            num_scalar_prefetch=1,
            grid=(n_tiles,),
            in_specs=[pl.BlockSpec((GATHER_TOKENS, N_EXPERTS), lambda t, o: (t, 0)),
                      pl.BlockSpec((GATHER_TOKENS, D), lambda t, o: (tile0 + t, 0)),
                      pl.BlockSpec((1, N_MOD, D), lambda t, o: (_mod_row(row0 + t * GATHER_TOKENS), 0, 0)),
                      pl.BlockSpec((1, D), lambda t, o: (0, 0)),
                      pl.BlockSpec(memory_space=pl.ANY)],
            out_specs=pl.BlockSpec((GATHER_TOKENS, D), lambda t, o: (t, 0)),
            scratch_shapes=[pltpu.VMEM((2, N_EXPERTS * WINDOW, D), BF16),
                            pltpu.SemaphoreType.DMA((2,))]),
        out_shape=jax.ShapeDtypeStruct((n, D), F32),
        compiler_params=_params(("arbitrary",), VMEM_LIMIT),
    )(offs, slot_cols, x_all, mod_l, ln.reshape(1, D), ye)


def _token_major(blocks):
    return blocks.transpose(0, 2, 1).reshape(-1, N_EXPERTS)


def kernel(x_prompt, x_sample, cache_k, cache_v, state_ret, c, c_ctx, w_mod, b_mod, ln_pre_mix, ln_post_mix,
           ln_pre_ffn, ln_post_ffn, w_in, ret_decay, pool_w, pool_scale, na_rpb, w_ret_o, w_pool_o, w_na_o, w_o,
           w_router, w_gate, w_up, w_down):
    cvecs = jnp.zeros((8, D), F32).at[0].set(c_ctx).at[1:1 + DEC_BATCH].set(c)
    mod = _modulation(cvecs, w_mod, b_mod)
    rope = _rope_tables()
    na_bias = _na_bias(na_rpb)
    x_pair = (x_prompt.reshape(NP, D), x_sample.reshape(NS, D))
    h_layers, new_s = [], []
    groups = ((0, NP, NP // N_EXPERTS * 2), (NP, NS, NS // N_EXPERTS * 2))

    for l in range(DEPTH):
        mod_l = mod[l]
        h_all = _prenorm(x_pair, mod_l, ln_pre_mix[l])
        h_layers.append(h_all)
        z = _in_proj(h_all, w_in, l)
        log_g = jax.nn.log_sigmoid(ret_decay[l].astype(F32))

        ret_p, st = _retention_ctx(z, log_g)
        ret_s = _retention(z, log_g, rope, state_ret, l)
        pool_p = _pool(z, pool_w[l], pool_scale[l], nb=BATCH, seq=SEQ, row_block0=0)
        pool_s = _pool(z, pool_w[l], pool_scale[l], nb=DEC_BATCH, seq=DEC_SEQ, row_block0=NP // DEC_SEQ)
        na_p = _ctx_attention(z)
        na_s = _neighbourhood_attention(z, na_bias, cache_k, cache_v, l)
        x_mid, h2, aff = _merge((ret_p, ret_s), (pool_p, pool_s), (na_p, na_s), x_pair, z, w_ret_o, w_pool_o, w_na_o,
                                w_o, mod_l, ln_post_mix[l], ln_pre_ffn[l], w_router[l].T, l)
        new_s.append(st)
        routed = []
        for row0, n, cap in groups:
            slot, offs = _route(aff, blk0=row0 // LANES, nblk=n // LANES, cap=cap)
            gate_terms = _gate_terms(_token_major(aff[row0 // LANES:(row0 + n) // LANES]))
            routed.append((slot, offs, _gather(offs, slot, h2, gate_terms, row0=row0, n=n, cap=cap)))
        ye = _experts(routed[0][2], routed[1][2], w_gate, w_up, w_down, l)
        outs = []
        for (row0, n, cap), (slot, offs, _), y in zip(groups, routed, ye):
            outs.append(_combine(offs, y, _token_major(slot), x_mid, mod_l, ln_post_ffn[l], row0=row0, n=n))
        x_pair = tuple(outs)

    y_prompt = x_pair[0].reshape(BATCH, SEQ, D)
    y_sample = x_pair[1].reshape(DEC_BATCH, DEC_SEQ, D)
    new_k, new_v = _kv_proj(h_layers, w_in)
    return (y_prompt, y_sample, new_k, new_v, jnp.stack(new_s, axis=1))
```

```python
import functools

import numpy as np
import jax
import jax.numpy as jnp
from jax import lax
from jax.experimental import pallas as pl
from jax.experimental.pallas import tpu as pltpu

F32 = jnp.float32
BF16 = jnp.bfloat16
I32 = jnp.int32

D = 1024
BATCH, SEQ = 32, 256
DEC_BATCH, DEC_SEQ = 2, 2048
DEPTH = 2
NP = BATCH * SEQ
NS = DEC_BATCH * DEC_SEQ
NT = NP + NS
GRID_W = 64
N_MOD = 6
EPS = 1e-6
ROPE_BASE = 10000.0
HEAD_DIM = 64
N_PAIRS = 4
CHUNK = 512
POOL_WINDOWS = (2, 4, 8, 16)
POOL_PAD = 16
NA_KH, NA_KW = 8, 16
N_EXPERTS = 16
EXPERT_FF = 2048
IN_COLS = 7168
LANES = 128
VMEM_LIMIT = 56 * 1024 * 1024

NT_DIMS = (((1,), (1,)), ((), ()))


def _params(sem, vmem=None):
    return pltpu.CompilerParams(dimension_semantics=sem, vmem_limit_bytes=vmem)


def _mod_row(row_start):
    return jnp.where(row_start < NP, 0, 1 + (row_start - NP) // DEC_SEQ)


def _silu(x):
    return x * jax.nn.sigmoid(x)


def _rms(x):
    return x * lax.rsqrt(jnp.mean(x * x, axis=-1, keepdims=True) + EPS)


def _pair_specs(tm, width):
    n_p = NP // tm
    return [pl.BlockSpec((tm, width), lambda i: (jnp.minimum(i, n_p - 1), 0)),
            pl.BlockSpec((tm, width), lambda i: (jnp.maximum(i - n_p, 0), 0))]


def _pick(p_ref, s_ref):
    return jnp.where(pl.program_id(0) < NP // p_ref.shape[0], p_ref[...], s_ref[...])


def _mod_kernel(c_ref, w_ref, b_ref, o_ref):
    a = _silu(c_ref[...]).astype(BF16)
    o_ref[0] = jnp.dot(a, w_ref[0].astype(BF16), preferred_element_type=F32) + b_ref[0]


def _modulation(cvecs, w_mod, b_mod):
    out = pl.pallas_call(
        _mod_kernel,
        name="modulation",
        grid=(DEPTH, N_MOD),
        in_specs=[pl.BlockSpec((8, D), lambda l, j: (0, 0)),
                  pl.BlockSpec((1, D, D), lambda l, j: (l, 0, j)),
                  pl.BlockSpec((1, 1, D), lambda l, j: (l, 0, j))],
        out_specs=pl.BlockSpec((1, 8, D), lambda l, j: (l, 0, j)),
        out_shape=jax.ShapeDtypeStruct((DEPTH, 8, N_MOD * D), F32),
        compiler_params=_params(("arbitrary", "arbitrary")),
    )(cvecs, w_mod, b_mod.reshape(DEPTH, 1, N_MOD * D))
    return out.reshape(DEPTH, 8, N_MOD, D)


def _prenorm_kernel(xp_ref, xs_ref, m_ref, ln_ref, h_ref):
    y = _rms(_pick(xp_ref, xs_ref)) * ln_ref[...]
    h_ref[...] = (y * (1.0 + m_ref[0, 1:2, :]) + m_ref[0, 0:1, :]).astype(BF16)


def _prenorm(x_pair, mod_l, ln):
    tm = 512
    return pl.pallas_call(
        _prenorm_kernel,
        name="prenorm",
        grid=(NT // tm,),
        in_specs=_pair_specs(tm, D)
                 + [pl.BlockSpec((1, N_MOD, D), lambda i: (_mod_row(i * tm), 0, 0)),
                  pl.BlockSpec((1, D), lambda i: (0, 0))],
        out_specs=pl.BlockSpec((tm, D), lambda i: (i, 0)),
        out_shape=jax.ShapeDtypeStruct((NT, D), BF16),
        compiler_params=_params(("arbitrary",)),
    )(*x_pair, mod_l, ln.reshape(1, D))


def _mm_kernel(a_ref, w_ref, o_ref, wb_ref):
    @pl.when(pl.program_id(1) == 0)
    def _():
        wb_ref[...] = w_ref[...].astype(BF16)

    o_ref[...] = jnp.dot(a_ref[...], wb_ref[...], preferred_element_type=F32)


def _kv_kernel(h0_ref, h1_ref, w_ref, k_ref, v_ref, wb_ref):
    @pl.when(pl.program_id(1) == 0)
    def _():
        wb_ref[...] = w_ref[0].astype(BF16)

    h = jnp.where(pl.program_id(0) == 0, h0_ref[...], h1_ref[...])
    kv = jnp.dot(h, wb_ref[...], preferred_element_type=F32)
    half = N_PAIRS * LANES
    k_ref[...] = kv[:, :half].reshape(k_ref.shape)
    v_ref[...] = kv[:, half:].reshape(v_ref.shape)


def _kv_proj(h_layers, w_in):
    per = 4
    kv_col = 3072 // D
    hspec = pl.BlockSpec((per * SEQ, D), lambda l, i: (i, 0))
    ospec = pl.BlockSpec((per, 1, SEQ, N_PAIRS * LANES), lambda l, i: (i, l, 0, 0))
    shape = jax.ShapeDtypeStruct((BATCH, DEPTH, SEQ, N_PAIRS * LANES), F32)
    k, v = pl.pallas_call(
        _kv_kernel,
        name="kv_proj",
        grid=(DEPTH, BATCH // per),
        in_specs=[hspec, hspec, pl.BlockSpec((1, D, D), lambda l, i: (l, 0, kv_col))],
        out_specs=[ospec, ospec],
        out_shape=[shape, shape],
        scratch_shapes=[pltpu.VMEM((D, D), BF16)],
        compiler_params=_params(("arbitrary", "arbitrary"), VMEM_LIMIT),
    )(*h_layers, w_in)
    cache_shape = (BATCH, DEPTH, SEQ, 2 * N_PAIRS, HEAD_DIM)
    return k.reshape(cache_shape), v.reshape(cache_shape)


def _in_proj(h_all, w_in, layer):
    tm, tn = 1024, 1792
    return pl.pallas_call(
        _mm_kernel,
        name="in_proj",
        grid=(IN_COLS // tn, NT // tm),
        in_specs=[pl.BlockSpec((tm, D), lambda j, i: (i, 0)),
                  pl.BlockSpec((None, D, tn), lambda j, i: (layer, 0, j))],
        out_specs=pl.BlockSpec((tm, tn), lambda j, i: (i, j)),
        out_shape=jax.ShapeDtypeStruct((NT, IN_COLS), F32),
        scratch_shapes=[pltpu.VMEM((D, tn), BF16)],
        compiler_params=_params(("arbitrary", "arbitrary"), VMEM_LIMIT),
    )(h_all, w_in)


def _swap16(x):
    lane = lax.broadcasted_iota(I32, x.shape, 1)
    return jnp.where((lane // 16) % 2 == 0, pltpu.roll(x, LANES - 16, 1), pltpu.roll(x, 16, 1))


def _block_diag(top, bottom):
    z = jnp.zeros((HEAD_DIM, HEAD_DIM), F32)
    return jnp.concatenate([jnp.concatenate([top, z], axis=1),
                            jnp.concatenate([z, bottom], axis=1)], axis=0)


RET_PAIRS_PER_STEP = 2


def _retention_kernel(lg_ref, q_ref, k_ref, v_ref, g_ref, cos_ref, sin_ref, s0_ref, o_ref, sf_scr, sb_scr):
    for pp in range(RET_PAIRS_PER_STEP):
        _retention_pair(lg_ref, q_ref, k_ref, v_ref, g_ref, cos_ref, sin_ref, s0_ref, o_ref, sf_scr, sb_scr,
                        pp, pl.program_id(1) * RET_PAIRS_PER_STEP + pp)


def _retention_pair(lg_ref, q_ref, k_ref, v_ref, g_ref, cos_ref, sin_ref, s0_ref, o_ref, sf_scr, sb_scr, pp, pair):
    n_chunks = DEC_SEQ // CHUNK
    lanes = slice(pp * LANES, (pp + 1) * LANES)
    lane1 = lax.broadcasted_iota(I32, (1, LANES), 1)
    lo1 = lane1 < HEAD_DIM
    lgf = jnp.where(lo1, lg_ref[0, 2 * pair], lg_ref[0, 2 * pair + 1])
    lgb = jnp.where(lo1, lg_ref[1, 2 * pair], lg_ref[1, 2 * pair + 1])
    lg_heads = [(lg_ref[0, 2 * pair], lg_ref[1, 2 * pair]),
                (lg_ref[0, 2 * pair + 1], lg_ref[1, 2 * pair + 1])]

    rel = (lax.broadcasted_iota(I32, (CHUNK, CHUNK), 0) - lax.broadcasted_iota(I32, (CHUNK, CHUNK), 1)).astype(F32)
    lo_mask = lax.broadcasted_iota(I32, (CHUNK, LANES), 1) < HEAD_DIM
    blockdiag = ((lax.broadcasted_iota(I32, (LANES, LANES), 0) < HEAD_DIM)
                 == (lax.broadcasted_iota(I32, (LANES, LANES), 1) < HEAD_DIM))
    posf = lax.broadcasted_iota(I32, (CHUNK, LANES), 0).astype(F32)
    dmat = []
    for hf, hb in lg_heads:
        dmat.append(jnp.where(rel >= 0, jnp.exp(jnp.where(rel >= 0, rel, 0.0) * hf), 0.0)
                    + jnp.where(rel <= 0, jnp.exp(jnp.where(rel <= 0, -rel, 0.0) * hb), 0.0))
    qdec_f = jnp.exp((posf + 1.0) * lgf)
    kdec_f = jnp.exp((CHUNK - 1.0 - posf) * lgf)
    qdec_b = jnp.exp((CHUNK - posf) * lgb)
    kdec_b = jnp.exp(posf * lgb)
    sdec_f = jnp.exp(CHUNK * lgf)
    sdec_b = jnp.exp(CHUNK * lgb)

    def load(c):
        rows = pl.ds(c * CHUNK, CHUNK)
        cs, sn = cos_ref[rows, :], sin_ref[rows, :]
        q = q_ref[rows, lanes]
        k = k_ref[rows, lanes]
        q = q * cs + _swap16(q) * sn
        k = k * cs + _swap16(k) * sn
        return q, k * (HEAD_DIM ** -0.5), v_ref[rows, lanes]

    def state_update(s, k, v, kdec, sdec):
        kd = (k * kdec).T.astype(BF16)
        u = jnp.dot(kd, v.astype(BF16), preferred_element_type=F32)
        return s * sdec + jnp.where(blockdiag, u, 0.0)

    h0, h1 = 2 * pp, 2 * pp + 1
    s_f = _block_diag(s0_ref[0, 0, 0, h0], s0_ref[0, 0, 0, h1])
    s_b = _block_diag(s0_ref[0, 0, 1, h0], s0_ref[0, 0, 1, h1])
    scr0 = pp * n_chunks
    for c in range(n_chunks):
        sf_scr[scr0 + c] = s_f
        _, k, v = load(c)
        s_f = state_update(s_f, k, v, kdec_f, sdec_f)
    for c in reversed(range(n_chunks)):
        sb_scr[scr0 + c] = s_b
        _, k, v = load(c)
        s_b = state_update(s_b, k, v, kdec_b, sdec_b)

    for c in range(n_chunks):
        q, k, v = load(c)
        qb, kb, vb = q.astype(BF16), k.astype(BF16), v.astype(BF16)
        outs = []
        for h in range(2):
            qh = jnp.where(lo_mask if h == 0 else ~lo_mask, qb, jnp.zeros_like(qb))
            a = lax.dot_general(qh, kb, NT_DIMS, preferred_element_type=F32) * dmat[h]
            outs.append(jnp.dot(a.astype(BF16), vb, preferred_element_type=F32))
        o = jnp.where(lo_mask, outs[0], outs[1])
        o = o + jnp.dot(qb, sf_scr[scr0 + c].astype(BF16), preferred_element_type=F32) * qdec_f
        o = o + jnp.dot(qb, sb_scr[scr0 + c].astype(BF16), preferred_element_type=F32) * qdec_b
        o2 = o * o
        ms0 = jnp.sum(jnp.where(lo_mask, o2, 0.0), axis=1, keepdims=True) * (1.0 / HEAD_DIM)
        ms1 = jnp.sum(jnp.where(lo_mask, 0.0, o2), axis=1, keepdims=True) * (1.0 / HEAD_DIM)
        inv = jnp.where(lo_mask, lax.rsqrt(ms0 + EPS), lax.rsqrt(ms1 + EPS))
        g = g_ref[pl.ds(c * CHUNK, CHUNK), lanes]
        o_ref[pl.ds(c * CHUNK, CHUNK), lanes] = (_silu(g) * (o * inv)).astype(BF16)


def _retention(z, log_g, rope, state_ret, layer):
    pps = RET_PAIRS_PER_STEP
    width = pps * LANES
    row_block0 = NP // DEC_SEQ
    scratch = pltpu.VMEM((pps * (DEC_SEQ // CHUNK), LANES, LANES), F32)

    def zspec(cb):
        return pl.BlockSpec((DEC_SEQ, width), lambda b, p: (row_block0 + b, cb // pps + p))

    table = pl.BlockSpec((DEC_SEQ, LANES), lambda b, p: (0, 0))
    return pl.pallas_call(
        _retention_kernel,
        name="retention",
        grid=(DEC_BATCH, N_PAIRS // pps),
        in_specs=[pl.BlockSpec(memory_space=pltpu.SMEM), zspec(0), zspec(4), zspec(8), zspec(12), table, table,
                  pl.BlockSpec((1, 1, 2, 2 * pps, HEAD_DIM, HEAD_DIM), lambda b, p: (b, layer, 0, p, 0, 0))],
        out_specs=pl.BlockSpec((DEC_SEQ, width), lambda b, p: (b, p)),
        out_shape=jax.ShapeDtypeStruct((NS, N_PAIRS * LANES), BF16),
        scratch_shapes=[scratch, scratch],
        compiler_params=_params(("arbitrary", "arbitrary"), VMEM_LIMIT),
    )(log_g, z, z, z, z, *rope, state_ret)


def _retention_ctx_kernel(lg_ref, q_ref, k_ref, v_ref, g_ref, o_ref, st_ref, decay_ref):
    heads = 2 * N_PAIRS

    @pl.when(pl.program_id(0) == 0)
    def _():
        rel = (lax.broadcasted_iota(I32, (SEQ, SEQ), 0) - lax.broadcasted_iota(I32, (SEQ, SEQ), 1)).astype(F32)
        for h in range(heads):
            decay_ref[h] = (jnp.where(rel >= 0, jnp.exp(jnp.where(rel >= 0, rel, 0.0) * lg_ref[0, h]), 0.0)
                            + jnp.where(rel <= 0, jnp.exp(jnp.where(rel <= 0, -rel, 0.0) * lg_ref[1, h]), 0.0))

    lane = lax.broadcasted_iota(I32, (SEQ, LANES), 1)
    lo_mask = lane < HEAD_DIM
    pos = lax.broadcasted_iota(I32, (SEQ, LANES), 0).astype(F32)
    blockdiag = ((lax.broadcasted_iota(I32, (LANES, LANES), 0) < HEAD_DIM)
                 == (lax.broadcasted_iota(I32, (LANES, LANES), 1) < HEAD_DIM))
    for pp in range(N_PAIRS):
        lanes = slice(pp * LANES, (pp + 1) * LANES)
        h0, h1 = 2 * pp, 2 * pp + 1
        k = k_ref[:, lanes] * (HEAD_DIM ** -0.5)
        qb, kb, vb = q_ref[:, lanes].astype(BF16), k.astype(BF16), v_ref[:, lanes].astype(BF16)
        outs = []
        for h, mask in ((h0, lo_mask), (h1, ~lo_mask)):
            qh = jnp.where(mask, qb, jnp.zeros_like(qb))
            a = lax.dot_general(qh, kb, NT_DIMS, preferred_element_type=F32) * decay_ref[h]
            outs.append(jnp.dot(a.astype(BF16), vb, preferred_element_type=F32))
        o = jnp.where(lo_mask, outs[0], outs[1])
        o2 = o * o
        ms0 = jnp.sum(jnp.where(lo_mask, o2, 0.0), axis=1, keepdims=True) * (1.0 / HEAD_DIM)
        ms1 = jnp.sum(jnp.where(lo_mask, 0.0, o2), axis=1, keepdims=True) * (1.0 / HEAD_DIM)
        inv = jnp.where(lo_mask, lax.rsqrt(ms0 + EPS), lax.rsqrt(ms1 + EPS))
        o_ref[:, lanes] = (_silu(g_ref[:, lanes]) * (o * inv)).astype(BF16)

        for d, age in ((0, SEQ - 1.0 - pos), (1, pos)):
            lg = jnp.where(lo_mask, lg_ref[d, h0], lg_ref[d, h1])
            kd = (k * jnp.exp(age * lg)).T.astype(BF16)
            s = jnp.where(blockdiag, jnp.dot(kd, vb, preferred_element_type=F32), 0.0)
            st_ref[0, d, h0] = s[:HEAD_DIM, :HEAD_DIM]
            st_ref[0, d, h1] = s[HEAD_DIM:, HEAD_DIM:]


def _retention_ctx(z, log_g):
    width = N_PAIRS * LANES
    heads = 2 * N_PAIRS
    zspec = lambda cb: pl.BlockSpec((SEQ, width), lambda b: (b, cb))
    return pl.pallas_call(
        _retention_ctx_kernel,
        name="retention_ctx",
        grid=(BATCH,),
        in_specs=[pl.BlockSpec(memory_space=pltpu.SMEM), zspec(0), zspec(1), zspec(2), zspec(3)],
        out_specs=[pl.BlockSpec((SEQ, width), lambda b: (b, 0)),
                   pl.BlockSpec((1, 2, heads, HEAD_DIM, HEAD_DIM), lambda b: (b, 0, 0, 0, 0))],
        out_shape=[jax.ShapeDtypeStruct((NP, width), BF16),
                   jax.ShapeDtypeStruct((BATCH, 2, heads, HEAD_DIM, HEAD_DIM), F32)],
        scratch_shapes=[pltpu.VMEM((heads, SEQ, SEQ), F32)],
        compiler_params=_params(("arbitrary",)),
    )(log_g, z, z, z, z)


def _rope_tables():
    t = np.arange(DEC_SEQ)
    posn = [(t // GRID_W).astype(np.float32), (t % GRID_W).astype(np.float32)]
    nf = HEAD_DIM // 4
    freqs = (1.0 / (np.float32(ROPE_BASE) ** (np.arange(nf, dtype=np.float32) / np.float32(nf)))).astype(np.float32)
    cos = np.zeros((DEC_SEQ, HEAD_DIM), np.float32)
    sin = np.zeros((DEC_SEQ, HEAD_DIM), np.float32)
    for half in range(2):
        ang = (posn[half][:, None] * freqs[None, :]).astype(np.float32)
        for grp in range(2):
            lo = half * 32 + grp * nf
            cos[:, lo:lo + nf] = np.cos(ang)
            sin[:, lo:lo + nf] = np.sin(ang) * (-1.0 if grp == 0 else 1.0)
    return jnp.asarray(np.tile(cos, (1, 2))), jnp.asarray(np.tile(sin, (1, 2)))


def _pool_kernel(u_ref, w_ref, sc_ref, o_ref, *, seq):
    padded = seq + 2 * POOL_PAD
    t = lax.broadcasted_iota(I32, (seq, 1), 0)
    zpad = jnp.zeros((POOL_PAD, LANES), F32)
    for s in range(u_ref.shape[0] // seq):
        rows = slice(s * seq, (s + 1) * seq)
        for gi, w in enumerate(POOL_WINDOWS):
            cols = slice(gi * LANES, (gi + 1) * LANES)
            x = u_ref[rows, cols]
            run = jnp.concatenate([zpad, x, zpad], axis=0)
            span = 1
            while span < w:
                run = run + pltpu.roll(run, padded - span, 0)
                span *= 2
            win = pltpu.roll(run, padded - (POOL_PAD - w // 2), 0)[:seq]
            cnt = (jnp.minimum(t + w // 2, seq) - jnp.maximum(t - w // 2, 0)).astype(F32)
            pooled = win / cnt - x
            mixed = jnp.dot(pooled.astype(BF16), w_ref[gi].astype(BF16), preferred_element_type=F32)
            o_ref[rows, cols] = (mixed * sc_ref[:, cols]).astype(BF16)


def _pool(z, pool_w, pool_scale, *, nb, seq, per_step, row_block0):
    width = len(POOL_WINDOWS) * LANES
    rows = per_step * seq
    return pl.pallas_call(
        functools.partial(_pool_kernel, seq=seq),
        name="pool",
        grid=(nb // per_step,),
        in_specs=[pl.BlockSpec((rows, width), lambda b: (row_block0 + b, 2048 // width)),
                  pl.BlockSpec((len(POOL_WINDOWS), LANES, LANES), lambda b: (0, 0, 0)),
                  pl.BlockSpec((1, width), lambda b: (0, 0))],
        out_specs=pl.BlockSpec((rows, width), lambda b: (b, 0)),
        out_shape=jax.ShapeDtypeStruct((nb * seq, width), BF16),
        compiler_params=_params(("arbitrary",), VMEM_LIMIT),
    )(z, pool_w, pool_scale.reshape(1, width))


def _head_select(h, shape):
    lane = lax.broadcasted_iota(I32, shape, 1)
    return (lane < HEAD_DIM) if h == 0 else (lane >= HEAD_DIM)


def _ctx_attn_kernel(q_ref, k_ref, v_ref, o_ref):
    for pp in range(N_PAIRS):
        lanes = slice(pp * LANES, (pp + 1) * LANES)
        qb = (q_ref[:, lanes] * (HEAD_DIM ** -0.5)).astype(BF16)
        kb = k_ref[:, lanes].astype(BF16)
        vb = v_ref[:, lanes].astype(BF16)
        outs = []
        for h in range(2):
            qh = jnp.where(_head_select(h, qb.shape), qb, jnp.zeros_like(qb))
            s = lax.dot_general(qh, kb, NT_DIMS, preferred_element_type=F32)
            p = jnp.exp(s - jnp.max(s, axis=1, keepdims=True))
            denom = jnp.sum(p, axis=1, keepdims=True)
            outs.append(jnp.dot(p.astype(BF16), vb, preferred_element_type=F32) / denom)
        o_ref[:, lanes] = jnp.where(_head_select(0, outs[0].shape), outs[0], outs[1]).astype(BF16)


def _ctx_attention(z):
    width = N_PAIRS * LANES

    def zspec(cb):
        return pl.BlockSpec((SEQ, width), lambda b: (b, cb))

    return pl.pallas_call(
        _ctx_attn_kernel,
        name="ctx_attn",
        grid=(BATCH,),
        in_specs=[zspec(5), zspec(6), zspec(7)],
        out_specs=pl.BlockSpec((SEQ, width), lambda b: (b, 0)),
        out_shape=jax.ShapeDtypeStruct((NP, width), BF16),
        compiler_params=_params(("arbitrary",)),
    )(z, z, z)


NA_QROWS = 4
NA_QBLK = NA_QROWS * GRID_W
NA_KROWS = 12
NA_NBLK = DEC_SEQ // NA_QBLK


def _na_key_block(i):
    return jnp.clip(i - 1, 0, NA_NBLK - 3)


NA_PATTERNS = ((0, 0), (NA_QROWS, 0), (DEC_SEQ // GRID_W - NA_QROWS, DEC_SEQ // GRID_W - NA_KROWS))
NA_DX_LANE = GRID_W - (NA_KW - 1)


def _na_bias_kernel(rpb_ref, o_ref):
    rows = DEC_SEQ // GRID_W
    q = lax.broadcasted_iota(I32, (GRID_W, LANES), 0)
    lane = lax.broadcasted_iota(I32, (GRID_W, LANES), 1)
    c = lane % GRID_W
    c_start = jnp.clip(q - NA_KW // 2, 0, GRID_W - NA_KW)
    col_ok = (c >= c_start) & (c < c_start + NA_KW)
    lower = lane < GRID_W
    for p, (r0, ks) in enumerate(NA_PATTERNS):
        for rr in range(NA_QROWS):
            r = r0 + rr
            start = min(max(r - NA_KH // 2, 0), rows - NA_KH)
            for kp in range(NA_KROWS // 2):
                halves = []
                for half in range(2):
                    kr = ks + 2 * kp + half
                    if start <= kr < start + NA_KH:
                        row = jnp.broadcast_to(rpb_ref[0, 0, pl.ds(kr - r + NA_KH - 1, 1), :], (GRID_W, LANES))
                        halves.append(pltpu.roll(row, GRID_W * (1 - half), 1, stride=1, stride_axis=0))
                    else:
                        halves.append(None)
                neg = jnp.full((GRID_W, LANES), -jnp.inf, F32)
                lo_half = neg if halves[0] is None else jnp.where(col_ok, halves[0], neg)
                hi_half = neg if halves[1] is None else jnp.where(col_ok, halves[1], neg)
                o_ref[0, p, 0, rr * GRID_W:(rr + 1) * GRID_W, kp * LANES:(kp + 1) * LANES] = (
                    jnp.where(lower, lo_half, hi_half))


def _na_bias(na_rpb):
    ny, nx = 2 * NA_KH - 1, 2 * NA_KW - 1
    padded = jnp.pad(na_rpb.astype(F32), ((0, 0), (0, 0), (0, 16 - ny), (NA_DX_LANE, LANES - NA_DX_LANE - nx)))
    heads = 2 * N_PAIRS
    return pl.pallas_call(
        _na_bias_kernel,
        name="nbr_bias",
        grid=(DEPTH, heads),
        in_specs=[pl.BlockSpec((1, 1, 16, LANES), lambda l, h: (l, h, 0, 0))],
        out_specs=pl.BlockSpec((1, len(NA_PATTERNS), 1, NA_QBLK, NA_KROWS * GRID_W), lambda l, h: (l, 0, h, 0, 0)),
        out_shape=jax.ShapeDtypeStruct((DEPTH, len(NA_PATTERNS), heads, NA_QBLK, NA_KROWS * GRID_W), F32),
        compiler_params=_params(("arbitrary", "arbitrary")),
    )(padded)


def _na_kernel(q_ref, k0_ref, k1_ref, k2_ref, v0_ref, v1_ref, v2_ref, ck_ref, cv_ref, bias_ref, o_ref):
    for pp in range(N_PAIRS):
        lanes = slice(pp * LANES, (pp + 1) * LANES)
        qb = (q_ref[:, lanes] * (HEAD_DIM ** -0.5)).astype(BF16)
        ks = [r[:, lanes].astype(BF16) for r in (k0_ref, k1_ref, k2_ref)] + [ck_ref[0, 0, :, lanes].astype(BF16)]
        vs = [r[:, lanes].astype(BF16) for r in (v0_ref, v1_ref, v2_ref)] + [cv_ref[0, 0, :, lanes].astype(BF16)]
        outs = []
        for h in range(2):
            qh = jnp.where(_head_select(h, qb.shape), qb, jnp.zeros_like(qb))
            ss = []
            for j in range(4):
                s = lax.dot_general(qh, ks[j], NT_DIMS, preferred_element_type=F32)
                if j < 3:
                    s = s + bias_ref[0, 2 * pp + h, :, j * NA_QBLK:(j + 1) * NA_QBLK]
                ss.append(s)
            m = functools.reduce(jnp.maximum, [jnp.max(s, axis=1, keepdims=True) for s in ss])
            ps = [jnp.exp(s - m) for s in ss]
            denom = functools.reduce(jnp.add, [jnp.sum(p, axis=1, keepdims=True) for p in ps])
            acc = functools.reduce(jnp.add, [jnp.dot(p.astype(BF16), v, preferred_element_type=F32)
                                             for p, v in zip(ps, vs)])
            outs.append(acc / denom)
        o_ref[:, lanes] = jnp.where(_head_select(0, outs[0].shape), outs[0], outs[1]).astype(BF16)


def _neighbourhood_attention(z, bias, cache_k, cache_v, layer):
    base = NP // NA_QBLK
    width = N_PAIRS * LANES
    heads = 2 * N_PAIRS

    def kvspec(cb, j):
        return pl.BlockSpec((NA_QBLK, width), lambda b, i: (base + b * NA_NBLK + _na_key_block(i) + j, cb))

    cspec = pl.BlockSpec((1, 1, SEQ, width), lambda b, i: (b, layer, 0, 0))
    pattern = lambda i: jnp.where(i == 0, 0, jnp.where(i == NA_NBLK - 1, 2, 1))
    ck = cache_k.reshape(DEC_BATCH, DEPTH, SEQ, width)
    cv = cache_v.reshape(DEC_BATCH, DEPTH, SEQ, width)
    return pl.pallas_call(
        _na_kernel,
        name="nbr_attn",
        grid=(DEC_BATCH, NA_NBLK),
        in_specs=[pl.BlockSpec((NA_QBLK, width), lambda b, i: (base + b * NA_NBLK + i, 5))]
                 + [kvspec(6, j) for j in range(3)] + [kvspec(7, j) for j in range(3)]
                 + [cspec, cspec,
                    pl.BlockSpec((None, 1, heads, NA_QBLK, NA_KROWS * GRID_W),
                                 lambda b, i: (layer, pattern(i), 0, 0, 0))],
        out_specs=pl.BlockSpec((NA_QBLK, width), lambda b, i: (b * NA_NBLK + i, 0)),
        out_shape=jax.ShapeDtypeStruct((NS, width), BF16),
        compiler_params=_params(("arbitrary", "arbitrary"), VMEM_LIMIT),
    )(z, z, z, z, z, z, z, ck, cv, bias)


def _merge_kernel(rp_ref, rs_ref, pp_ref, ps_ref, ap_ref, as_ref, xp_ref, xs_ref, g0_ref, g1_ref, g2_ref,
                  wr_ref, wp_ref, wa_ref, wo_ref, m_ref, ln_ref, ln_ffn_ref, wrt_ref,
                  o_ref, h_ref, aff_ref, wrb, wpb, wab, wob):
    @pl.when(pl.program_id(0) == 0)
    def _():
        wrb[...] = wr_ref[...].astype(BF16)
        wpb[...] = wp_ref[...].astype(BF16)
        wab[...] = wa_ref[...].astype(BF16)
        wob[...] = wo_ref[...].astype(BF16)

    branch = lambda p_ref, s_ref, w: jnp.dot(_pick(p_ref, s_ref), w[...], preferred_element_type=F32)
    merged = (jax.nn.sigmoid(g0_ref[...]) * branch(rp_ref, rs_ref, wrb)
              + jax.nn.sigmoid(g1_ref[...]) * branch(pp_ref, ps_ref, wpb)
              + jax.nn.sigmoid(g2_ref[...]) * branch(ap_ref, as_ref, wab))
    mix = jnp.dot(merged.astype(BF16), wob[...], preferred_element_type=F32)
    x = _pick(xp_ref, xs_ref) + m_ref[0, 2:3, :] * (_rms(mix) * ln_ref[...])
    o_ref[...] = x

    h = (_rms(x) * ln_ffn_ref[...]) * (1.0 + m_ref[0, 4:5, :]) + m_ref[0, 3:4, :]
    hb = h.astype(BF16)
    h_ref[...] = hb
    hl = (h - hb.astype(F32)).astype(BF16)
    w = wrt_ref[...]
    wb = w.astype(BF16)
    wl = (w - wb.astype(F32)).astype(BF16)
    logits = (lax.dot_general(wb, hb, NT_DIMS, preferred_element_type=F32)
              + lax.dot_general(wb, hl, NT_DIMS, preferred_element_type=F32)
              + lax.dot_general(wl, hb, NT_DIMS, preferred_element_type=F32))
    e = jnp.exp(logits - jnp.max(logits, axis=0, keepdims=True))
    aff = e / jnp.sum(e, axis=0, keepdims=True)
    for j in range(aff_ref.shape[0]):
        aff_ref[j] = aff[:, j * LANES:(j + 1) * LANES]


def _merge(ret_pair, pool_pair, na_pair, x_pair, z, w_ret_o, w_pool_o, w_na_o, w_o, mod_l, ln, ln_ffn, w_router_t,
           layer):
    tm = 512
    half = N_PAIRS * LANES
    row = lambda i: (i, 0)
    const = lambda i: (0, 0)
    slab = lambda i: (layer, 0, 0)
    return pl.pallas_call(
        _merge_kernel,
        name="merge",
        grid=(NT // tm,),
        in_specs=_pair_specs(tm, half) * 3 + _pair_specs(tm, D)
                 + [pl.BlockSpec((tm, D), lambda i, c=c: (i, 4 + c)) for c in range(3)]
                 + [pl.BlockSpec((None, half, D), slab)] * 3
                 + [pl.BlockSpec((None, D, D), slab),
                    pl.BlockSpec((1, N_MOD, D), lambda i: (_mod_row(i * tm), 0, 0)),
                    pl.BlockSpec((1, D), const),
                    pl.BlockSpec((1, D), const),
                    pl.BlockSpec((N_EXPERTS, D), const)],
        out_specs=[pl.BlockSpec((tm, D), row),
                   pl.BlockSpec((tm, D), row),
                   pl.BlockSpec((tm // LANES, N_EXPERTS, LANES), lambda i: (i, 0, 0))],
        out_shape=[jax.ShapeDtypeStruct((NT, D), F32),
                   jax.ShapeDtypeStruct((NT, D), BF16),
                   jax.ShapeDtypeStruct((NT // LANES, N_EXPERTS, LANES), F32)],
        scratch_shapes=[pltpu.VMEM((half, D), BF16)] * 3 + [pltpu.VMEM((D, D), BF16)],
        compiler_params=_params(("arbitrary",), VMEM_LIMIT),
    )(*ret_pair, *pool_pair, *na_pair, *x_pair, z, z, z, w_ret_o, w_pool_o, w_na_o, w_o, mod_l, ln.reshape(1, D),
      ln_ffn.reshape(1, D), w_router_t)


def _route_kernel(aff_ref, slot_ref, offs_ref, *, cap, nblk):
    as_float = lambda bits: lax.bitcast_convert_type(bits, F32)

    def count(pred):
        return jnp.sum(jnp.sum(jnp.where(pred, 1.0, 0.0), axis=0), axis=1, keepdims=True)

    def search(_, lohi):
        lo, hi = lohi
        mid = lo + ((hi - lo + 1) >> 1)
        ok = count(aff_ref[...] >= as_float(mid)[None]) >= cap
        return jnp.where(ok, mid, lo), jnp.where(ok, hi, mid - 1)

    lo0 = jnp.zeros((N_EXPERTS, 1), I32)
    hi0 = jnp.full((N_EXPERTS, 1), 0x7F800000, I32)
    thr_bits, _ = lax.fori_loop(0, 31, search, (lo0, hi0))
    thr = as_float(thr_bits)
    need = cap - count(aff_ref[...] > thr[None])

    upper = (lax.broadcasted_iota(I32, (LANES, LANES), 0)
             < lax.broadcasted_iota(I32, (LANES, LANES), 1)).astype(BF16)

    def running_count(flags):
        inside = jnp.dot(flags.reshape(nblk * N_EXPERTS, LANES).astype(BF16), upper,
                         preferred_element_type=F32).reshape(nblk, N_EXPERTS, LANES)
        totals = jnp.sum(flags, axis=2, keepdims=True)
        before, run = [], jnp.zeros((N_EXPERTS, 1), F32)
        for b in range(nblk):
            before.append(run)
            run = run + totals[b]
        return inside + jnp.stack(before), before, run

    aff = aff_ref[...]
    tied = aff == thr[None]
    tied_rank, _, _ = running_count(jnp.where(tied, 1.0, 0.0))
    chosen = (aff > thr[None]) | (tied & (tied_rank < need[None]))
    rank, before, total = running_count(jnp.where(chosen, 1.0, 0.0))
    slot_ref[...] = jnp.where(chosen, rank, -1.0)

    lane = lax.broadcasted_iota(I32, (N_EXPERTS, LANES), 1)
    offs = jnp.where(lane >= nblk, total, 0.0)
    for b in range(nblk):
        offs = jnp.where(lane == b, before[b], offs)
    offs_ref[...] = offs.astype(I32)


def _route(aff_blocks, *, blk0, nblk, cap):
    return pl.pallas_call(
        functools.partial(_route_kernel, cap=cap, nblk=nblk),
        name="route",
        grid=(1,),
        in_specs=[pl.BlockSpec((nblk, N_EXPERTS, LANES), lambda i: (blk0 // nblk, 0, 0))],
        out_specs=[pl.BlockSpec((nblk, N_EXPERTS, LANES), lambda i: (0, 0, 0)),
                   pl.BlockSpec((N_EXPERTS, LANES), lambda i: (0, 0))],
        out_shape=[jax.ShapeDtypeStruct((nblk, N_EXPERTS, LANES), F32),
                   jax.ShapeDtypeStruct((N_EXPERTS, LANES), I32)],
        compiler_params=_params(("arbitrary",)),
    )(aff_blocks)


GATHER_TOKENS = 256
TILE_BLOCKS = GATHER_TOKENS // LANES
SLOT_ALIGN = 16
WINDOW = LANES // 2
ROUND_SLOTS = WINDOW - SLOT_ALIGN
GATE_TERMS = 3
ROW_W = D + LANES


def _round_bounds(offs_ref, e, t, r):
    off0 = offs_ref[e, TILE_BLOCKS * t]
    off1 = offs_ref[e, TILE_BLOCKS * t + TILE_BLOCKS]
    lo = jnp.minimum(off0 + ROUND_SLOTS * r, off1)
    hi = jnp.minimum(lo + ROUND_SLOTS, off1)
    return lo, hi, pl.multiple_of(lo & -SLOT_ALIGN, SLOT_ALIGN)


def _n_rounds(offs_ref, t):
    most = jnp.int32(0)
    for e in range(N_EXPERTS):
        most = jnp.maximum(most, offs_ref[e, TILE_BLOCKS * t + TILE_BLOCKS] - offs_ref[e, TILE_BLOCKS * t])
    rounds = jnp.int32(0)
    for filled in range(0, GATHER_TOKENS, ROUND_SLOTS):
        rounds = rounds + (most > filled).astype(I32)
    return rounds


def _gate_terms(aff_cols):
    hi = aff_cols.astype(BF16)
    rest = aff_cols - hi.astype(F32)
    mid = rest.astype(BF16)
    lo = (rest - mid.astype(F32)).astype(BF16)
    terms = jnp.stack([hi, mid, lo], axis=-1).reshape(aff_cols.shape[0], GATE_TERMS * N_EXPERTS)
    return jnp.pad(terms, ((0, 0), (0, LANES - GATE_TERMS * N_EXPERTS)))


def _gather_kernel(offs_ref, slot_ref, h_ref, g_ref, xe_hbm, stage_ref, carry_ref, sem, nround_ref, *, n_tiles):
    t = pl.program_id(0)
    cap = xe_hbm.shape[1] - WINDOW

    def out_copy(buf, e, start):
        return pltpu.make_async_copy(stage_ref.at[buf, e], xe_hbm.at[e, pl.ds(start, WINDOW)], sem.at[buf])

    def wait_round(buf):
        for e in range(N_EXPERTS):
            out_copy(buf, e, 0).wait()

    @pl.when(t == 0)
    def _():
        carry_ref[...] = jnp.zeros_like(carry_ref)
        nround_ref[0] = 0
        stage_ref[0, 0] = jnp.zeros((WINDOW, ROW_W), BF16)
        for e in range(N_EXPERTS):
            pltpu.make_async_copy(stage_ref.at[0, 0], xe_hbm.at[e, pl.ds(cap, WINDOW)], sem.at[0]).start()
        wait_round(0)

    hb = jnp.concatenate([h_ref[...], g_ref[...]], axis=1)
    sub = lax.broadcasted_iota(I32, (WINDOW, GATHER_TOKENS), 0).astype(F32)

    def one_round(r, carry):
        done = nround_ref[0]
        buf = done & 1
        bounds = [_round_bounds(offs_ref, e, t, r) for e in range(N_EXPERTS)]
        onehots = []
        for e in range(N_EXPERTS):
            lo, hi, start = bounds[e]
            srow = jnp.concatenate([slot_ref[j, e:e + 1, :] for j in range(TILE_BLOCKS)], axis=1)
            hit = ((srow - start.astype(F32) == sub) & (srow >= lo.astype(F32)) & (srow < hi.astype(F32)))
            onehots.append(jnp.where(hit, 1.0, 0.0).astype(BF16))
        rows = jnp.dot(jnp.concatenate(onehots, axis=0), hb, preferred_element_type=F32)
        for e in range(N_EXPERTS):
            lo, hi, start = bounds[e]
            piece = rows[e * WINDOW:(e + 1) * WINDOW]
            head = piece[:SLOT_ALIGN] + carry_ref[e].astype(F32)
            stage_ref[buf, e, :SLOT_ALIGN, :] = head.astype(BF16)
            stage_ref[buf, e, SLOT_ALIGN:, :] = piece[SLOT_ALIGN:].astype(BF16)
            tail = pl.multiple_of((hi & -SLOT_ALIGN) - start, SLOT_ALIGN)
            carry_ref[e] = stage_ref[buf, e, pl.ds(tail, SLOT_ALIGN), :]

        @pl.when(done > 0)
        def _():
            wait_round(1 - buf)

        for e in range(N_EXPERTS):
            out_copy(buf, e, bounds[e][2]).start()
        nround_ref[0] = done + 1
        return carry

    lax.fori_loop(0, _n_rounds(offs_ref, t), one_round, 0)

    @pl.when((t == n_tiles - 1) & (nround_ref[0] > 0))
    def _():
        wait_round((nround_ref[0] - 1) & 1)


def _gather(offs, slot, h_all, gate_terms, *, row0, n, cap):
    n_tiles = n // GATHER_TOKENS
    tile0 = row0 // GATHER_TOKENS
    return pl.pallas_call(
        functools.partial(_gather_kernel, n_tiles=n_tiles),
        name="gather",
        grid_spec=pltpu.PrefetchScalarGridSpec(
            num_scalar_prefetch=1,
            grid=(n_tiles,),
            in_specs=[pl.BlockSpec((TILE_BLOCKS, N_EXPERTS, LANES), lambda t, o: (t, 0, 0)),
                      pl.BlockSpec((GATHER_TOKENS, D), lambda t, o: (tile0 + t, 0)),
                      pl.BlockSpec((GATHER_TOKENS, LANES), lambda t, o: (t, 0))],
            out_specs=pl.BlockSpec(memory_space=pl.ANY),
            scratch_shapes=[pltpu.VMEM((2, N_EXPERTS, WINDOW, ROW_W), BF16),
                            pltpu.VMEM((N_EXPERTS, SLOT_ALIGN, ROW_W), BF16),
                            pltpu.SemaphoreType.DMA((2,)),
                            pltpu.SMEM((1,), I32)]),
        out_shape=jax.ShapeDtypeStruct((N_EXPERTS, cap + WINDOW, ROW_W), BF16),
        compiler_params=_params(("arbitrary",), VMEM_LIMIT),
    )(offs, slot, h_all, gate_terms)


FF_CHUNK = 512


def _experts_kernel(xp_ref, xs_ref, wg_ref, wu_ref, wd_ref, yp_ref, ys_ref, accp_ref, accs_ref, *, n_f):
    f = pl.program_id(1)
    wg = wg_ref[0].astype(BF16)
    wu = wu_ref[0].astype(BF16)
    wd = wd_ref[0].astype(BF16)

    def ffn(x):
        a = jnp.dot(x, wg, preferred_element_type=F32)
        b = jnp.dot(x, wu, preferred_element_type=F32)
        return jnp.dot((_silu(a) * b).astype(BF16), wd, preferred_element_type=F32)

    @pl.when(f == 0)
    def _():
        accp_ref[...] = ffn(xp_ref[0, :, :D])
        accs_ref[...] = ffn(xs_ref[0, :, :D])

    @pl.when(f > 0)
    def _():
        accp_ref[...] += ffn(xp_ref[0, :, :D])
        accs_ref[...] += ffn(xs_ref[0, :, :D])

    @pl.when(f == n_f - 1)
    def _():
        first = GATE_TERMS * pl.program_id(0)
        for x_ref, y_ref, acc_ref in ((xp_ref, yp_ref, accp_ref), (xs_ref, ys_ref, accs_ref)):
            cap = acc_ref.shape[0]
            lane = lax.broadcasted_iota(I32, (cap, LANES), 1)
            mine = (lane >= first) & (lane < first + GATE_TERMS)
            gate = jnp.sum(jnp.where(mine, x_ref[0, :, D:].astype(F32), 0.0), axis=1, keepdims=True)
            y_ref[0, :cap, :] = (acc_ref[...] * gate).astype(BF16)
            y_ref[0, cap:, :] = jnp.zeros((WINDOW, D), BF16)


def _experts(xe_p, xe_s, w_gate, w_up, w_down, layer):
    n_f = EXPERT_FF // FF_CHUNK
    cap_p, cap_s = xe_p.shape[1] - WINDOW, xe_s.shape[1] - WINDOW
    spec = lambda rows, width: pl.BlockSpec((1, rows, width), lambda e, f: (e, 0, 0))
    out = lambda cap: jax.ShapeDtypeStruct((N_EXPERTS, cap + WINDOW, D), BF16)
    return pl.pallas_call(
        functools.partial(_experts_kernel, n_f=n_f),
        name="experts",
        grid=(N_EXPERTS, n_f),
        in_specs=[spec(cap_p, ROW_W), spec(cap_s, ROW_W),
                  pl.BlockSpec((None, 1, D, FF_CHUNK), lambda e, f: (layer, e, 0, f)),
                  pl.BlockSpec((None, 1, D, FF_CHUNK), lambda e, f: (layer, e, 0, f)),
                  pl.BlockSpec((None, 1, FF_CHUNK, D), lambda e, f: (layer, e, f, 0))],
        out_specs=[spec(cap_p + WINDOW, D), spec(cap_s + WINDOW, D)],
        out_shape=[out(cap_p), out(cap_s)],
        scratch_shapes=[pltpu.VMEM((cap_p, D), F32), pltpu.VMEM((cap_s, D), F32)],
        compiler_params=_params(("arbitrary", "arbitrary"), VMEM_LIMIT),
    )(xe_p, xe_s, w_gate, w_up, w_down)


def _combine_kernel(offs_ref, slot_ref, x_ref, m_ref, ln_ref, ye_hbm, o_ref, stage_ref, sem, *, n_tiles):
    t = pl.program_id(0)
    buf = t & 1

    def in_copy(b, e, start):
        return pltpu.make_async_copy(ye_hbm.at[e, pl.ds(start, WINDOW)],
                                     stage_ref.at[b, pl.ds(e * WINDOW, WINDOW)], sem.at[b])

    def start_round(b, tile, r):
        for e in range(N_EXPERTS):
            in_copy(b, e, _round_bounds(offs_ref, e, tile, r)[2]).start()

    def wait_round(b):
        for e in range(N_EXPERTS):
            in_copy(b, e, 0).wait()

    @pl.when(t == 0)
    def _():
        start_round(0, 0, 0)

    wait_round(buf)

    @pl.when(t + 1 < n_tiles)
    def _():
        start_round(1 - buf, t + 1, 0)

    slot = slot_ref[...]
    lane = lax.broadcasted_iota(I32, (GATHER_TOKENS, LANES), 1)
    first = lane < WINDOW
    row_in_window = (lane % WINDOW).astype(F32)

    def token_rows(b, r):
        onehots = []
        for e in range(0, N_EXPERTS, 2):
            (lo0, hi0, st0), (lo1, hi1, st1) = (_round_bounds(offs_ref, e + i, t, r) for i in range(2))
            s = jnp.where(first, slot[:, e:e + 1], slot[:, e + 1:e + 2])
            lo = jnp.where(first, lo0, lo1).astype(F32)
            hi = jnp.where(first, hi0, hi1).astype(F32)
            start = jnp.where(first, st0, st1).astype(F32)
            hit = (s - start == row_in_window) & (s >= lo) & (s < hi)
            onehots.append(jnp.where(hit, 1.0, 0.0).astype(BF16))
        return jnp.dot(jnp.concatenate(onehots, axis=1), stage_ref[b], preferred_element_type=F32)

    o_ref[...] = token_rows(buf, 0)

    def extra_round(r, carry):
        start_round(buf, t, r)
        wait_round(buf)
        o_ref[...] += token_rows(buf, r)
        return carry

    lax.fori_loop(1, _n_rounds(offs_ref, t), extra_round, 0)
    o_ref[...] = x_ref[...] + m_ref[0, 5:6, :] * (_rms(o_ref[...]) * ln_ref[...])


def _combine(offs, ye, slot_cols, x_all, mod_l, ln, *, row0, n):
    n_tiles = n // GATHER_TOKENS
    tile0 = row0 // GATHER_TOKENS
    return pl.pallas_call(
        functools.partial(_combine_kernel, n_tiles=n_tiles),
        name="combine",
        grid_spec=pltpu.PrefetchScalarGridSpec(
            num_scalar_prefetch=1,
            grid=(n_tiles,),
            in_specs=[pl.BlockSpec((GATHER_TOKENS, N_EXPERTS), lambda t, o: (t, 0)),
                      pl.BlockSpec((GATHER_TOKENS, D), lambda t, o: (tile0 + t, 0)),
                      pl.BlockSpec((1, N_MOD, D), lambda t, o: (_mod_row(row0 + t * GATHER_TOKENS), 0, 0)),
                      pl.BlockSpec((1, D), lambda t, o: (0, 0)),
                      pl.BlockSpec(memory_space=pl.ANY)],
            out_specs=pl.BlockSpec((GATHER_TOKENS, D), lambda t, o: (t, 0)),
            scratch_shapes=[pltpu.VMEM((2, N_EXPERTS * WINDOW, D), BF16),
                            pltpu.SemaphoreType.DMA((2,))]),
        out_shape=jax.ShapeDtypeStruct((n, D), F32),
        compiler_params=_params(("arbitrary",), VMEM_LIMIT),
    )(offs, slot_cols, x_all, mod_l, ln.reshape(1, D), ye)


def _token_major(blocks):
    return blocks.transpose(0, 2, 1).reshape(-1, N_EXPERTS)


def kernel(x_prompt, x_sample, cache_k, cache_v, state_ret, c, c_ctx, w_mod, b_mod, ln_pre_mix, ln_post_mix,
           ln_pre_ffn, ln_post_ffn, w_in, ret_decay, pool_w, pool_scale, na_rpb, w_ret_o, w_pool_o, w_na_o, w_o,
           w_router, w_gate, w_up, w_down):
    cvecs = jnp.zeros((8, D), F32).at[0].set(c_ctx).at[1:1 + DEC_BATCH].set(c)
    mod = _modulation(cvecs, w_mod, b_mod)
    rope = _rope_tables()
    na_bias = _na_bias(na_rpb)
    x_pair = (x_prompt.reshape(NP, D), x_sample.reshape(NS, D))
    h_layers, new_s = [], []
    groups = ((0, NP, NP // N_EXPERTS * 2), (NP, NS, NS // N_EXPERTS * 2))

    for l in range(DEPTH):
        mod_l = mod[l]
        h_all = _prenorm(x_pair, mod_l, ln_pre_mix[l])
        h_layers.append(h_all)
        z = _in_proj(h_all, w_in, l)
        log_g = jax.nn.log_sigmoid(ret_decay[l].astype(F32))

        ret_p, st = _retention_ctx(z, log_g)
        ret_s = _retention(z, log_g, rope, state_ret, l)
        pool_p = _pool(z, pool_w[l], pool_scale[l], nb=BATCH, seq=SEQ, per_step=4, row_block0=0)
        pool_s = _pool(z, pool_w[l], pool_scale[l], nb=DEC_BATCH, seq=DEC_SEQ, per_step=1,
                       row_block0=NP // DEC_SEQ)
        na_p = _ctx_attention(z)
        na_s = _neighbourhood_attention(z, na_bias, cache_k, cache_v, l)
        x_mid, h2, aff = _merge((ret_p, ret_s), (pool_p, pool_s), (na_p, na_s), x_pair, z, w_ret_o, w_pool_o, w_na_o,
                                w_o, mod_l, ln_post_mix[l], ln_pre_ffn[l], w_router[l].T, l)
        new_s.append(st)
        routed = []
        for row0, n, cap in groups:
            slot, offs = _route(aff, blk0=row0 // LANES, nblk=n // LANES, cap=cap)
            gate_terms = _gate_terms(_token_major(aff[row0 // LANES:(row0 + n) // LANES]))
            routed.append((slot, offs, _gather(offs, slot, h2, gate_terms, row0=row0, n=n, cap=cap)))
        ye = _experts(routed[0][2], routed[1][2], w_gate, w_up, w_down, l)
        outs = []
        for (row0, n, cap), (slot, offs, _), y in zip(groups, routed, ye):
            outs.append(_combine(offs, y, _token_major(slot), x_mid, mod_l, ln_post_ffn[l], row0=row0, n=n))
        x_pair = tuple(outs)

    y_prompt = x_pair[0].reshape(BATCH, SEQ, D)
    y_sample = x_pair[1].reshape(DEC_BATCH, DEC_SEQ, D)
    new_k, new_v = _kv_proj(h_layers, w_in)
    return (y_prompt, y_sample, new_k, new_v, jnp.stack(new_s, axis=1))
```

```python
import functools

import numpy as np
import jax
import jax.numpy as jnp
from jax import lax
from jax.experimental import pallas as pl
from jax.experimental.pallas import tpu as pltpu

F32 = jnp.float32
BF16 = jnp.bfloat16
I32 = jnp.int32

D = 1024
BATCH, SEQ = 32, 256
DEC_BATCH, DEC_SEQ = 2, 2048
DEPTH = 2
NP = BATCH * SEQ
NS = DEC_BATCH * DEC_SEQ
NT = NP + NS
GRID_W = 64
N_MOD = 6
EPS = 1e-6
ROPE_BASE = 10000.0
HEAD_DIM = 64
N_PAIRS = 4
CHUNK = 512
POOL_WINDOWS = (2, 4, 8, 16)
POOL_PAD = 16
NA_KH, NA_KW = 8, 16
N_EXPERTS = 16
EXPERT_FF = 2048
IN_COLS = 7168
LANES = 128
VMEM_LIMIT = 56 * 1024 * 1024

NT_DIMS = (((1,), (1,)), ((), ()))


def _params(sem, vmem=None):
    return pltpu.CompilerParams(dimension_semantics=sem, vmem_limit_bytes=vmem)


def _mod_row(row_start):
    return jnp.where(row_start < NP, 0, 1 + (row_start - NP) // DEC_SEQ)


def _sigmoid(x):
    return 0.5 * jnp.tanh(0.5 * x) + 0.5


def _silu(x):
    return x * _sigmoid(x)


def _rms(x):
    return x * lax.rsqrt(jnp.mean(x * x, axis=-1, keepdims=True) + EPS)


def _pair_specs(tm, width):
    n_p = NP // tm
    return [pl.BlockSpec((tm, width), lambda i: (jnp.minimum(i, n_p - 1), 0)),
            pl.BlockSpec((tm, width), lambda i: (jnp.maximum(i - n_p, 0), 0))]


def _pick(p_ref, s_ref):
    return jnp.where(pl.program_id(0) < NP // p_ref.shape[0], p_ref[...], s_ref[...])


def _mod_kernel(c_ref, w_ref, b_ref, o_ref):
    a = _silu(c_ref[...]).astype(BF16)
    o_ref[0] = jnp.dot(a, w_ref[0].astype(BF16), preferred_element_type=F32) + b_ref[0]


def _modulation(cvecs, w_mod, b_mod):
    out = pl.pallas_call(
        _mod_kernel,
        name="modulation",
        grid=(DEPTH, N_MOD),
        in_specs=[pl.BlockSpec((8, D), lambda l, j: (0, 0)),
                  pl.BlockSpec((1, D, D), lambda l, j: (l, 0, j)),
                  pl.BlockSpec((1, 1, D), lambda l, j: (l, 0, j))],
        out_specs=pl.BlockSpec((1, 8, D), lambda l, j: (l, 0, j)),
        out_shape=jax.ShapeDtypeStruct((DEPTH, 8, N_MOD * D), F32),
        compiler_params=_params(("arbitrary", "arbitrary")),
    )(cvecs, w_mod, b_mod.reshape(DEPTH, 1, N_MOD * D))
    return out.reshape(DEPTH, 8, N_MOD, D)


def _prenorm_kernel(xp_ref, xs_ref, m_ref, ln_ref, h_ref):
    y = _rms(_pick(xp_ref, xs_ref)) * ln_ref[...]
    h_ref[...] = (y * (1.0 + m_ref[0, 1:2, :]) + m_ref[0, 0:1, :]).astype(BF16)


def _prenorm(x_pair, mod_l, ln):
    tm = 1024
    return pl.pallas_call(
        _prenorm_kernel,
        name="prenorm",
        grid=(NT // tm,),
        in_specs=_pair_specs(tm, D)
                 + [pl.BlockSpec((1, N_MOD, D), lambda i: (_mod_row(i * tm), 0, 0)),
                  pl.BlockSpec((1, D), lambda i: (0, 0))],
        out_specs=pl.BlockSpec((tm, D), lambda i: (i, 0)),
        out_shape=jax.ShapeDtypeStruct((NT, D), BF16),
        compiler_params=_params(("arbitrary",)),
    )(*x_pair, mod_l, ln.reshape(1, D))


def _mm_kernel(a_ref, w_ref, o_ref, wb_ref):
    @pl.when(pl.program_id(1) == 0)
    def _():
        wb_ref[...] = w_ref[...].astype(BF16)

    o_ref[...] = jnp.dot(a_ref[...], wb_ref[...], preferred_element_type=F32)


def _kv_kernel(h0_ref, h1_ref, w_ref, k_ref, v_ref, wb_ref):
    @pl.when(pl.program_id(1) == 0)
    def _():
        wb_ref[...] = w_ref[0].astype(BF16)

    h = jnp.where(pl.program_id(0) == 0, h0_ref[...], h1_ref[...])
    kv = jnp.dot(h, wb_ref[...], preferred_element_type=F32)
    half = N_PAIRS * LANES
    k_ref[...] = kv[:, :half].reshape(k_ref.shape)
    v_ref[...] = kv[:, half:].reshape(v_ref.shape)


def _kv_proj(h_layers, w_in):
    per = 4
    kv_col = 3072 // D
    hspec = pl.BlockSpec((per * SEQ, D), lambda l, i: (i, 0))
    ospec = pl.BlockSpec((per, 1, SEQ, N_PAIRS * LANES), lambda l, i: (i, l, 0, 0))
    shape = jax.ShapeDtypeStruct((BATCH, DEPTH, SEQ, N_PAIRS * LANES), F32)
    k, v = pl.pallas_call(
        _kv_kernel,
        name="kv_proj",
        grid=(DEPTH, BATCH // per),
        in_specs=[hspec, hspec, pl.BlockSpec((1, D, D), lambda l, i: (l, 0, kv_col))],
        out_specs=[ospec, ospec],
        out_shape=[shape, shape],
        scratch_shapes=[pltpu.VMEM((D, D), BF16)],
        compiler_params=_params(("arbitrary", "arbitrary"), VMEM_LIMIT),
    )(*h_layers, w_in)
    cache_shape = (BATCH, DEPTH, SEQ, 2 * N_PAIRS, HEAD_DIM)
    return k.reshape(cache_shape), v.reshape(cache_shape)


def _in_proj(h_all, w_in, layer):
    tm, tn = 1024, 1792
    return pl.pallas_call(
        _mm_kernel,
        name="in_proj",
        grid=(IN_COLS // tn, NT // tm),
        in_specs=[pl.BlockSpec((tm, D), lambda j, i: (i, 0)),
                  pl.BlockSpec((None, D, tn), lambda j, i: (layer, 0, j))],
        out_specs=pl.BlockSpec((tm, tn), lambda j, i: (i, j)),
        out_shape=jax.ShapeDtypeStruct((NT, IN_COLS), F32),
        scratch_shapes=[pltpu.VMEM((D, tn), BF16)],
        compiler_params=_params(("arbitrary", "arbitrary"), VMEM_LIMIT),
    )(h_all, w_in)


def _swap16(x):
    lane = lax.broadcasted_iota(I32, x.shape, 1)
    return jnp.where((lane // 16) % 2 == 0, pltpu.roll(x, LANES - 16, 1), pltpu.roll(x, 16, 1))


def _block_diag(top, bottom):
    z = jnp.zeros((HEAD_DIM, HEAD_DIM), F32)
    return jnp.concatenate([jnp.concatenate([top, z], axis=1),
                            jnp.concatenate([z, bottom], axis=1)], axis=0)


RET_PAIRS_PER_STEP = 2


def _retention_kernel(lg_ref, q_ref, k_ref, v_ref, g_ref, cos_ref, sin_ref, s0_ref, o_ref, sf_scr, sb_scr):
    for pp in range(RET_PAIRS_PER_STEP):
        _retention_pair(lg_ref, q_ref, k_ref, v_ref, g_ref, cos_ref, sin_ref, s0_ref, o_ref, sf_scr, sb_scr,
                        pp, pl.program_id(1) * RET_PAIRS_PER_STEP + pp)


def _retention_pair(lg_ref, q_ref, k_ref, v_ref, g_ref, cos_ref, sin_ref, s0_ref, o_ref, sf_scr, sb_scr, pp, pair):
    n_chunks = DEC_SEQ // CHUNK
    lanes = slice(pp * LANES, (pp + 1) * LANES)
    lane1 = lax.broadcasted_iota(I32, (1, LANES), 1)
    lo1 = lane1 < HEAD_DIM
    lgf = jnp.where(lo1, lg_ref[0, 2 * pair], lg_ref[0, 2 * pair + 1])
    lgb = jnp.where(lo1, lg_ref[1, 2 * pair], lg_ref[1, 2 * pair + 1])
    lg_heads = [(lg_ref[0, 2 * pair], lg_ref[1, 2 * pair]),
                (lg_ref[0, 2 * pair + 1], lg_ref[1, 2 * pair + 1])]

    rel = (lax.broadcasted_iota(I32, (CHUNK, CHUNK), 0) - lax.broadcasted_iota(I32, (CHUNK, CHUNK), 1)).astype(F32)
    lo_mask = lax.broadcasted_iota(I32, (CHUNK, LANES), 1) < HEAD_DIM
    blockdiag = ((lax.broadcasted_iota(I32, (LANES, LANES), 0) < HEAD_DIM)
                 == (lax.broadcasted_iota(I32, (LANES, LANES), 1) < HEAD_DIM))
    posf = lax.broadcasted_iota(I32, (CHUNK, LANES), 0).astype(F32)
    dmat = []
    for hf, hb in lg_heads:
        dmat.append(jnp.where(rel >= 0, jnp.exp(jnp.where(rel >= 0, rel, 0.0) * hf), 0.0)
                    + jnp.where(rel <= 0, jnp.exp(jnp.where(rel <= 0, -rel, 0.0) * hb), 0.0))
    qdec_f = jnp.exp((posf + 1.0) * lgf)
    kdec_f = jnp.exp((CHUNK - 1.0 - posf) * lgf)
    qdec_b = jnp.exp((CHUNK - posf) * lgb)
    kdec_b = jnp.exp(posf * lgb)
    sdec_f = jnp.exp(CHUNK * lgf)
    sdec_b = jnp.exp(CHUNK * lgb)

    def load(c):
        rows = pl.ds(c * CHUNK, CHUNK)
        cs, sn = cos_ref[rows, :], sin_ref[rows, :]
        q = q_ref[rows, lanes]
        k = k_ref[rows, lanes]
        q = q * cs + _swap16(q) * sn
        k = k * cs + _swap16(k) * sn
        return q, k * (HEAD_DIM ** -0.5), v_ref[rows, lanes]

    def state_update(s, k, v, kdec, sdec):
        kd = (k * kdec).T.astype(BF16)
        u = jnp.dot(kd, v.astype(BF16), preferred_element_type=F32)
        return s * sdec + jnp.where(blockdiag, u, 0.0)

    h0, h1 = 2 * pp, 2 * pp + 1
    s_f = _block_diag(s0_ref[0, 0, 0, h0], s0_ref[0, 0, 0, h1])
    s_b = _block_diag(s0_ref[0, 0, 1, h0], s0_ref[0, 0, 1, h1])
    scr0 = pp * n_chunks
    for c in range(n_chunks):
        sf_scr[scr0 + c] = s_f
        _, k, v = load(c)
        s_f = state_update(s_f, k, v, kdec_f, sdec_f)
    for c in reversed(range(n_chunks)):
        sb_scr[scr0 + c] = s_b
        _, k, v = load(c)
        s_b = state_update(s_b, k, v, kdec_b, sdec_b)

    for c in range(n_chunks):
        q, k, v = load(c)
        qb, kb, vb = q.astype(BF16), k.astype(BF16), v.astype(BF16)
        outs = []
        for h in range(2):
            qh = jnp.where(lo_mask if h == 0 else ~lo_mask, qb, jnp.zeros_like(qb))
            a = lax.dot_general(qh, kb, NT_DIMS, preferred_element_type=F32) * dmat[h]
            outs.append(jnp.dot(a.astype(BF16), vb, preferred_element_type=F32))
        o = jnp.where(lo_mask, outs[0], outs[1])
        o = o + jnp.dot(qb, sf_scr[scr0 + c].astype(BF16), preferred_element_type=F32) * qdec_f
        o = o + jnp.dot(qb, sb_scr[scr0 + c].astype(BF16), preferred_element_type=F32) * qdec_b
        o2 = o * o
        ms0 = jnp.sum(jnp.where(lo_mask, o2, 0.0), axis=1, keepdims=True) * (1.0 / HEAD_DIM)
        ms1 = jnp.sum(jnp.where(lo_mask, 0.0, o2), axis=1, keepdims=True) * (1.0 / HEAD_DIM)
        inv = jnp.where(lo_mask, lax.rsqrt(ms0 + EPS), lax.rsqrt(ms1 + EPS))
        g = g_ref[pl.ds(c * CHUNK, CHUNK), lanes]
        o_ref[pl.ds(c * CHUNK, CHUNK), lanes] = (_silu(g) * (o * inv)).astype(BF16)


def _retention(z, log_g, rope, state_ret, layer):
    pps = RET_PAIRS_PER_STEP
    width = pps * LANES
    row_block0 = NP // DEC_SEQ
    scratch = pltpu.VMEM((pps * (DEC_SEQ // CHUNK), LANES, LANES), F32)

    def zspec(cb):
        return pl.BlockSpec((DEC_SEQ, width), lambda b, p: (row_block0 + b, cb // pps + p))

    table = pl.BlockSpec((DEC_SEQ, LANES), lambda b, p: (0, 0))
    return pl.pallas_call(
        _retention_kernel,
        name="retention",
        grid=(DEC_BATCH, N_PAIRS // pps),
        in_specs=[pl.BlockSpec(memory_space=pltpu.SMEM), zspec(0), zspec(4), zspec(8), zspec(12), table, table,
                  pl.BlockSpec((1, 1, 2, 2 * pps, HEAD_DIM, HEAD_DIM), lambda b, p: (b, layer, 0, p, 0, 0))],
        out_specs=pl.BlockSpec((DEC_SEQ, width), lambda b, p: (b, p)),
        out_shape=jax.ShapeDtypeStruct((NS, N_PAIRS * LANES), BF16),
        scratch_shapes=[scratch, scratch],
        compiler_params=_params(("arbitrary", "arbitrary"), VMEM_LIMIT),
    )(log_g, z, z, z, z, *rope, state_ret)


def _retention_ctx_kernel(lg_ref, q_ref, k_ref, v_ref, g_ref, o_ref, st_ref, decay_ref):
    heads = 2 * N_PAIRS

    @pl.when(pl.program_id(0) == 0)
    def _():
        rel = (lax.broadcasted_iota(I32, (SEQ, SEQ), 0) - lax.broadcasted_iota(I32, (SEQ, SEQ), 1)).astype(F32)
        for h in range(heads):
            decay_ref[h] = (jnp.where(rel >= 0, jnp.exp(jnp.where(rel >= 0, rel, 0.0) * lg_ref[0, h]), 0.0)
                            + jnp.where(rel <= 0, jnp.exp(jnp.where(rel <= 0, -rel, 0.0) * lg_ref[1, h]), 0.0))

    lane = lax.broadcasted_iota(I32, (SEQ, LANES), 1)
    lo_mask = lane < HEAD_DIM
    pos = lax.broadcasted_iota(I32, (SEQ, LANES), 0).astype(F32)
    blockdiag = ((lax.broadcasted_iota(I32, (LANES, LANES), 0) < HEAD_DIM)
                 == (lax.broadcasted_iota(I32, (LANES, LANES), 1) < HEAD_DIM))
    for pp in range(N_PAIRS):
        lanes = slice(pp * LANES, (pp + 1) * LANES)
        h0, h1 = 2 * pp, 2 * pp + 1
        k = k_ref[:, lanes] * (HEAD_DIM ** -0.5)
        qb, kb, vb = q_ref[:, lanes].astype(BF16), k.astype(BF16), v_ref[:, lanes].astype(BF16)
        outs = []
        for h, mask in ((h0, lo_mask), (h1, ~lo_mask)):
            qh = jnp.where(mask, qb, jnp.zeros_like(qb))
            a = lax.dot_general(qh, kb, NT_DIMS, preferred_element_type=F32) * decay_ref[h]
            outs.append(jnp.dot(a.astype(BF16), vb, preferred_element_type=F32))
        o = jnp.where(lo_mask, outs[0], outs[1])
        o2 = o * o
        ms0 = jnp.sum(jnp.where(lo_mask, o2, 0.0), axis=1, keepdims=True) * (1.0 / HEAD_DIM)
        ms1 = jnp.sum(jnp.where(lo_mask, 0.0, o2), axis=1, keepdims=True) * (1.0 / HEAD_DIM)
        inv = jnp.where(lo_mask, lax.rsqrt(ms0 + EPS), lax.rsqrt(ms1 + EPS))
        o_ref[:, lanes] = (_silu(g_ref[:, lanes]) * (o * inv)).astype(BF16)

        for d, age in ((0, SEQ - 1.0 - pos), (1, pos)):
            lg = jnp.where(lo_mask, lg_ref[d, h0], lg_ref[d, h1])
            kd = (k * jnp.exp(age * lg)).T.astype(BF16)
            s = jnp.where(blockdiag, jnp.dot(kd, vb, preferred_element_type=F32), 0.0)
            st_ref[0, d, h0] = s[:HEAD_DIM, :HEAD_DIM]
            st_ref[0, d, h1] = s[HEAD_DIM:, HEAD_DIM:]


def _retention_ctx(z, log_g):
    width = N_PAIRS * LANES
    heads = 2 * N_PAIRS
    zspec = lambda cb: pl.BlockSpec((SEQ, width), lambda b: (b, cb))
    return pl.pallas_call(
        _retention_ctx_kernel,
        name="retention_ctx",
        grid=(BATCH,),
        in_specs=[pl.BlockSpec(memory_space=pltpu.SMEM), zspec(0), zspec(1), zspec(2), zspec(3)],
        out_specs=[pl.BlockSpec((SEQ, width), lambda b: (b, 0)),
                   pl.BlockSpec((1, 2, heads, HEAD_DIM, HEAD_DIM), lambda b: (b, 0, 0, 0, 0))],
        out_shape=[jax.ShapeDtypeStruct((NP, width), BF16),
                   jax.ShapeDtypeStruct((BATCH, 2, heads, HEAD_DIM, HEAD_DIM), F32)],
        scratch_shapes=[pltpu.VMEM((heads, SEQ, SEQ), F32)],
        compiler_params=_params(("arbitrary",)),
    )(log_g, z, z, z, z)


def _rope_tables():
    t = np.arange(DEC_SEQ)
    posn = [(t // GRID_W).astype(np.float32), (t % GRID_W).astype(np.float32)]
    nf = HEAD_DIM // 4
    freqs = (1.0 / (np.float32(ROPE_BASE) ** (np.arange(nf, dtype=np.float32) / np.float32(nf)))).astype(np.float32)
    cos = np.zeros((DEC_SEQ, HEAD_DIM), np.float32)
    sin = np.zeros((DEC_SEQ, HEAD_DIM), np.float32)
    for half in range(2):
        ang = (posn[half][:, None] * freqs[None, :]).astype(np.float32)
        for grp in range(2):
            lo = half * 32 + grp * nf
            cos[:, lo:lo + nf] = np.cos(ang)
            sin[:, lo:lo + nf] = np.sin(ang) * (-1.0 if grp == 0 else 1.0)
    return jnp.asarray(np.tile(cos, (1, 2))), jnp.asarray(np.tile(sin, (1, 2)))


def _pool_kernel(u_ref, w_ref, sc_ref, o_ref, *, seq):
    padded = seq + 2 * POOL_PAD
    t = lax.broadcasted_iota(I32, (seq, 1), 0)
    zpad = jnp.zeros((POOL_PAD, LANES), F32)
    for s in range(u_ref.shape[0] // seq):
        rows = slice(s * seq, (s + 1) * seq)
        for gi, w in enumerate(POOL_WINDOWS):
            cols = slice(gi * LANES, (gi + 1) * LANES)
            x = u_ref[rows, cols]
            run = jnp.concatenate([zpad, x, zpad], axis=0)
            span = 1
            while span < w:
                run = run + pltpu.roll(run, padded - span, 0)
                span *= 2
            win = pltpu.roll(run, padded - (POOL_PAD - w // 2), 0)[:seq]
            cnt = (jnp.minimum(t + w // 2, seq) - jnp.maximum(t - w // 2, 0)).astype(F32)
            pooled = win / cnt - x
            mixed = jnp.dot(pooled.astype(BF16), w_ref[gi].astype(BF16), preferred_element_type=F32)
            o_ref[rows, cols] = (mixed * sc_ref[:, cols]).astype(BF16)


def _pool(z, pool_w, pool_scale, *, nb, seq, per_step, row_block0):
    width = len(POOL_WINDOWS) * LANES
    rows = per_step * seq
    return pl.pallas_call(
        functools.partial(_pool_kernel, seq=seq),
        name="pool",
        grid=(nb // per_step,),
        in_specs=[pl.BlockSpec((rows, width), lambda b: (row_block0 + b, 2048 // width)),
                  pl.BlockSpec((len(POOL_WINDOWS), LANES, LANES), lambda b: (0, 0, 0)),
                  pl.BlockSpec((1, width), lambda b: (0, 0))],
        out_specs=pl.BlockSpec((rows, width), lambda b: (b, 0)),
        out_shape=jax.ShapeDtypeStruct((nb * seq, width), BF16),
        compiler_params=_params(("arbitrary",), VMEM_LIMIT),
    )(z, pool_w, pool_scale.reshape(1, width))


def _head_select(h, shape):
    lane = lax.broadcasted_iota(I32, shape, 1)
    return (lane < HEAD_DIM) if h == 0 else (lane >= HEAD_DIM)


def _ctx_attn_kernel(q_ref, k_ref, v_ref, o_ref):
    for pp in range(N_PAIRS):
        lanes = slice(pp * LANES, (pp + 1) * LANES)
        qb = (q_ref[:, lanes] * (HEAD_DIM ** -0.5)).astype(BF16)
        kb = k_ref[:, lanes].astype(BF16)
        vb = v_ref[:, lanes].astype(BF16)
        outs = []
        for h in range(2):
            qh = jnp.where(_head_select(h, qb.shape), qb, jnp.zeros_like(qb))
            s = lax.dot_general(qh, kb, NT_DIMS, preferred_element_type=F32)
            p = jnp.exp(s - jnp.max(s, axis=1, keepdims=True))
            denom = jnp.sum(p, axis=1, keepdims=True)
            outs.append(jnp.dot(p.astype(BF16), vb, preferred_element_type=F32) / denom)
        o_ref[:, lanes] = jnp.where(_head_select(0, outs[0].shape), outs[0], outs[1]).astype(BF16)


def _ctx_attention(z):
    width = N_PAIRS * LANES

    def zspec(cb):
        return pl.BlockSpec((SEQ, width), lambda b: (b, cb))

    return pl.pallas_call(
        _ctx_attn_kernel,
        name="ctx_attn",
        grid=(BATCH,),
        in_specs=[zspec(5), zspec(6), zspec(7)],
        out_specs=pl.BlockSpec((SEQ, width), lambda b: (b, 0)),
        out_shape=jax.ShapeDtypeStruct((NP, width), BF16),
        compiler_params=_params(("arbitrary",)),
    )(z, z, z)


NA_QROWS = 4
NA_QBLK = NA_QROWS * GRID_W
NA_KROWS = 12
NA_NBLK = DEC_SEQ // NA_QBLK


def _na_key_block(i):
    return jnp.clip(i - 1, 0, NA_NBLK - 3)


NA_PATTERNS = ((0, 0), (NA_QROWS, 0), (DEC_SEQ // GRID_W - NA_QROWS, DEC_SEQ // GRID_W - NA_KROWS))
NA_DX_LANE = GRID_W - (NA_KW - 1)


def _na_bias_kernel(rpb_ref, o_ref):
    rows = DEC_SEQ // GRID_W
    q = lax.broadcasted_iota(I32, (GRID_W, LANES), 0)
    lane = lax.broadcasted_iota(I32, (GRID_W, LANES), 1)
    c = lane % GRID_W
    c_start = jnp.clip(q - NA_KW // 2, 0, GRID_W - NA_KW)
    col_ok = (c >= c_start) & (c < c_start + NA_KW)
    lower = lane < GRID_W
    for p, (r0, ks) in enumerate(NA_PATTERNS):
        for rr in range(NA_QROWS):
            r = r0 + rr
            start = min(max(r - NA_KH // 2, 0), rows - NA_KH)
            for kp in range(NA_KROWS // 2):
                halves = []
                for half in range(2):
                    kr = ks + 2 * kp + half
                    if start <= kr < start + NA_KH:
                        row = jnp.broadcast_to(rpb_ref[0, 0, pl.ds(kr - r + NA_KH - 1, 1), :], (GRID_W, LANES))
                        halves.append(pltpu.roll(row, GRID_W * (1 - half), 1, stride=1, stride_axis=0))
                    else:
                        halves.append(None)
                neg = jnp.full((GRID_W, LANES), -jnp.inf, F32)
                lo_half = neg if halves[0] is None else jnp.where(col_ok, halves[0], neg)
                hi_half = neg if halves[1] is None else jnp.where(col_ok, halves[1], neg)
                o_ref[0, p, 0, rr * GRID_W:(rr + 1) * GRID_W, kp * LANES:(kp + 1) * LANES] = (
                    jnp.where(lower, lo_half, hi_half))


def _na_bias(na_rpb):
    ny, nx = 2 * NA_KH - 1, 2 * NA_KW - 1
    padded = jnp.pad(na_rpb.astype(F32), ((0, 0), (0, 0), (0, 16 - ny), (NA_DX_LANE, LANES - NA_DX_LANE - nx)))
    heads = 2 * N_PAIRS
    return pl.pallas_call(
        _na_bias_kernel,
        name="nbr_bias",
        grid=(DEPTH, heads),
        in_specs=[pl.BlockSpec((1, 1, 16, LANES), lambda l, h: (l, h, 0, 0))],
        out_specs=pl.BlockSpec((1, len(NA_PATTERNS), 1, NA_QBLK, NA_KROWS * GRID_W), lambda l, h: (l, 0, h, 0, 0)),
        out_shape=jax.ShapeDtypeStruct((DEPTH, len(NA_PATTERNS), heads, NA_QBLK, NA_KROWS * GRID_W), F32),
        compiler_params=_params(("arbitrary", "arbitrary")),
    )(padded)


def _na_kernel(q_ref, k0_ref, k1_ref, k2_ref, v0_ref, v1_ref, v2_ref, ck_ref, cv_ref, bias_ref, o_ref):
    for pp in range(N_PAIRS):
        lanes = slice(pp * LANES, (pp + 1) * LANES)
        qb = (q_ref[:, lanes] * (HEAD_DIM ** -0.5)).astype(BF16)
        ks = [r[:, lanes].astype(BF16) for r in (k0_ref, k1_ref, k2_ref)] + [ck_ref[0, 0, :, lanes].astype(BF16)]
        vs = [r[:, lanes].astype(BF16) for r in (v0_ref, v1_ref, v2_ref)] + [cv_ref[0, 0, :, lanes].astype(BF16)]
        outs = []
        for h in range(2):
            qh = jnp.where(_head_select(h, qb.shape), qb, jnp.zeros_like(qb))
            ss = []
            for j in range(4):
                s = lax.dot_general(qh, ks[j], NT_DIMS, preferred_element_type=F32)
                if j < 3:
                    s = s + bias_ref[0, 2 * pp + h, :, j * NA_QBLK:(j + 1) * NA_QBLK]
                ss.append(s)
            m = functools.reduce(jnp.maximum, [jnp.max(s, axis=1, keepdims=True) for s in ss])
            ps = [jnp.exp(s - m) for s in ss]
            denom = functools.reduce(jnp.add, [jnp.sum(p, axis=1, keepdims=True) for p in ps])
            acc = functools.reduce(jnp.add, [jnp.dot(p.astype(BF16), v, preferred_element_type=F32)
                                             for p, v in zip(ps, vs)])
            outs.append(acc / denom)
        o_ref[:, lanes] = jnp.where(_head_select(0, outs[0].shape), outs[0], outs[1]).astype(BF16)


def _neighbourhood_attention(z, bias, cache_k, cache_v, layer):
    base = NP // NA_QBLK
    width = N_PAIRS * LANES
    heads = 2 * N_PAIRS

    def kvspec(cb, j):
        return pl.BlockSpec((NA_QBLK, width), lambda b, i: (base + b * NA_NBLK + _na_key_block(i) + j, cb))

    cspec = pl.BlockSpec((1, 1, SEQ, width), lambda b, i: (b, layer, 0, 0))
    pattern = lambda i: jnp.where(i == 0, 0, jnp.where(i == NA_NBLK - 1, 2, 1))
    ck = cache_k.reshape(DEC_BATCH, DEPTH, SEQ, width)
    cv = cache_v.reshape(DEC_BATCH, DEPTH, SEQ, width)
    return pl.pallas_call(
        _na_kernel,
        name="nbr_attn",
        grid=(DEC_BATCH, NA_NBLK),
        in_specs=[pl.BlockSpec((NA_QBLK, width), lambda b, i: (base + b * NA_NBLK + i, 5))]
                 + [kvspec(6, j) for j in range(3)] + [kvspec(7, j) for j in range(3)]
                 + [cspec, cspec,
                    pl.BlockSpec((None, 1, heads, NA_QBLK, NA_KROWS * GRID_W),
                                 lambda b, i: (layer, pattern(i), 0, 0, 0))],
        out_specs=pl.BlockSpec((NA_QBLK, width), lambda b, i: (b * NA_NBLK + i, 0)),
        out_shape=jax.ShapeDtypeStruct((NS, width), BF16),
        compiler_params=_params(("arbitrary", "arbitrary"), VMEM_LIMIT),
    )(z, z, z, z, z, z, z, ck, cv, bias)


def _merge_kernel(rp_ref, rs_ref, pp_ref, ps_ref, ap_ref, as_ref, xp_ref, xs_ref, g0_ref, g1_ref, g2_ref,
                  wr_ref, wp_ref, wa_ref, wo_ref, m_ref, ln_ref, ln_ffn_ref, wrt_ref,
                  o_ref, h_ref, aff_ref, wrb, wpb, wab, wob):
    @pl.when(pl.program_id(0) == 0)
    def _():
        wrb[...] = wr_ref[...].astype(BF16)
        wpb[...] = wp_ref[...].astype(BF16)
        wab[...] = wa_ref[...].astype(BF16)
        wob[...] = wo_ref[...].astype(BF16)

    branch = lambda p_ref, s_ref, w: jnp.dot(_pick(p_ref, s_ref), w[...], preferred_element_type=F32)
    merged = (_sigmoid(g0_ref[...]) * branch(rp_ref, rs_ref, wrb)
              + _sigmoid(g1_ref[...]) * branch(pp_ref, ps_ref, wpb)
              + _sigmoid(g2_ref[...]) * branch(ap_ref, as_ref, wab))
    mix = jnp.dot(merged.astype(BF16), wob[...], preferred_element_type=F32)
    x = _pick(xp_ref, xs_ref) + m_ref[0, 2:3, :] * (_rms(mix) * ln_ref[...])
    o_ref[...] = x

    h = (_rms(x) * ln_ffn_ref[...]) * (1.0 + m_ref[0, 4:5, :]) + m_ref[0, 3:4, :]
    hb = h.astype(BF16)
    h_ref[...] = hb
    hl = (h - hb.astype(F32)).astype(BF16)
    w = wrt_ref[...]
    wb = w.astype(BF16)
    wl = (w - wb.astype(F32)).astype(BF16)
    logits = (lax.dot_general(wb, hb, NT_DIMS, preferred_element_type=F32)
              + lax.dot_general(wb, hl, NT_DIMS, preferred_element_type=F32)
              + lax.dot_general(wl, hb, NT_DIMS, preferred_element_type=F32))
    e = jnp.exp(logits - jnp.max(logits, axis=0, keepdims=True))
    aff = e / jnp.sum(e, axis=0, keepdims=True)
    for j in range(aff_ref.shape[0]):
        aff_ref[j] = aff[:, j * LANES:(j + 1) * LANES]


def _merge(ret_pair, pool_pair, na_pair, x_pair, z, w_ret_o, w_pool_o, w_na_o, w_o, mod_l, ln, ln_ffn, w_router_t,
           layer):
    tm = 512
    half = N_PAIRS * LANES
    row = lambda i: (i, 0)
    const = lambda i: (0, 0)
    slab = lambda i: (layer, 0, 0)
    return pl.pallas_call(
        _merge_kernel,
        name="merge",
        grid=(NT // tm,),
        in_specs=_pair_specs(tm, half) * 3 + _pair_specs(tm, D)
                 + [pl.BlockSpec((tm, D), lambda i, c=c: (i, 4 + c)) for c in range(3)]
                 + [pl.BlockSpec((None, half, D), slab)] * 3
                 + [pl.BlockSpec((None, D, D), slab),
                    pl.BlockSpec((1, N_MOD, D), lambda i: (_mod_row(i * tm), 0, 0)),
                    pl.BlockSpec((1, D), const),
                    pl.BlockSpec((1, D), const),
                    pl.BlockSpec((N_EXPERTS, D), const)],
        out_specs=[pl.BlockSpec((tm, D), row),
                   pl.BlockSpec((tm, D), row),
                   pl.BlockSpec((tm // LANES, N_EXPERTS, LANES), lambda i: (i, 0, 0))],
        out_shape=[jax.ShapeDtypeStruct((NT, D), F32),
                   jax.ShapeDtypeStruct((NT, D), BF16),
                   jax.ShapeDtypeStruct((NT // LANES, N_EXPERTS, LANES), F32)],
        scratch_shapes=[pltpu.VMEM((half, D), BF16)] * 3 + [pltpu.VMEM((D, D), BF16)],
        compiler_params=_params(("arbitrary",), VMEM_LIMIT),
    )(*ret_pair, *pool_pair, *na_pair, *x_pair, z, z, z, w_ret_o, w_pool_o, w_na_o, w_o, mod_l, ln.reshape(1, D),
      ln_ffn.reshape(1, D), w_router_t)


def _route_kernel(aff_ref, slot_ref, offs_ref, *, cap, nblk):
    as_float = lambda bits: lax.bitcast_convert_type(bits, F32)

    def count(pred):
        return jnp.sum(jnp.sum(jnp.where(pred, 1.0, 0.0), axis=0), axis=1, keepdims=True)

    def search(_, lohi):
        lo, hi = lohi
        mid = lo + ((hi - lo + 1) >> 1)
        ok = count(aff_ref[...] >= as_float(mid)[None]) >= cap
        return jnp.where(ok, mid, lo), jnp.where(ok, hi, mid - 1)

    lo0 = jnp.zeros((N_EXPERTS, 1), I32)
    hi0 = jnp.full((N_EXPERTS, 1), 0x7F800000, I32)
    thr_bits, _ = lax.fori_loop(0, 31, search, (lo0, hi0))
    thr = as_float(thr_bits)
    need = cap - count(aff_ref[...] > thr[None])

    upper = (lax.broadcasted_iota(I32, (LANES, LANES), 0)
             < lax.broadcasted_iota(I32, (LANES, LANES), 1)).astype(BF16)

    def running_count(flags):
        inside = jnp.dot(flags.reshape(nblk * N_EXPERTS, LANES).astype(BF16), upper,
                         preferred_element_type=F32).reshape(nblk, N_EXPERTS, LANES)
        totals = jnp.sum(flags, axis=2, keepdims=True)
        before, run = [], jnp.zeros((N_EXPERTS, 1), F32)
        for b in range(nblk):
            before.append(run)
            run = run + totals[b]
        return inside + jnp.stack(before), before, run

    aff = aff_ref[...]
    tied = aff == thr[None]
    tied_rank, _, _ = running_count(jnp.where(tied, 1.0, 0.0))
    chosen = (aff > thr[None]) | (tied & (tied_rank < need[None]))
    rank, before, total = running_count(jnp.where(chosen, 1.0, 0.0))
    slot_ref[...] = jnp.where(chosen, rank, -1.0)

    lane = lax.broadcasted_iota(I32, (N_EXPERTS, LANES), 1)
    offs = jnp.where(lane >= nblk, total, 0.0)
    for b in range(nblk):
        offs = jnp.where(lane == b, before[b], offs)
    offs_ref[...] = offs.astype(I32)


def _route(aff_blocks, *, blk0, nblk, cap):
    return pl.pallas_call(
        functools.partial(_route_kernel, cap=cap, nblk=nblk),
        name="route",
        grid=(1,),
        in_specs=[pl.BlockSpec((nblk, N_EXPERTS, LANES), lambda i: (blk0 // nblk, 0, 0))],
        out_specs=[pl.BlockSpec((nblk, N_EXPERTS, LANES), lambda i: (0, 0, 0)),
                   pl.BlockSpec((N_EXPERTS, LANES), lambda i: (0, 0))],
        out_shape=[jax.ShapeDtypeStruct((nblk, N_EXPERTS, LANES), F32),
                   jax.ShapeDtypeStruct((N_EXPERTS, LANES), I32)],
        compiler_params=_params(("arbitrary",)),
    )(aff_blocks)


GATHER_TOKENS = 256
TILE_BLOCKS = GATHER_TOKENS // LANES
SLOT_ALIGN = 16
WINDOW = LANES // 2
ROUND_SLOTS = WINDOW - SLOT_ALIGN
GATE_TERMS = 3
ROW_W = D + LANES


def _round_bounds(offs_ref, e, t, r):
    off0 = offs_ref[e, TILE_BLOCKS * t]
    off1 = offs_ref[e, TILE_BLOCKS * t + TILE_BLOCKS]
    lo = jnp.minimum(off0 + ROUND_SLOTS * r, off1)
    hi = jnp.minimum(lo + ROUND_SLOTS, off1)
    return lo, hi, pl.multiple_of(lo & -SLOT_ALIGN, SLOT_ALIGN)


def _n_rounds(offs_ref, t):
    most = jnp.int32(0)
    for e in range(N_EXPERTS):
        most = jnp.maximum(most, offs_ref[e, TILE_BLOCKS * t + TILE_BLOCKS] - offs_ref[e, TILE_BLOCKS * t])
    rounds = jnp.int32(0)
    for filled in range(0, GATHER_TOKENS, ROUND_SLOTS):
        rounds = rounds + (most > filled).astype(I32)
    return rounds


def _gate_terms(aff_cols):
    hi = aff_cols.astype(BF16)
    rest = aff_cols - hi.astype(F32)
    mid = rest.astype(BF16)
    lo = (rest - mid.astype(F32)).astype(BF16)
    terms = jnp.stack([hi, mid, lo], axis=-1).reshape(aff_cols.shape[0], GATE_TERMS * N_EXPERTS)
    return jnp.pad(terms, ((0, 0), (0, LANES - GATE_TERMS * N_EXPERTS)))


def _gather_kernel(offs_ref, slot_ref, h_ref, g_ref, xe_hbm, stage_ref, carry_ref, sem, nround_ref, *, n_tiles):
    t = pl.program_id(0)
    cap = xe_hbm.shape[1] - WINDOW

    def out_copy(buf, e, start):
        return pltpu.make_async_copy(stage_ref.at[buf, e], xe_hbm.at[e, pl.ds(start, WINDOW)], sem.at[buf])

    def wait_round(buf):
        for e in range(N_EXPERTS):
            out_copy(buf, e, 0).wait()

    @pl.when(t == 0)
    def _():
        carry_ref[...] = jnp.zeros_like(carry_ref)
        nround_ref[0] = 0
        stage_ref[0, 0] = jnp.zeros((WINDOW, ROW_W), BF16)
        for e in range(N_EXPERTS):
            pltpu.make_async_copy(stage_ref.at[0, 0], xe_hbm.at[e, pl.ds(cap, WINDOW)], sem.at[0]).start()
        wait_round(0)

    hb = jnp.concatenate([h_ref[...], g_ref[...]], axis=1)
    sub = lax.broadcasted_iota(I32, (WINDOW, GATHER_TOKENS), 0).astype(F32)

    def one_round(r, carry):
        done = nround_ref[0]
        buf = done & 1
        bounds = [_round_bounds(offs_ref, e, t, r) for e in range(N_EXPERTS)]
        onehots = []
        for e in range(N_EXPERTS):
            lo, hi, start = bounds[e]
            srow = jnp.concatenate([slot_ref[j, e:e + 1, :] for j in range(TILE_BLOCKS)], axis=1)
            hit = ((srow - start.astype(F32) == sub) & (srow >= lo.astype(F32)) & (srow < hi.astype(F32)))
            onehots.append(jnp.where(hit, 1.0, 0.0).astype(BF16))
        rows = jnp.dot(jnp.concatenate(onehots, axis=0), hb, preferred_element_type=F32)
        for e in range(N_EXPERTS):
            lo, hi, start = bounds[e]
            piece = rows[e * WINDOW:(e + 1) * WINDOW]
            head = piece[:SLOT_ALIGN] + carry_ref[e].astype(F32)
            stage_ref[buf, e, :SLOT_ALIGN, :] = head.astype(BF16)
            stage_ref[buf, e, SLOT_ALIGN:, :] = piece[SLOT_ALIGN:].astype(BF16)
            tail = pl.multiple_of((hi & -SLOT_ALIGN) - start, SLOT_ALIGN)
            carry_ref[e] = stage_ref[buf, e, pl.ds(tail, SLOT_ALIGN), :]

        @pl.when(done > 0)
        def _():
            wait_round(1 - buf)

        for e in range(N_EXPERTS):
            out_copy(buf, e, bounds[e][2]).start()
        nround_ref[0] = done + 1
        return carry

    lax.fori_loop(0, _n_rounds(offs_ref, t), one_round, 0)

    @pl.when((t == n_tiles - 1) & (nround_ref[0] > 0))
    def _():
        wait_round((nround_ref[0] - 1) & 1)


def _gather(offs, slot, h_all, gate_terms, *, row0, n, cap):
    n_tiles = n // GATHER_TOKENS
    tile0 = row0 // GATHER_TOKENS
    return pl.pallas_call(
        functools.partial(_gather_kernel, n_tiles=n_tiles),
        name="gather",
        grid_spec=pltpu.PrefetchScalarGridSpec(
            num_scalar_prefetch=1,
            grid=(n_tiles,),
            in_specs=[pl.BlockSpec((TILE_BLOCKS, N_EXPERTS, LANES), lambda t, o: (t, 0, 0)),
                      pl.BlockSpec((GATHER_TOKENS, D), lambda t, o: (tile0 + t, 0)),
                      pl.BlockSpec((GATHER_TOKENS, LANES), lambda t, o: (t, 0))],
            out_specs=pl.BlockSpec(memory_space=pl.ANY),
            scratch_shapes=[pltpu.VMEM((2, N_EXPERTS, WINDOW, ROW_W), BF16),
                            pltpu.VMEM((N_EXPERTS, SLOT_ALIGN, ROW_W), BF16),
                            pltpu.SemaphoreType.DMA((2,)),
                            pltpu.SMEM((1,), I32)]),
        out_shape=jax.ShapeDtypeStruct((N_EXPERTS, cap + WINDOW, ROW_W), BF16),
        compiler_params=_params(("arbitrary",), VMEM_LIMIT),
    )(offs, slot, h_all, gate_terms)


FF_CHUNK = 512


def _experts_kernel(xp_ref, xs_ref, wg_ref, wu_ref, wd_ref, yp_ref, ys_ref, accp_ref, accs_ref, *, n_f):
    f = pl.program_id(1)
    wg = wg_ref[0].astype(BF16)
    wu = wu_ref[0].astype(BF16)
    wd = wd_ref[0].astype(BF16)

    def ffn(x):
        a = jnp.dot(x, wg, preferred_element_type=F32)
        b = jnp.dot(x, wu, preferred_element_type=F32)
        return jnp.dot((_silu(a) * b).astype(BF16), wd, preferred_element_type=F32)

    @pl.when(f == 0)
    def _():
        accp_ref[...] = ffn(xp_ref[0, :, :D])
        accs_ref[...] = ffn(xs_ref[0, :, :D])

    @pl.when(f > 0)
    def _():
        accp_ref[...] += ffn(xp_ref[0, :, :D])
        accs_ref[...] += ffn(xs_ref[0, :, :D])

    @pl.when(f == n_f - 1)
    def _():
        first = GATE_TERMS * pl.program_id(0)
        for x_ref, y_ref, acc_ref in ((xp_ref, yp_ref, accp_ref), (xs_ref, ys_ref, accs_ref)):
            cap = acc_ref.shape[0]
            lane = lax.broadcasted_iota(I32, (cap, LANES), 1)
            mine = (lane >= first) & (lane < first + GATE_TERMS)
            gate = jnp.sum(jnp.where(mine, x_ref[0, :, D:].astype(F32), 0.0), axis=1, keepdims=True)
            y_ref[0, :cap, :] = (acc_ref[...] * gate).astype(BF16)
            y_ref[0, cap:, :] = jnp.zeros((WINDOW, D), BF16)


def _experts(xe_p, xe_s, w_gate, w_up, w_down, layer):
    n_f = EXPERT_FF // FF_CHUNK
    cap_p, cap_s = xe_p.shape[1] - WINDOW, xe_s.shape[1] - WINDOW
    spec = lambda rows, width: pl.BlockSpec((1, rows, width), lambda e, f: (e, 0, 0))
    out = lambda cap: jax.ShapeDtypeStruct((N_EXPERTS, cap + WINDOW, D), BF16)
    return pl.pallas_call(
        functools.partial(_experts_kernel, n_f=n_f),
        name="experts",
        grid=(N_EXPERTS, n_f),
        in_specs=[spec(cap_p, ROW_W), spec(cap_s, ROW_W),
                  pl.BlockSpec((None, 1, D, FF_CHUNK), lambda e, f: (layer, e, 0, f)),
                  pl.BlockSpec((None, 1, D, FF_CHUNK), lambda e, f: (layer, e, 0, f)),
                  pl.BlockSpec((None, 1, FF_CHUNK, D), lambda e, f: (layer, e, f, 0))],
        out_specs=[spec(cap_p + WINDOW, D), spec(cap_s + WINDOW, D)],
        out_shape=[out(cap_p), out(cap_s)],
        scratch_shapes=[pltpu.VMEM((cap_p, D), F32), pltpu.VMEM((cap_s, D), F32)],
        compiler_params=_params(("arbitrary", "arbitrary"), VMEM_LIMIT),
    )(xe_p, xe_s, w_gate, w_up, w_down)


def _combine_kernel(offs_ref, slot_ref, x_ref, m_ref, ln_ref, ye_hbm, o_ref, stage_ref, sem, *, n_tiles):
    t = pl.program_id(0)
    buf = t & 1

    def in_copy(b, e, start):
        return pltpu.make_async_copy(ye_hbm.at[e, pl.ds(start, WINDOW)],
                                     stage_ref.at[b, pl.ds(e * WINDOW, WINDOW)], sem.at[b])

    def start_round(b, tile, r):
        for e in range(N_EXPERTS):
            in_copy(b, e, _round_bounds(offs_ref, e, tile, r)[2]).start()

    def wait_round(b):
        for e in range(N_EXPERTS):
            in_copy(b, e, 0).wait()

    @pl.when(t == 0)
    def _():
        start_round(0, 0, 0)

    wait_round(buf)

    @pl.when(t + 1 < n_tiles)
    def _():
        start_round(1 - buf, t + 1, 0)

    slot = slot_ref[...]
    lane = lax.broadcasted_iota(I32, (GATHER_TOKENS, LANES), 1)
    first = lane < WINDOW
    row_in_window = (lane % WINDOW).astype(F32)

    def token_rows(b, r):
        onehots = []
        for e in range(0, N_EXPERTS, 2):
            (lo0, hi0, st0), (lo1, hi1, st1) = (_round_bounds(offs_ref, e + i, t, r) for i in range(2))
            s = jnp.where(first, slot[:, e:e + 1], slot[:, e + 1:e + 2])
            lo = jnp.where(first, lo0, lo1).astype(F32)
            hi = jnp.where(first, hi0, hi1).astype(F32)
            start = jnp.where(first, st0, st1).astype(F32)
            hit = (s - start == row_in_window) & (s >= lo) & (s < hi)
            onehots.append(jnp.where(hit, 1.0, 0.0).astype(BF16))
        return jnp.dot(jnp.concatenate(onehots, axis=1), stage_ref[b], preferred_element_type=F32)

    o_ref[...] = token_rows(buf, 0)

    def extra_round(r, carry):
        start_round(buf, t, r)
        wait_round(buf)
        o_ref[...] += token_rows(buf, r)
        return carry

    lax.fori_loop(1, _n_rounds(offs_ref, t), extra_round, 0)
    o_ref[...] = x_ref[...] + m_ref[0, 5:6, :] * (_rms(o_ref[...]) * ln_ref[...])


def _combine(offs, ye, slot_cols, x_all, mod_l, ln, *, row0, n):
    n_tiles = n // GATHER_TOKENS
    tile0 = row0 // GATHER_TOKENS
    return pl.pallas_call(
        functools.partial(_combine_kernel, n_tiles=n_tiles),
        name="combine",
        grid_spec=pltpu.PrefetchScalarGridSpec(
            num_scalar_prefetch=1,
            grid=(n_tiles,),
            in_specs=[pl.BlockSpec((GATHER_TOKENS, N_EXPERTS), lambda t, o: (t, 0)),
                      pl.BlockSpec((GATHER_TOKENS, D), lambda t, o: (tile0 + t, 0)),
                      pl.BlockSpec((1, N_MOD, D), lambda t, o: (_mod_row(row0 + t * GATHER_TOKENS), 0, 0)),
                      pl.BlockSpec((1, D), lambda t, o: (0, 0)),
                      pl.BlockSpec(memory_space=pl.ANY)],
            out_specs=pl.BlockSpec((GATHER_TOKENS, D), lambda t, o: (t, 0)),
            scratch_shapes=[pltpu.VMEM((2, N_EXPERTS * WINDOW, D), BF16),
                            pltpu.SemaphoreType.DMA((2,))]),
        out_shape=jax.ShapeDtypeStruct((n, D), F32),
        compiler_params=_params(("arbitrary",), VMEM_LIMIT),
    )(offs, slot_cols, x_all, mod_l, ln.reshape(1, D), ye)


def _token_major(blocks):
    return blocks.transpose(0, 2, 1).reshape(-1, N_EXPERTS)


def kernel(x_prompt, x_sample, cache_k, cache_v, state_ret, c, c_ctx, w_mod, b_mod, ln_pre_mix, ln_post_mix,
           ln_pre_ffn, ln_post_ffn, w_in, ret_decay, pool_w, pool_scale, na_rpb, w_ret_o, w_pool_o, w_na_o, w_o,
           w_router, w_gate, w_up, w_down):
    cvecs = jnp.zeros((8, D), F32).at[0].set(c_ctx).at[1:1 + DEC_BATCH].set(c)
    mod = _modulation(cvecs, w_mod, b_mod)
    rope = _rope_tables()
    na_bias = _na_bias(na_rpb)
    x_pair = (x_prompt.reshape(NP, D), x_sample.reshape(NS, D))
    h_layers, new_s = [], []
    groups = ((0, NP, NP // N_EXPERTS * 2), (NP, NS, NS // N_EXPERTS * 2))

    for l in range(DEPTH):
        mod_l = mod[l]
        h_all = _prenorm(x_pair, mod_l, ln_pre_mix[l])
        h_layers.append(h_all)
        z = _in_proj(h_all, w_in, l)
        log_g = jax.nn.log_sigmoid(ret_decay[l].astype(F32))

        ret_p, st = _retention_ctx(z, log_g)
        ret_s = _retention(z, log_g, rope, state_ret, l)
        pool_p = _pool(z, pool_w[l], pool_scale[l], nb=BATCH, seq=SEQ, per_step=4, row_block0=0)
        pool_s = _pool(z, pool_w[l], pool_scale[l], nb=DEC_BATCH, seq=DEC_SEQ, per_step=1,
                       row_block0=NP // DEC_SEQ)
        na_p = _ctx_attention(z)
        na_s = _neighbourhood_attention(z, na_bias, cache_k, cache_v, l)
        x_mid, h2, aff = _merge((ret_p, ret_s), (pool_p, pool_s), (na_p, na_s), x_pair, z, w_ret_o, w_pool_o, w_na_o,
                                w_o, mod_l, ln_post_mix[l], ln_pre_ffn[l], w_router[l].T, l)
        new_s.append(st)
        routed = []
        for row0, n, cap in groups:
            slot, offs = _route(aff, blk0=row0 // LANES, nblk=n // LANES, cap=cap)
            gate_terms = _gate_terms(_token_major(aff[row0 // LANES:(row0 + n) // LANES]))
            routed.append((slot, offs, _gather(offs, slot, h2, gate_terms, row0=row0, n=n, cap=cap)))
        ye = _experts(routed[0][2], routed[1][2], w_gate, w_up, w_down, l)
        outs = []
        for (row0, n, cap), (slot, offs, _), y in zip(groups, routed, ye):
            outs.append(_combine(offs, y, _token_major(slot), x_mid, mod_l, ln_post_ffn[l], row0=row0, n=n))
        x_pair = tuple(outs)

    y_prompt = x_pair[0].reshape(BATCH, SEQ, D)
    y_sample = x_pair[1].reshape(DEC_BATCH, DEC_SEQ, D)
    new_k, new_v = _kv_proj(h_layers, w_in)
    return (y_prompt, y_sample, new_k, new_v, jnp.stack(new_s, axis=1))
```

```python
import functools

import numpy as np
import jax
import jax.numpy as jnp
from jax import lax
from jax.experimental import pallas as pl
from jax.experimental.pallas import tpu as pltpu

F32 = jnp.float32
BF16 = jnp.bfloat16
I32 = jnp.int32

D = 1024
BATCH, SEQ = 32, 256
DEC_BATCH, DEC_SEQ = 2, 2048
DEPTH = 2
NP = BATCH * SEQ
NS = DEC_BATCH * DEC_SEQ
NT = NP + NS
GRID_W = 64
N_MOD = 6
EPS = 1e-6
ROPE_BASE = 10000.0
HEAD_DIM = 64
N_PAIRS = 4
CHUNK = 512
POOL_WINDOWS = (2, 4, 8, 16)
POOL_PAD = 16
NA_KH, NA_KW = 8, 16
N_EXPERTS = 16
EXPERT_FF = 2048
IN_COLS = 7168
LANES = 128
VMEM_LIMIT = 56 * 1024 * 1024

NT_DIMS = (((1,), (1,)), ((), ()))


def _params(sem, vmem=None):
    return pltpu.CompilerParams(dimension_semantics=sem, vmem_limit_bytes=vmem)


def _mod_row(row_start):
    return jnp.where(row_start < NP, 0, 1 + (row_start - NP) // DEC_SEQ)


def _sigmoid(x):
    return 0.5 * jnp.tanh(0.5 * x) + 0.5


def _silu(x):
    return x * _sigmoid(x)


def _rms(x):
    return x * lax.rsqrt(jnp.mean(x * x, axis=-1, keepdims=True) + EPS)


def _pair_specs(tm, width):
    n_p = NP // tm
    return [pl.BlockSpec((tm, width), lambda i: (jnp.minimum(i, n_p - 1), 0)),
            pl.BlockSpec((tm, width), lambda i: (jnp.maximum(i - n_p, 0), 0))]


def _pick(p_ref, s_ref):
    return jnp.where(pl.program_id(0) < NP // p_ref.shape[0], p_ref[...], s_ref[...])


def _mod_kernel(c_ref, w_ref, b_ref, o_ref):
    a = _silu(c_ref[...]).astype(BF16)
    o_ref[0] = jnp.dot(a, w_ref[0].astype(BF16), preferred_element_type=F32) + b_ref[0]


def _modulation(cvecs, w_mod, b_mod):
    out = pl.pallas_call(
        _mod_kernel,
        name="modulation",
        grid=(DEPTH, N_MOD),
        in_specs=[pl.BlockSpec((8, D), lambda l, j: (0, 0)),
                  pl.BlockSpec((1, D, D), lambda l, j: (l, 0, j)),
                  pl.BlockSpec((1, 1, D), lambda l, j: (l, 0, j))],
        out_specs=pl.BlockSpec((1, 8, D), lambda l, j: (l, 0, j)),
        out_shape=jax.ShapeDtypeStruct((DEPTH, 8, N_MOD * D), F32),
        compiler_params=_params(("arbitrary", "arbitrary")),
    )(cvecs, w_mod, b_mod.reshape(DEPTH, 1, N_MOD * D))
    return out.reshape(DEPTH, 8, N_MOD, D)


def _prenorm_kernel(xp_ref, xs_ref, m_ref, ln_ref, h_ref):
    y = _rms(_pick(xp_ref, xs_ref)) * ln_ref[...]
    h_ref[...] = (y * (1.0 + m_ref[0, 1:2, :]) + m_ref[0, 0:1, :]).astype(BF16)


def _prenorm(x_pair, mod_l, ln):
    tm = 1024
    return pl.pallas_call(
        _prenorm_kernel,
        name="prenorm",
        grid=(NT // tm,),
        in_specs=_pair_specs(tm, D)
                 + [pl.BlockSpec((1, N_MOD, D), lambda i: (_mod_row(i * tm), 0, 0)),
                  pl.BlockSpec((1, D), lambda i: (0, 0))],
        out_specs=pl.BlockSpec((tm, D), lambda i: (i, 0)),
        out_shape=jax.ShapeDtypeStruct((NT, D), BF16),
        compiler_params=_params(("arbitrary",)),
    )(*x_pair, mod_l, ln.reshape(1, D))


def _mm_kernel(a_ref, w_ref, o_ref, wb_ref):
    @pl.when(pl.program_id(1) == 0)
    def _():
        wb_ref[...] = w_ref[...].astype(BF16)

    o_ref[...] = jnp.dot(a_ref[...], wb_ref[...], preferred_element_type=F32)


def _kv_kernel(h0_ref, h1_ref, w_ref, k_ref, v_ref, wb_ref):
    @pl.when(pl.program_id(1) == 0)
    def _():
        wb_ref[...] = w_ref[0].astype(BF16)

    h = jnp.where(pl.program_id(0) == 0, h0_ref[...], h1_ref[...])
    kv = jnp.dot(h, wb_ref[...], preferred_element_type=F32)
    half = N_PAIRS * LANES
    k_ref[...] = kv[:, :half].reshape(k_ref.shape)
    v_ref[...] = kv[:, half:].reshape(v_ref.shape)


def _kv_proj(h_layers, w_in):
    per = 4
    kv_col = 3072 // D
    hspec = pl.BlockSpec((per * SEQ, D), lambda l, i: (i, 0))
    ospec = pl.BlockSpec((per, 1, SEQ, N_PAIRS * LANES), lambda l, i: (i, l, 0, 0))
    shape = jax.ShapeDtypeStruct((BATCH, DEPTH, SEQ, N_PAIRS * LANES), F32)
    k, v = pl.pallas_call(
        _kv_kernel,
        name="kv_proj",
        grid=(DEPTH, BATCH // per),
        in_specs=[hspec, hspec, pl.BlockSpec((1, D, D), lambda l, i: (l, 0, kv_col))],
        out_specs=[ospec, ospec],
        out_shape=[shape, shape],
        scratch_shapes=[pltpu.VMEM((D, D), BF16)],
        compiler_params=_params(("arbitrary", "arbitrary"), VMEM_LIMIT),
    )(*h_layers, w_in)
    cache_shape = (BATCH, DEPTH, SEQ, 2 * N_PAIRS, HEAD_DIM)
    return k.reshape(cache_shape), v.reshape(cache_shape)


def _in_proj(h_all, w_in, layer):
    tm, tn = 1024, 1792
    return pl.pallas_call(
        _mm_kernel,
        name="in_proj",
        grid=(IN_COLS // tn, NT // tm),
        in_specs=[pl.BlockSpec((tm, D), lambda j, i: (i, 0)),
                  pl.BlockSpec((None, D, tn), lambda j, i: (layer, 0, j))],
        out_specs=pl.BlockSpec((tm, tn), lambda j, i: (i, j)),
        out_shape=jax.ShapeDtypeStruct((NT, IN_COLS), F32),
        scratch_shapes=[pltpu.VMEM((D, tn), BF16)],
        compiler_params=_params(("arbitrary", "arbitrary"), VMEM_LIMIT),
    )(h_all, w_in)


def _swap16(x):
    lane = lax.broadcasted_iota(I32, x.shape, 1)
    return jnp.where((lane // 16) % 2 == 0, pltpu.roll(x, LANES - 16, 1), pltpu.roll(x, 16, 1))


def _block_diag(top, bottom):
    z = jnp.zeros((HEAD_DIM, HEAD_DIM), F32)
    return jnp.concatenate([jnp.concatenate([top, z], axis=1),
                            jnp.concatenate([z, bottom], axis=1)], axis=0)


RET_PAIRS_PER_STEP = 2


def _retention_kernel(lg_ref, q_ref, k_ref, v_ref, g_ref, cos_ref, sin_ref, s0_ref, o_ref, sf_scr, sb_scr):
    for pp in range(RET_PAIRS_PER_STEP):
        _retention_pair(lg_ref, q_ref, k_ref, v_ref, g_ref, cos_ref, sin_ref, s0_ref, o_ref, sf_scr, sb_scr,
                        pp, pl.program_id(1) * RET_PAIRS_PER_STEP + pp)


def _retention_pair(lg_ref, q_ref, k_ref, v_ref, g_ref, cos_ref, sin_ref, s0_ref, o_ref, sf_scr, sb_scr, pp, pair):
    n_chunks = DEC_SEQ // CHUNK
    lanes = slice(pp * LANES, (pp + 1) * LANES)
    lane1 = lax.broadcasted_iota(I32, (1, LANES), 1)
    lo1 = lane1 < HEAD_DIM
    lgf = jnp.where(lo1, lg_ref[0, 2 * pair], lg_ref[0, 2 * pair + 1])
    lgb = jnp.where(lo1, lg_ref[1, 2 * pair], lg_ref[1, 2 * pair + 1])
    lg_heads = [(lg_ref[0, 2 * pair], lg_ref[1, 2 * pair]),
                (lg_ref[0, 2 * pair + 1], lg_ref[1, 2 * pair + 1])]

    rel = (lax.broadcasted_iota(I32, (CHUNK, CHUNK), 0) - lax.broadcasted_iota(I32, (CHUNK, CHUNK), 1)).astype(F32)
    lo_mask = lax.broadcasted_iota(I32, (CHUNK, LANES), 1) < HEAD_DIM
    blockdiag = ((lax.broadcasted_iota(I32, (LANES, LANES), 0) < HEAD_DIM)
                 == (lax.broadcasted_iota(I32, (LANES, LANES), 1) < HEAD_DIM))
    posf = lax.broadcasted_iota(I32, (CHUNK, LANES), 0).astype(F32)
    dmat = []
    for hf, hb in lg_heads:
        dmat.append(jnp.where(rel >= 0, jnp.exp(jnp.where(rel >= 0, rel, 0.0) * hf), 0.0)
                    + jnp.where(rel <= 0, jnp.exp(jnp.where(rel <= 0, -rel, 0.0) * hb), 0.0))
    qdec_f = jnp.exp((posf + 1.0) * lgf)
    kdec_f = jnp.exp((CHUNK - 1.0 - posf) * lgf)
    qdec_b = jnp.exp((CHUNK - posf) * lgb)
    kdec_b = jnp.exp(posf * lgb)
    sdec_f = jnp.exp(CHUNK * lgf)
    sdec_b = jnp.exp(CHUNK * lgb)

    def load(c):
        rows = pl.ds(c * CHUNK, CHUNK)
        cs, sn = cos_ref[rows, :], sin_ref[rows, :]
        q = q_ref[rows, lanes]
        k = k_ref[rows, lanes]
        q = q * cs + _swap16(q) * sn
        k = k * cs + _swap16(k) * sn
        return q, k * (HEAD_DIM ** -0.5), v_ref[rows, lanes]

    def state_update(s, k, v, kdec, sdec):
        kd = (k * kdec).T.astype(BF16)
        u = jnp.dot(kd, v.astype(BF16), preferred_element_type=F32)
        return s * sdec + jnp.where(blockdiag, u, 0.0)

    h0, h1 = 2 * pp, 2 * pp + 1
    s_f = _block_diag(s0_ref[0, 0, 0, h0], s0_ref[0, 0, 0, h1])
    s_b = _block_diag(s0_ref[0, 0, 1, h0], s0_ref[0, 0, 1, h1])
    scr0 = pp * n_chunks
    for c in range(n_chunks):
        sf_scr[scr0 + c] = s_f
        _, k, v = load(c)
        s_f = state_update(s_f, k, v, kdec_f, sdec_f)
    for c in reversed(range(n_chunks)):
        sb_scr[scr0 + c] = s_b
        _, k, v = load(c)
        s_b = state_update(s_b, k, v, kdec_b, sdec_b)

    for c in range(n_chunks):
        q, k, v = load(c)
        qb, kb, vb = q.astype(BF16), k.astype(BF16), v.astype(BF16)
        outs = []
        for h in range(2):
            qh = jnp.where(lo_mask if h == 0 else ~lo_mask, qb, jnp.zeros_like(qb))
            a = lax.dot_general(qh, kb, NT_DIMS, preferred_element_type=F32) * dmat[h]
            outs.append(jnp.dot(a.astype(BF16), vb, preferred_element_type=F32))
        o = jnp.where(lo_mask, outs[0], outs[1])
        o = o + jnp.dot(qb, sf_scr[scr0 + c].astype(BF16), preferred_element_type=F32) * qdec_f
        o = o + jnp.dot(qb, sb_scr[scr0 + c].astype(BF16), preferred_element_type=F32) * qdec_b
        o2 = o * o
        ms0 = jnp.sum(jnp.where(lo_mask, o2, 0.0), axis=1, keepdims=True) * (1.0 / HEAD_DIM)
        ms1 = jnp.sum(jnp.where(lo_mask, 0.0, o2), axis=1, keepdims=True) * (1.0 / HEAD_DIM)
        inv = jnp.where(lo_mask, lax.rsqrt(ms0 + EPS), lax.rsqrt(ms1 + EPS))
        g = g_ref[pl.ds(c * CHUNK, CHUNK), lanes]
        o_ref[pl.ds(c * CHUNK, CHUNK), lanes] = (_silu(g) * (o * inv)).astype(BF16)


def _retention(z, log_g, rope, state_ret, layer):
    pps = RET_PAIRS_PER_STEP
    width = pps * LANES
    row_block0 = NP // DEC_SEQ
    scratch = pltpu.VMEM((pps * (DEC_SEQ // CHUNK), LANES, LANES), F32)

    def zspec(cb):
        return pl.BlockSpec((DEC_SEQ, width), lambda b, p: (row_block0 + b, cb // pps + p))

    table = pl.BlockSpec((DEC_SEQ, LANES), lambda b, p: (0, 0))
    return pl.pallas_call(
        _retention_kernel,
        name="retention",
        grid=(DEC_BATCH, N_PAIRS // pps),
        in_specs=[pl.BlockSpec(memory_space=pltpu.SMEM), zspec(0), zspec(4), zspec(8), zspec(12), table, table,
                  pl.BlockSpec((1, 1, 2, 2 * pps, HEAD_DIM, HEAD_DIM), lambda b, p: (b, layer, 0, p, 0, 0))],
        out_specs=pl.BlockSpec((DEC_SEQ, width), lambda b, p: (b, p)),
        out_shape=jax.ShapeDtypeStruct((NS, N_PAIRS * LANES), BF16),
        scratch_shapes=[scratch, scratch],
        compiler_params=_params(("arbitrary", "arbitrary"), VMEM_LIMIT),
    )(log_g, z, z, z, z, *rope, state_ret)


def _retention_ctx_kernel(lg_ref, q_ref, k_ref, v_ref, g_ref, o_ref, st_ref, decay_ref):
    heads = 2 * N_PAIRS

    @pl.when(pl.program_id(0) == 0)
    def _():
        rel = (lax.broadcasted_iota(I32, (SEQ, SEQ), 0) - lax.broadcasted_iota(I32, (SEQ, SEQ), 1)).astype(F32)
        for h in range(heads):
            decay_ref[h] = (jnp.where(rel >= 0, jnp.exp(jnp.where(rel >= 0, rel, 0.0) * lg_ref[0, h]), 0.0)
                            + jnp.where(rel <= 0, jnp.exp(jnp.where(rel <= 0, -rel, 0.0) * lg_ref[1, h]), 0.0))

    lane = lax.broadcasted_iota(I32, (SEQ, LANES), 1)
    lo_mask = lane < HEAD_DIM
    pos = lax.broadcasted_iota(I32, (SEQ, LANES), 0).astype(F32)
    blockdiag = ((lax.broadcasted_iota(I32, (LANES, LANES), 0) < HEAD_DIM)
                 == (lax.broadcasted_iota(I32, (LANES, LANES), 1) < HEAD_DIM))
    for pp in range(N_PAIRS):
        lanes = slice(pp * LANES, (pp + 1) * LANES)
        h0, h1 = 2 * pp, 2 * pp + 1
        k = k_ref[:, lanes] * (HEAD_DIM ** -0.5)
        qb, kb, vb = q_ref[:, lanes].astype(BF16), k.astype(BF16), v_ref[:, lanes].astype(BF16)
        outs = []
        for h, mask in ((h0, lo_mask), (h1, ~lo_mask)):
            qh = jnp.where(mask, qb, jnp.zeros_like(qb))
            a = lax.dot_general(qh, kb, NT_DIMS, preferred_element_type=F32) * decay_ref[h]
            outs.append(jnp.dot(a.astype(BF16), vb, preferred_element_type=F32))
        o = jnp.where(lo_mask, outs[0], outs[1])
        o2 = o * o
        ms0 = jnp.sum(jnp.where(lo_mask, o2, 0.0), axis=1, keepdims=True) * (1.0 / HEAD_DIM)
        ms1 = jnp.sum(jnp.where(lo_mask, 0.0, o2), axis=1, keepdims=True) * (1.0 / HEAD_DIM)
        inv = jnp.where(lo_mask, lax.rsqrt(ms0 + EPS), lax.rsqrt(ms1 + EPS))
        o_ref[:, lanes] = (_silu(g_ref[:, lanes]) * (o * inv)).astype(BF16)

        for d, age in ((0, SEQ - 1.0 - pos), (1, pos)):
            lg = jnp.where(lo_mask, lg_ref[d, h0], lg_ref[d, h1])
            kd = (k * jnp.exp(age * lg)).T.astype(BF16)
            s = jnp.where(blockdiag, jnp.dot(kd, vb, preferred_element_type=F32), 0.0)
            st_ref[0, d, h0] = s[:HEAD_DIM, :HEAD_DIM]
            st_ref[0, d, h1] = s[HEAD_DIM:, HEAD_DIM:]


def _retention_ctx(z, log_g):
    width = N_PAIRS * LANES
    heads = 2 * N_PAIRS
    zspec = lambda cb: pl.BlockSpec((SEQ, width), lambda b: (b, cb))
    return pl.pallas_call(
        _retention_ctx_kernel,
        name="retention_ctx",
        grid=(BATCH,),
        in_specs=[pl.BlockSpec(memory_space=pltpu.SMEM), zspec(0), zspec(1), zspec(2), zspec(3)],
        out_specs=[pl.BlockSpec((SEQ, width), lambda b: (b, 0)),
                   pl.BlockSpec((1, 2, heads, HEAD_DIM, HEAD_DIM), lambda b: (b, 0, 0, 0, 0))],
        out_shape=[jax.ShapeDtypeStruct((NP, width), BF16),
                   jax.ShapeDtypeStruct((BATCH, 2, heads, HEAD_DIM, HEAD_DIM), F32)],
        scratch_shapes=[pltpu.VMEM((heads, SEQ, SEQ), F32)],
        compiler_params=_params(("arbitrary",)),
    )(log_g, z, z, z, z)


def _rope_tables():
    t = np.arange(DEC_SEQ)
    posn = [(t // GRID_W).astype(np.float32), (t % GRID_W).astype(np.float32)]
    nf = HEAD_DIM // 4
    freqs = (1.0 / (np.float32(ROPE_BASE) ** (np.arange(nf, dtype=np.float32) / np.float32(nf)))).astype(np.float32)
    cos = np.zeros((DEC_SEQ, HEAD_DIM), np.float32)
    sin = np.zeros((DEC_SEQ, HEAD_DIM), np.float32)
    for half in range(2):
        ang = (posn[half][:, None] * freqs[None, :]).astype(np.float32)
        for grp in range(2):
            lo = half * 32 + grp * nf
            cos[:, lo:lo + nf] = np.cos(ang)
            sin[:, lo:lo + nf] = np.sin(ang) * (-1.0 if grp == 0 else 1.0)
    return jnp.asarray(np.tile(cos, (1, 2))), jnp.asarray(np.tile(sin, (1, 2)))


def _pool_kernel(u_ref, w_ref, sc_ref, o_ref, *, seq):
    padded = seq + 2 * POOL_PAD
    t = lax.broadcasted_iota(I32, (seq, 1), 0)
    zpad = jnp.zeros((POOL_PAD, LANES), F32)
    for s in range(u_ref.shape[0] // seq):
        rows = slice(s * seq, (s + 1) * seq)
        for gi, w in enumerate(POOL_WINDOWS):
            cols = slice(gi * LANES, (gi + 1) * LANES)
            x = u_ref[rows, cols]
            run = jnp.concatenate([zpad, x, zpad], axis=0)
            span = 1
            while span < w:
                run = run + pltpu.roll(run, padded - span, 0)
                span *= 2
            win = pltpu.roll(run, padded - (POOL_PAD - w // 2), 0)[:seq]
            cnt = (jnp.minimum(t + w // 2, seq) - jnp.maximum(t - w // 2, 0)).astype(F32)
            pooled = win / cnt - x
            mixed = jnp.dot(pooled.astype(BF16), w_ref[gi].astype(BF16), preferred_element_type=F32)
            o_ref[rows, cols] = (mixed * sc_ref[:, cols]).astype(BF16)


def _pool(z, pool_w, pool_scale, *, nb, seq, per_step, row_block0):
    width = len(POOL_WINDOWS) * LANES
    rows = per_step * seq
    return pl.pallas_call(
        functools.partial(_pool_kernel, seq=seq),
        name="pool",
        grid=(nb // per_step,),
        in_specs=[pl.BlockSpec((rows, width), lambda b: (row_block0 + b, 2048 // width)),
                  pl.BlockSpec((len(POOL_WINDOWS), LANES, LANES), lambda b: (0, 0, 0)),
                  pl.BlockSpec((1, width), lambda b: (0, 0))],
        out_specs=pl.BlockSpec((rows, width), lambda b: (b, 0)),
        out_shape=jax.ShapeDtypeStruct((nb * seq, width), BF16),
        compiler_params=_params(("arbitrary",), VMEM_LIMIT),
    )(z, pool_w, pool_scale.reshape(1, width))


def _head_select(h, shape):
    lane = lax.broadcasted_iota(I32, shape, 1)
    return (lane < HEAD_DIM) if h == 0 else (lane >= HEAD_DIM)


def _ctx_attn_kernel(q_ref, k_ref, v_ref, o_ref):
    for pp in range(N_PAIRS):
        lanes = slice(pp * LANES, (pp + 1) * LANES)
        qb = (q_ref[:, lanes] * (HEAD_DIM ** -0.5)).astype(BF16)
        kb = k_ref[:, lanes].astype(BF16)
        vb = v_ref[:, lanes].astype(BF16)
        outs = []
        for h in range(2):
            qh = jnp.where(_head_select(h, qb.shape), qb, jnp.zeros_like(qb))
            s = lax.dot_general(qh, kb, NT_DIMS, preferred_element_type=F32)
            p = jnp.exp(s - jnp.max(s, axis=1, keepdims=True))
            denom = jnp.sum(p, axis=1, keepdims=True)
            outs.append(jnp.dot(p.astype(BF16), vb, preferred_element_type=F32) / denom)
        o_ref[:, lanes] = jnp.where(_head_select(0, outs[0].shape), outs[0], outs[1]).astype(BF16)


def _ctx_attention(z):
    width = N_PAIRS * LANES

    def zspec(cb):
        return pl.BlockSpec((SEQ, width), lambda b: (b, cb))

    return pl.pallas_call(
        _ctx_attn_kernel,
        name="ctx_attn",
        grid=(BATCH,),
        in_specs=[zspec(5), zspec(6), zspec(7)],
        out_specs=pl.BlockSpec((SEQ, width), lambda b: (b, 0)),
        out_shape=jax.ShapeDtypeStruct((NP, width), BF16),
        compiler_params=_params(("arbitrary",)),
    )(z, z, z)


NA_QROWS = 4
NA_QBLK = NA_QROWS * GRID_W
NA_KROWS = 12
NA_NBLK = DEC_SEQ // NA_QBLK


def _na_key_block(i):
    return jnp.clip(i - 1, 0, NA_NBLK - 3)


NA_PATTERNS = ((0, 0), (NA_QROWS, 0), (DEC_SEQ // GRID_W - NA_QROWS, DEC_SEQ // GRID_W - NA_KROWS))
NA_DX_LANE = GRID_W - (NA_KW - 1)


def _na_bias_kernel(rpb_ref, o_ref):
    rows = DEC_SEQ // GRID_W
    q = lax.broadcasted_iota(I32, (GRID_W, LANES), 0)
    lane = lax.broadcasted_iota(I32, (GRID_W, LANES), 1)
    c = lane % GRID_W
    c_start = jnp.clip(q - NA_KW // 2, 0, GRID_W - NA_KW)
    col_ok = (c >= c_start) & (c < c_start + NA_KW)
    lower = lane < GRID_W
    for p, (r0, ks) in enumerate(NA_PATTERNS):
        for rr in range(NA_QROWS):
            r = r0 + rr
            start = min(max(r - NA_KH // 2, 0), rows - NA_KH)
            for kp in range(NA_KROWS // 2):
                halves = []
                for half in range(2):
                    kr = ks + 2 * kp + half
                    if start <= kr < start + NA_KH:
                        row = jnp.broadcast_to(rpb_ref[0, 0, pl.ds(kr - r + NA_KH - 1, 1), :], (GRID_W, LANES))
                        halves.append(pltpu.roll(row, GRID_W * (1 - half), 1, stride=1, stride_axis=0))
                    else:
                        halves.append(None)
                neg = jnp.full((GRID_W, LANES), -jnp.inf, F32)
                lo_half = neg if halves[0] is None else jnp.where(col_ok, halves[0], neg)
                hi_half = neg if halves[1] is None else jnp.where(col_ok, halves[1], neg)
                o_ref[0, p, 0, rr * GRID_W:(rr + 1) * GRID_W, kp * LANES:(kp + 1) * LANES] = (
                    jnp.where(lower, lo_half, hi_half))


def _na_bias(na_rpb):
    ny, nx = 2 * NA_KH - 1, 2 * NA_KW - 1
    padded = jnp.pad(na_rpb.astype(F32), ((0, 0), (0, 0), (0, 16 - ny), (NA_DX_LANE, LANES - NA_DX_LANE - nx)))
    heads = 2 * N_PAIRS
    return pl.pallas_call(
        _na_bias_kernel,
        name="nbr_bias",
        grid=(DEPTH, heads),
        in_specs=[pl.BlockSpec((1, 1, 16, LANES), lambda l, h: (l, h, 0, 0))],
        out_specs=pl.BlockSpec((1, len(NA_PATTERNS), 1, NA_QBLK, NA_KROWS * GRID_W), lambda l, h: (l, 0, h, 0, 0)),
        out_shape=jax.ShapeDtypeStruct((DEPTH, len(NA_PATTERNS), heads, NA_QBLK, NA_KROWS * GRID_W), F32),
        compiler_params=_params(("arbitrary", "arbitrary")),
    )(padded)


def _na_kernel(q_ref, k0_ref, k1_ref, k2_ref, v0_ref, v1_ref, v2_ref, ck_ref, cv_ref, bias_ref, o_ref):
    for pp in range(N_PAIRS):
        lanes = slice(pp * LANES, (pp + 1) * LANES)
        qb = (q_ref[:, lanes] * (HEAD_DIM ** -0.5)).astype(BF16)
        ks = [r[:, lanes].astype(BF16) for r in (k0_ref, k1_ref, k2_ref)] + [ck_ref[0, 0, :, lanes].astype(BF16)]
        vs = [r[:, lanes].astype(BF16) for r in (v0_ref, v1_ref, v2_ref)] + [cv_ref[0, 0, :, lanes].astype(BF16)]
        outs = []
        for h in range(2):
            qh = jnp.where(_head_select(h, qb.shape), qb, jnp.zeros_like(qb))
            ss = []
            for j in range(4):
                s = lax.dot_general(qh, ks[j], NT_DIMS, preferred_element_type=F32)
                if j < 3:
                    s = s + bias_ref[0, 2 * pp + h, :, j * NA_QBLK:(j + 1) * NA_QBLK]
                ss.append(s)
            m = functools.reduce(jnp.maximum, [jnp.max(s, axis=1, keepdims=True) for s in ss])
            ps = [jnp.exp(s - m) for s in ss]
            denom = functools.reduce(jnp.add, [jnp.sum(p, axis=1, keepdims=True) for p in ps])
            acc = functools.reduce(jnp.add, [jnp.dot(p.astype(BF16), v, preferred_element_type=F32)
                                             for p, v in zip(ps, vs)])
            outs.append(acc / denom)
        o_ref[:, lanes] = jnp.where(_head_select(0, outs[0].shape), outs[0], outs[1]).astype(BF16)


def _neighbourhood_attention(z, bias, cache_k, cache_v, layer):
    base = NP // NA_QBLK
    width = N_PAIRS * LANES
    heads = 2 * N_PAIRS

    def kvspec(cb, j):
        return pl.BlockSpec((NA_QBLK, width), lambda b, i: (base + b * NA_NBLK + _na_key_block(i) + j, cb))

    cspec = pl.BlockSpec((1, 1, SEQ, width), lambda b, i: (b, layer, 0, 0))
    pattern = lambda i: jnp.where(i == 0, 0, jnp.where(i == NA_NBLK - 1, 2, 1))
    ck = cache_k.reshape(DEC_BATCH, DEPTH, SEQ, width)
    cv = cache_v.reshape(DEC_BATCH, DEPTH, SEQ, width)
    return pl.pallas_call(
        _na_kernel,
        name="nbr_attn",
        grid=(DEC_BATCH, NA_NBLK),
        in_specs=[pl.BlockSpec((NA_QBLK, width), lambda b, i: (base + b * NA_NBLK + i, 5))]
                 + [kvspec(6, j) for j in range(3)] + [kvspec(7, j) for j in range(3)]
                 + [cspec, cspec,
                    pl.BlockSpec((None, 1, heads, NA_QBLK, NA_KROWS * GRID_W),
                                 lambda b, i: (layer, pattern(i), 0, 0, 0))],
        out_specs=pl.BlockSpec((NA_QBLK, width), lambda b, i: (b * NA_NBLK + i, 0)),
        out_shape=jax.ShapeDtypeStruct((NS, width), BF16),
        compiler_params=_params(("arbitrary", "arbitrary"), VMEM_LIMIT),
    )(z, z, z, z, z, z, z, ck, cv, bias)


def _merge_kernel(rp_ref, rs_ref, pp_ref, ps_ref, ap_ref, as_ref, xp_ref, xs_ref, g0_ref, g1_ref, g2_ref,
                  wr_ref, wp_ref, wa_ref, wo_ref, m_ref, ln_ref, ln_ffn_ref, wrt_ref,
                  o_ref, h_ref, aff_ref, wrb, wpb, wab, wob):
    @pl.when(pl.program_id(0) == 0)
    def _():
        wrb[...] = wr_ref[...].astype(BF16)
        wpb[...] = wp_ref[...].astype(BF16)
        wab[...] = wa_ref[...].astype(BF16)
        wob[...] = wo_ref[...].astype(BF16)

    branch = lambda p_ref, s_ref, w: jnp.dot(_pick(p_ref, s_ref), w[...], preferred_element_type=F32)
    merged = (_sigmoid(g0_ref[...]) * branch(rp_ref, rs_ref, wrb)
              + _sigmoid(g1_ref[...]) * branch(pp_ref, ps_ref, wpb)
              + _sigmoid(g2_ref[...]) * branch(ap_ref, as_ref, wab))
    mix = jnp.dot(merged.astype(BF16), wob[...], preferred_element_type=F32)
    x = _pick(xp_ref, xs_ref) + m_ref[0, 2:3, :] * (_rms(mix) * ln_ref[...])
    o_ref[...] = x

    h = (_rms(x) * ln_ffn_ref[...]) * (1.0 + m_ref[0, 4:5, :]) + m_ref[0, 3:4, :]
    hb = h.astype(BF16)
    h_ref[...] = hb
    hl = (h - hb.astype(F32)).astype(BF16)
    w = wrt_ref[...]
    wb = w.astype(BF16)
    wl = (w - wb.astype(F32)).astype(BF16)
    logits = (lax.dot_general(wb, hb, NT_DIMS, preferred_element_type=F32)
              + lax.dot_general(wb, hl, NT_DIMS, preferred_element_type=F32)
              + lax.dot_general(wl, hb, NT_DIMS, preferred_element_type=F32))
    e = jnp.exp(logits - jnp.max(logits, axis=0, keepdims=True))
    aff = e / jnp.sum(e, axis=0, keepdims=True)
    for j in range(aff_ref.shape[0]):
        aff_ref[j] = aff[:, j * LANES:(j + 1) * LANES]


def _merge(ret_pair, pool_pair, na_pair, x_pair, z, w_ret_o, w_pool_o, w_na_o, w_o, mod_l, ln, ln_ffn, w_router_t,
           layer):
    tm = 512
    half = N_PAIRS * LANES
    row = lambda i: (i, 0)
    const = lambda i: (0, 0)
    slab = lambda i: (layer, 0, 0)
    return pl.pallas_call(
        _merge_kernel,
        name="merge",
        grid=(NT // tm,),
        in_specs=_pair_specs(tm, half) * 3 + _pair_specs(tm, D)
                 + [pl.BlockSpec((tm, D), lambda i, c=c: (i, 4 + c)) for c in range(3)]
                 + [pl.BlockSpec((None, half, D), slab)] * 3
                 + [pl.BlockSpec((None, D, D), slab),
                    pl.BlockSpec((1, N_MOD, D), lambda i: (_mod_row(i * tm), 0, 0)),
                    pl.BlockSpec((1, D), const),
                    pl.BlockSpec((1, D), const),
                    pl.BlockSpec((N_EXPERTS, D), const)],
        out_specs=[pl.BlockSpec((tm, D), row),
                   pl.BlockSpec((tm, D), row),
                   pl.BlockSpec((tm // LANES, N_EXPERTS, LANES), lambda i: (i, 0, 0))],
        out_shape=[jax.ShapeDtypeStruct((NT, D), F32),
                   jax.ShapeDtypeStruct((NT, D), BF16),
                   jax.ShapeDtypeStruct((NT // LANES, N_EXPERTS, LANES), F32)],
        scratch_shapes=[pltpu.VMEM((half, D), BF16)] * 3 + [pltpu.VMEM((D, D), BF16)],
        compiler_params=_params(("arbitrary",), VMEM_LIMIT),
    )(*ret_pair, *pool_pair, *na_pair, *x_pair, z, z, z, w_ret_o, w_pool_o, w_na_o, w_o, mod_l, ln.reshape(1, D),
      ln_ffn.reshape(1, D), w_router_t)


def _route_kernel(aff_ref, slot_ref, offs_ref, *, cap, nblk):
    as_float = lambda bits: lax.bitcast_convert_type(bits, F32)

    def count(pred):
        return jnp.sum(jnp.sum(jnp.where(pred, 1.0, 0.0), axis=0), axis=1, keepdims=True)

    def search(_, lohi):
        lo, hi = lohi
        mid = lo + ((hi - lo + 1) >> 1)
        ok = count(aff_ref[...] >= as_float(mid)[None]) >= cap
        return jnp.where(ok, mid, lo), jnp.where(ok, hi, mid - 1)

    lo0 = jnp.zeros((N_EXPERTS, 1), I32)
    hi0 = jnp.full((N_EXPERTS, 1), 0x7F800000, I32)
    thr_bits, _ = lax.fori_loop(0, 31, search, (lo0, hi0))
    thr = as_float(thr_bits)
    need = cap - count(aff_ref[...] > thr[None])

    upper = (lax.broadcasted_iota(I32, (LANES, LANES), 0)
             < lax.broadcasted_iota(I32, (LANES, LANES), 1)).astype(BF16)

    def running_count(flags):
        inside = jnp.dot(flags.reshape(nblk * N_EXPERTS, LANES).astype(BF16), upper,
                         preferred_element_type=F32).reshape(nblk, N_EXPERTS, LANES)
        totals = jnp.sum(flags, axis=2, keepdims=True)
        before, run = [], jnp.zeros((N_EXPERTS, 1), F32)
        for b in range(nblk):
            before.append(run)
            run = run + totals[b]
        return inside + jnp.stack(before), before, run

    aff = aff_ref[...]
    tied = aff == thr[None]
    tied_rank, _, _ = running_count(jnp.where(tied, 1.0, 0.0))
    chosen = (aff > thr[None]) | (tied & (tied_rank < need[None]))
    rank, before, total = running_count(jnp.where(chosen, 1.0, 0.0))
    slot_ref[...] = jnp.where(chosen, rank, -1.0)

    lane = lax.broadcasted_iota(I32, (N_EXPERTS, LANES), 1)
    offs = jnp.where(lane >= nblk, total, 0.0)
    for b in range(nblk):
        offs = jnp.where(lane == b, before[b], offs)
    offs_ref[...] = offs.astype(I32)


def _route(aff_blocks, *, blk0, nblk, cap):
    return pl.pallas_call(
        functools.partial(_route_kernel, cap=cap, nblk=nblk),
        name="route",
        grid=(1,),
        in_specs=[pl.BlockSpec((nblk, N_EXPERTS, LANES), lambda i: (blk0 // nblk, 0, 0))],
        out_specs=[pl.BlockSpec((nblk, N_EXPERTS, LANES), lambda i: (0, 0, 0)),
                   pl.BlockSpec((N_EXPERTS, LANES), lambda i: (0, 0))],
        out_shape=[jax.ShapeDtypeStruct((nblk, N_EXPERTS, LANES), F32),
                   jax.ShapeDtypeStruct((N_EXPERTS, LANES), I32)],
        compiler_params=_params(("arbitrary",)),
    )(aff_blocks)


GATHER_TOKENS = 256
TILE_BLOCKS = GATHER_TOKENS // LANES
SLOT_ALIGN = 16
GATE_TERMS = 3
ROW_W = D + LANES


def _round_bounds(offs_ref, e, t, r, window):
    slots = window - SLOT_ALIGN
    off0 = offs_ref[e, TILE_BLOCKS * t]
    off1 = offs_ref[e, TILE_BLOCKS * t + TILE_BLOCKS]
    lo = jnp.minimum(off0 + slots * r, off1)
    hi = jnp.minimum(lo + slots, off1)
    return lo, hi, pl.multiple_of(lo & -SLOT_ALIGN, SLOT_ALIGN)


def _n_rounds(offs_ref, t, window):
    most = jnp.int32(0)
    for e in range(N_EXPERTS):
        most = jnp.maximum(most, offs_ref[e, TILE_BLOCKS * t + TILE_BLOCKS] - offs_ref[e, TILE_BLOCKS * t])
    rounds = jnp.int32(0)
    for filled in range(0, GATHER_TOKENS, window - SLOT_ALIGN):
        rounds = rounds + (most > filled).astype(I32)
    return rounds


def _gate_terms(aff_cols):
    hi = aff_cols.astype(BF16)
    rest = aff_cols - hi.astype(F32)
    mid = rest.astype(BF16)
    lo = (rest - mid.astype(F32)).astype(BF16)
    terms = jnp.stack([hi, mid, lo], axis=-1).reshape(aff_cols.shape[0], GATE_TERMS * N_EXPERTS)
    return jnp.pad(terms, ((0, 0), (0, LANES - GATE_TERMS * N_EXPERTS)))


def _gather_kernel(offs_ref, slot_ref, h_ref, g_ref, xe_hbm, stage_ref, carry_ref, sem, nround_ref, *, n_tiles):
    t = pl.program_id(0)
    window = stage_ref.shape[2]
    cap = xe_hbm.shape[1] - window

    def out_copy(buf, e, start):
        return pltpu.make_async_copy(stage_ref.at[buf, e], xe_hbm.at[e, pl.ds(start, window)], sem.at[buf])

    def wait_round(buf):
        for e in range(N_EXPERTS):
            out_copy(buf, e, 0).wait()

    @pl.when(t == 0)
    def _():
        carry_ref[...] = jnp.zeros_like(carry_ref)
        nround_ref[0] = 0
        stage_ref[0, 0] = jnp.zeros((window, ROW_W), BF16)
        for e in range(N_EXPERTS):
            pltpu.make_async_copy(stage_ref.at[0, 0], xe_hbm.at[e, pl.ds(cap, window)], sem.at[0]).start()
        wait_round(0)

    hb = jnp.concatenate([h_ref[...], g_ref[...]], axis=1)
    sub = lax.broadcasted_iota(I32, (window, GATHER_TOKENS), 0).astype(F32)

    def one_round(r, carry):
        done = nround_ref[0]
        buf = done & 1
        bounds = [_round_bounds(offs_ref, e, t, r, window) for e in range(N_EXPERTS)]
        onehots = []
        for e in range(N_EXPERTS):
            lo, hi, start = bounds[e]
            srow = jnp.concatenate([slot_ref[j, e:e + 1, :] for j in range(TILE_BLOCKS)], axis=1)
            hit = ((srow - start.astype(F32) == sub) & (srow >= lo.astype(F32)) & (srow < hi.astype(F32)))
            onehots.append(jnp.where(hit, 1.0, 0.0).astype(BF16))
        rows = jnp.dot(jnp.concatenate(onehots, axis=0), hb, preferred_element_type=F32)
        for e in range(N_EXPERTS):
            lo, hi, start = bounds[e]
            piece = rows[e * window:(e + 1) * window]
            head = piece[:SLOT_ALIGN] + carry_ref[e].astype(F32)
            stage_ref[buf, e, :SLOT_ALIGN, :] = head.astype(BF16)
            stage_ref[buf, e, SLOT_ALIGN:, :] = piece[SLOT_ALIGN:].astype(BF16)
            tail = pl.multiple_of((hi & -SLOT_ALIGN) - start, SLOT_ALIGN)
            carry_ref[e] = stage_ref[buf, e, pl.ds(tail, SLOT_ALIGN), :]

        @pl.when(done > 0)
        def _():
            wait_round(1 - buf)

        for e in range(N_EXPERTS):
            out_copy(buf, e, bounds[e][2]).start()
        nround_ref[0] = done + 1
        return carry

    lax.fori_loop(0, _n_rounds(offs_ref, t, window), one_round, 0)

    @pl.when((t == n_tiles - 1) & (nround_ref[0] > 0))
    def _():
        wait_round((nround_ref[0] - 1) & 1)


def _gather(offs, slot, h_all, gate_terms, *, row0, n, cap, window):
    n_tiles = n // GATHER_TOKENS
    tile0 = row0 // GATHER_TOKENS
    return pl.pallas_call(
        functools.partial(_gather_kernel, n_tiles=n_tiles),
        name="gather",
        grid_spec=pltpu.PrefetchScalarGridSpec(
            num_scalar_prefetch=1,
            grid=(n_tiles,),
            in_specs=[pl.BlockSpec((TILE_BLOCKS, N_EXPERTS, LANES), lambda t, o: (t, 0, 0)),
                      pl.BlockSpec((GATHER_TOKENS, D), lambda t, o: (tile0 + t, 0)),
                      pl.BlockSpec((GATHER_TOKENS, LANES), lambda t, o: (t, 0))],
            out_specs=pl.BlockSpec(memory_space=pl.ANY),
            scratch_shapes=[pltpu.VMEM((2, N_EXPERTS, window, ROW_W), BF16),
                            pltpu.VMEM((N_EXPERTS, SLOT_ALIGN, ROW_W), BF16),
                            pltpu.SemaphoreType.DMA((2,)),
                            pltpu.SMEM((1,), I32)]),
        out_shape=jax.ShapeDtypeStruct((N_EXPERTS, cap + window, ROW_W), BF16),
        compiler_params=_params(("arbitrary",), VMEM_LIMIT),
    )(offs, slot, h_all, gate_terms)


FF_CHUNK = 512


def _experts_kernel(xp_ref, xs_ref, wg_ref, wu_ref, wd_ref, yp_ref, ys_ref, accp_ref, accs_ref, *, n_f):
    f = pl.program_id(1)
    wg = wg_ref[0].astype(BF16)
    wu = wu_ref[0].astype(BF16)
    wd = wd_ref[0].astype(BF16)

    def ffn(x):
        a = jnp.dot(x, wg, preferred_element_type=F32)
        b = jnp.dot(x, wu, preferred_element_type=F32)
        return jnp.dot((_silu(a) * b).astype(BF16), wd, preferred_element_type=F32)

    @pl.when(f == 0)
    def _():
        accp_ref[...] = ffn(xp_ref[0, :, :D])
        accs_ref[...] = ffn(xs_ref[0, :, :D])

    @pl.when(f > 0)
    def _():
        accp_ref[...] += ffn(xp_ref[0, :, :D])
        accs_ref[...] += ffn(xs_ref[0, :, :D])

    @pl.when(f == n_f - 1)
    def _():
        first = GATE_TERMS * pl.program_id(0)
        for x_ref, y_ref, acc_ref in ((xp_ref, yp_ref, accp_ref), (xs_ref, ys_ref, accs_ref)):
            cap = acc_ref.shape[0]
            lane = lax.broadcasted_iota(I32, (cap, LANES), 1)
            mine = (lane >= first) & (lane < first + GATE_TERMS)
            gate = jnp.sum(jnp.where(mine, x_ref[0, :, D:].astype(F32), 0.0), axis=1, keepdims=True)
            y_ref[0, :cap, :] = (acc_ref[...] * gate).astype(BF16)
            y_ref[0, cap:, :] = jnp.zeros((y_ref.shape[1] - cap, D), BF16)


def _experts(xe_p, xe_s, caps, w_gate, w_up, w_down, layer):
    n_f = EXPERT_FF // FF_CHUNK
    cap_p, cap_s = caps
    spec = lambda rows, width: pl.BlockSpec((1, rows, width), lambda e, f: (e, 0, 0))
    out = lambda xe: jax.ShapeDtypeStruct((N_EXPERTS, xe.shape[1], D), BF16)
    return pl.pallas_call(
        functools.partial(_experts_kernel, n_f=n_f),
        name="experts",
        grid=(N_EXPERTS, n_f),
        in_specs=[spec(cap_p, ROW_W), spec(cap_s, ROW_W),
                  pl.BlockSpec((None, 1, D, FF_CHUNK), lambda e, f: (layer, e, 0, f)),
                  pl.BlockSpec((None, 1, D, FF_CHUNK), lambda e, f: (layer, e, 0, f)),
                  pl.BlockSpec((None, 1, FF_CHUNK, D), lambda e, f: (layer, e, f, 0))],
        out_specs=[spec(xe_p.shape[1], D), spec(xe_s.shape[1], D)],
        out_shape=[out(xe_p), out(xe_s)],
        scratch_shapes=[pltpu.VMEM((cap_p, D), F32), pltpu.VMEM((cap_s, D), F32)],
        compiler_params=_params(("arbitrary", "arbitrary"), VMEM_LIMIT),
    )(xe_p, xe_s, w_gate, w_up, w_down)


def _combine_kernel(offs_ref, slot_ref, x_ref, m_ref, ln_ref, ye_hbm, o_ref, stage_ref, sem, *, n_tiles):
    t = pl.program_id(0)
    buf = t & 1
    window = stage_ref.shape[1] // N_EXPERTS
    per_tile = LANES // window

    def in_copy(b, e, start):
        return pltpu.make_async_copy(ye_hbm.at[e, pl.ds(start, window)],
                                     stage_ref.at[b, pl.ds(e * window, window)], sem.at[b])

    def start_round(b, tile, r):
        for e in range(N_EXPERTS):
            in_copy(b, e, _round_bounds(offs_ref, e, tile, r, window)[2]).start()

    def wait_round(b):
        for e in range(N_EXPERTS):
            in_copy(b, e, 0).wait()

    @pl.when(t == 0)
    def _():
        start_round(0, 0, 0)

    wait_round(buf)

    @pl.when(t + 1 < n_tiles)
    def _():
        start_round(1 - buf, t + 1, 0)

    slot = slot_ref[...]
    lane = lax.broadcasted_iota(I32, (GATHER_TOKENS, LANES), 1)
    which = lane // window
    row_in_window = (lane % window).astype(F32)

    def token_rows(b, r):
        onehots = []
        for e0 in range(0, N_EXPERTS, per_tile):
            s = slot[:, e0:e0 + 1]
            lo, hi, start = _round_bounds(offs_ref, e0, t, r, window)
            for i in range(1, per_tile):
                lo_i, hi_i, start_i = _round_bounds(offs_ref, e0 + i, t, r, window)
                s = jnp.where(which == i, slot[:, e0 + i:e0 + i + 1], s)
                lo, hi, start = (jnp.where(which == i, a_i, a) for a_i, a in ((lo_i, lo), (hi_i, hi), (start_i, start)))
            hit = (s - start.astype(F32) == row_in_window) & (s >= lo.astype(F32)) & (s < hi.astype(F32))
            onehots.append(jnp.where(hit, 1.0, 0.0).astype(BF16))
        return jnp.dot(jnp.concatenate(onehots, axis=1), stage_ref[b], preferred_element_type=F32)

    o_ref[...] = token_rows(buf, 0)

    def extra_round(r, carry):
        start_round(buf, t, r)
        wait_round(buf)
        o_ref[...] += token_rows(buf, r)
        return carry

    lax.fori_loop(1, _n_rounds(offs_ref, t, window), extra_round, 0)
    o_ref[...] = x_ref[...] + m_ref[0, 5:6, :] * (_rms(o_ref[...]) * ln_ref[...])


def _combine(offs, ye, slot_cols, x_all, mod_l, ln, *, row0, n, window):
    n_tiles = n // GATHER_TOKENS
    tile0 = row0 // GATHER_TOKENS
    return pl.pallas_call(
        functools.partial(_combine_kernel, n_tiles=n_tiles),
        name="combine",
        grid_spec=pltpu.PrefetchScalarGridSpec(
            num_scalar_prefetch=1,
            grid=(n_tiles,),
            in_specs=[pl.BlockSpec((GATHER_TOKENS, N_EXPERTS), lambda t, o: (t, 0)),
                      pl.BlockSpec((GATHER_TOKENS, D), lambda t, o: (tile0 + t, 0)),
                      pl.BlockSpec((1, N_MOD, D), lambda t, o: (_mod_row(row0 + t * GATHER_TOKENS), 0, 0)),
                      pl.BlockSpec((1, D), lambda t, o: (0, 0)),
                      pl.BlockSpec(memory_space=pl.ANY)],
            out_specs=pl.BlockSpec((GATHER_TOKENS, D), lambda t, o: (t, 0)),
            scratch_shapes=[pltpu.VMEM((2, N_EXPERTS * window, D), BF16),
                            pltpu.SemaphoreType.DMA((2,))]),
        out_shape=jax.ShapeDtypeStruct((n, D), F32),
        compiler_params=_params(("arbitrary",), VMEM_LIMIT),
    )(offs, slot_cols, x_all, mod_l, ln.reshape(1, D), ye)


def _token_major(blocks):
    return blocks.transpose(0, 2, 1).reshape(-1, N_EXPERTS)


def kernel(x_prompt, x_sample, cache_k, cache_v, state_ret, c, c_ctx, w_mod, b_mod, ln_pre_mix, ln_post_mix,
           ln_pre_ffn, ln_post_ffn, w_in, ret_decay, pool_w, pool_scale, na_rpb, w_ret_o, w_pool_o, w_na_o, w_o,
           w_router, w_gate, w_up, w_down):
    cvecs = jnp.zeros((8, D), F32).at[0].set(c_ctx).at[1:1 + DEC_BATCH].set(c)
    mod = _modulation(cvecs, w_mod, b_mod)
    rope = _rope_tables()
    na_bias = _na_bias(na_rpb)
    x_pair = (x_prompt.reshape(NP, D), x_sample.reshape(NS, D))
    h_layers, new_s = [], []
    groups = ((0, NP, NP // N_EXPERTS * 2, LANES // 2), (NP, NS, NS // N_EXPERTS * 2, LANES))

    for l in range(DEPTH):
        mod_l = mod[l]
        h_all = _prenorm(x_pair, mod_l, ln_pre_mix[l])
        h_layers.append(h_all)
        z = _in_proj(h_all, w_in, l)
        log_g = jax.nn.log_sigmoid(ret_decay[l].astype(F32))

        ret_p, st = _retention_ctx(z, log_g)
        ret_s = _retention(z, log_g, rope, state_ret, l)
        pool_p = _pool(z, pool_w[l], pool_scale[l], nb=BATCH, seq=SEQ, per_step=4, row_block0=0)
        pool_s = _pool(z, pool_w[l], pool_scale[l], nb=DEC_BATCH, seq=DEC_SEQ, per_step=1,
                       row_block0=NP // DEC_SEQ)
        na_p = _ctx_attention(z)
        na_s = _neighbourhood_attention(z, na_bias, cache_k, cache_v, l)
        x_mid, h2, aff = _merge((ret_p, ret_s), (pool_p, pool_s), (na_p, na_s), x_pair, z, w_ret_o, w_pool_o, w_na_o,
                                w_o, mod_l, ln_post_mix[l], ln_pre_ffn[l], w_router[l].T, l)
        new_s.append(st)
        routed = []
        for row0, n, cap, window in groups:
            slot, offs = _route(aff, blk0=row0 // LANES, nblk=n // LANES, cap=cap)
            gate_terms = _gate_terms(_token_major(aff[row0 // LANES:(row0 + n) // LANES]))
            routed.append((slot, offs, _gather(offs, slot, h2, gate_terms, row0=row0, n=n, cap=cap, window=window)))
        ye = _experts(routed[0][2], routed[1][2], (groups[0][2], groups[1][2]), w_gate, w_up, w_down, l)
        outs = []
        for (row0, n, cap, window), (slot, offs, _), y in zip(groups, routed, ye):
            outs.append(_combine(offs, y, _token_major(slot), x_mid, mod_l, ln_post_ffn[l], row0=row0, n=n,
                                 window=window))
        x_pair = tuple(outs)

    y_prompt = x_pair[0].reshape(BATCH, SEQ, D)
    y_sample = x_pair[1].reshape(DEC_BATCH, DEC_SEQ, D)
    new_k, new_v = _kv_proj(h_layers, w_in)
    return (y_prompt, y_sample, new_k, new_v, jnp.stack(new_s, axis=1))
```

```python
import functools

import numpy as np
import jax
import jax.numpy as jnp
from jax import lax
from jax.experimental import pallas as pl
from jax.experimental.pallas import tpu as pltpu

F32 = jnp.float32
BF16 = jnp.bfloat16
I32 = jnp.int32

D = 1024
BATCH, SEQ = 32, 256
DEC_BATCH, DEC_SEQ = 2, 2048
DEPTH = 2
NP = BATCH * SEQ
NS = DEC_BATCH * DEC_SEQ
NT = NP + NS
GRID_W = 64
N_MOD = 6
EPS = 1e-6
ROPE_BASE = 10000.0
HEAD_DIM = 64
N_PAIRS = 4
CHUNK = 512
POOL_WINDOWS = (2, 4, 8, 16)
POOL_PAD = 16
NA_KH, NA_KW = 8, 16
N_EXPERTS = 16
EXPERT_FF = 2048
IN_COLS = 7168
COL_RET_Q, COL_RET_K, COL_RET_V, COL_RET_G, COL_POOL = 0, 512, 1024, 1536, 2048
COL_NA_Q, COL_NA_K, COL_NA_V, COL_GATES = 2560, 3072, 3584, 4096
LANES = 128
VMEM_LIMIT = 56 * 1024 * 1024

NT_DIMS = (((1,), (1,)), ((), ()))


def _params(sem, vmem=None):
    return pltpu.CompilerParams(dimension_semantics=sem, vmem_limit_bytes=vmem)


def _mod_row(row_start):
    return jnp.where(row_start < NP, 0, 1 + (row_start - NP) // DEC_SEQ)


def _sigmoid(x):
    return 0.5 * jnp.tanh(0.5 * x) + 0.5


def _silu(x):
    return x * _sigmoid(x)


def _rms(x):
    return x * lax.rsqrt(jnp.mean(x * x, axis=-1, keepdims=True) + EPS)


def _pair_specs(tm, width):
    n_p = NP // tm
    return [pl.BlockSpec((tm, width), lambda i: (jnp.minimum(i, n_p - 1), 0)),
            pl.BlockSpec((tm, width), lambda i: (jnp.maximum(i - n_p, 0), 0))]


def _pick(p_ref, s_ref):
    return jnp.where(pl.program_id(0) < NP // p_ref.shape[0], p_ref[...], s_ref[...])


def _mod_kernel(c_ref, w_ref, b_ref, o_ref):
    a = _silu(c_ref[...]).astype(BF16)
    o_ref[0] = jnp.dot(a, w_ref[0].astype(BF16), preferred_element_type=F32) + b_ref[0]


def _modulation(cvecs, w_mod, b_mod):
    out = pl.pallas_call(
        _mod_kernel,
        name="modulation",
        grid=(DEPTH, N_MOD),
        in_specs=[pl.BlockSpec((8, D), lambda l, j: (0, 0)),
                  pl.BlockSpec((1, D, D), lambda l, j: (l, 0, j)),
                  pl.BlockSpec((1, 1, D), lambda l, j: (l, 0, j))],
        out_specs=pl.BlockSpec((1, 8, D), lambda l, j: (l, 0, j)),
        out_shape=jax.ShapeDtypeStruct((DEPTH, 8, N_MOD * D), F32),
        compiler_params=_params(("arbitrary", "arbitrary")),
    )(cvecs, w_mod, b_mod.reshape(DEPTH, 1, N_MOD * D))
    return out.reshape(DEPTH, 8, N_MOD, D)


def _prenorm_kernel(xp_ref, xs_ref, m_ref, ln_ref, h_ref):
    y = _rms(_pick(xp_ref, xs_ref)) * ln_ref[...]
    h_ref[...] = (y * (1.0 + m_ref[0, 1:2, :]) + m_ref[0, 0:1, :]).astype(BF16)


def _prenorm(x_pair, mod_l, ln):
    tm = 1024
    return pl.pallas_call(
        _prenorm_kernel,
        name="prenorm",
        grid=(NT // tm,),
        in_specs=_pair_specs(tm, D)
                 + [pl.BlockSpec((1, N_MOD, D), lambda i: (_mod_row(i * tm), 0, 0)),
                  pl.BlockSpec((1, D), lambda i: (0, 0))],
        out_specs=pl.BlockSpec((tm, D), lambda i: (i, 0)),
        out_shape=jax.ShapeDtypeStruct((NT, D), BF16),
        compiler_params=_params(("arbitrary",)),
    )(*x_pair, mod_l, ln.reshape(1, D))


def _mm_kernel(a_ref, w_ref, o_ref, wb_ref):
    @pl.when(pl.program_id(1) == 0)
    def _():
        wb_ref[...] = w_ref[...].astype(BF16)

    o_ref[...] = jnp.dot(a_ref[...], wb_ref[...], preferred_element_type=F32)


def _kv_kernel(h0_ref, h1_ref, w_ref, k_ref, v_ref, wb_ref):
    @pl.when(pl.program_id(1) == 0)
    def _():
        wb_ref[...] = w_ref[0].astype(BF16)

    h = jnp.where(pl.program_id(0) == 0, h0_ref[...], h1_ref[...])
    kv = jnp.dot(h, wb_ref[...], preferred_element_type=F32)
    half = N_PAIRS * LANES
    k_ref[...] = kv[:, :half].reshape(k_ref.shape)
    v_ref[...] = kv[:, half:].reshape(v_ref.shape)


def _kv_proj(h_layers, w_in):
    per = 4
    kv_col = COL_NA_K // D
    hspec = pl.BlockSpec((per * SEQ, D), lambda l, i: (i, 0))
    ospec = pl.BlockSpec((per, 1, SEQ, N_PAIRS * LANES), lambda l, i: (i, l, 0, 0))
    shape = jax.ShapeDtypeStruct((BATCH, DEPTH, SEQ, N_PAIRS * LANES), F32)
    k, v = pl.pallas_call(
        _kv_kernel,
        name="kv_proj",
        grid=(DEPTH, BATCH // per),
        in_specs=[hspec, hspec, pl.BlockSpec((1, D, D), lambda l, i: (l, 0, kv_col))],
        out_specs=[ospec, ospec],
        out_shape=[shape, shape],
        scratch_shapes=[pltpu.VMEM((D, D), BF16)],
        compiler_params=_params(("arbitrary", "arbitrary"), VMEM_LIMIT),
    )(*h_layers, w_in)
    cache_shape = (BATCH, DEPTH, SEQ, 2 * N_PAIRS, HEAD_DIM)
    return k.reshape(cache_shape), v.reshape(cache_shape)


def _in_proj(h_all, w_in, layer):
    tm, tn = 1024, 1792
    return pl.pallas_call(
        _mm_kernel,
        name="in_proj",
        grid=(IN_COLS // tn, NT // tm),
        in_specs=[pl.BlockSpec((tm, D), lambda j, i: (i, 0)),
                  pl.BlockSpec((None, D, tn), lambda j, i: (layer, 0, j))],
        out_specs=pl.BlockSpec((tm, tn), lambda j, i: (i, j)),
        out_shape=jax.ShapeDtypeStruct((NT, IN_COLS), F32),
        scratch_shapes=[pltpu.VMEM((D, tn), BF16)],
        compiler_params=_params(("arbitrary", "arbitrary"), VMEM_LIMIT),
    )(h_all, w_in)


def _swap16(x):
    lane = lax.broadcasted_iota(I32, x.shape, 1)
    return jnp.where((lane // 16) % 2 == 0, pltpu.roll(x, LANES - 16, 1), pltpu.roll(x, 16, 1))


def _block_diag(top, bottom):
    z = jnp.zeros((HEAD_DIM, HEAD_DIM), F32)
    return jnp.concatenate([jnp.concatenate([top, z], axis=1),
                            jnp.concatenate([z, bottom], axis=1)], axis=0)


RET_PAIRS_PER_STEP = 2


def _retention_kernel(lg_ref, q_ref, k_ref, v_ref, g_ref, cos_ref, sin_ref, s0_ref, o_ref, sf_scr, sb_scr):
    for pp in range(RET_PAIRS_PER_STEP):
        _retention_pair(lg_ref, q_ref, k_ref, v_ref, g_ref, cos_ref, sin_ref, s0_ref, o_ref, sf_scr, sb_scr,
                        pp, pl.program_id(1) * RET_PAIRS_PER_STEP + pp)


def _retention_pair(lg_ref, q_ref, k_ref, v_ref, g_ref, cos_ref, sin_ref, s0_ref, o_ref, sf_scr, sb_scr, pp, pair):
    n_chunks = DEC_SEQ // CHUNK
    lanes = slice(pp * LANES, (pp + 1) * LANES)
    lane1 = lax.broadcasted_iota(I32, (1, LANES), 1)
    lo1 = lane1 < HEAD_DIM
    lgf = jnp.where(lo1, lg_ref[0, 2 * pair], lg_ref[0, 2 * pair + 1])
    lgb = jnp.where(lo1, lg_ref[1, 2 * pair], lg_ref[1, 2 * pair + 1])
    lg_heads = [(lg_ref[0, 2 * pair], lg_ref[1, 2 * pair]),
                (lg_ref[0, 2 * pair + 1], lg_ref[1, 2 * pair + 1])]

    rel = (lax.broadcasted_iota(I32, (CHUNK, CHUNK), 0) - lax.broadcasted_iota(I32, (CHUNK, CHUNK), 1)).astype(F32)
    lo_mask = lax.broadcasted_iota(I32, (CHUNK, LANES), 1) < HEAD_DIM
    blockdiag = ((lax.broadcasted_iota(I32, (LANES, LANES), 0) < HEAD_DIM)
                 == (lax.broadcasted_iota(I32, (LANES, LANES), 1) < HEAD_DIM))
    posf = lax.broadcasted_iota(I32, (CHUNK, LANES), 0).astype(F32)
    dmat = []
    for hf, hb in lg_heads:
        dmat.append(jnp.where(rel >= 0, jnp.exp(jnp.where(rel >= 0, rel, 0.0) * hf), 0.0)
                    + jnp.where(rel <= 0, jnp.exp(jnp.where(rel <= 0, -rel, 0.0) * hb), 0.0))
    qdec_f = jnp.exp((posf + 1.0) * lgf)
    kdec_f = jnp.exp((CHUNK - 1.0 - posf) * lgf)
    qdec_b = jnp.exp((CHUNK - posf) * lgb)
    kdec_b = jnp.exp(posf * lgb)
    sdec_f = jnp.exp(CHUNK * lgf)
    sdec_b = jnp.exp(CHUNK * lgb)

    def load(c):
        rows = pl.ds(c * CHUNK, CHUNK)
        cs, sn = cos_ref[rows, :], sin_ref[rows, :]
        q = q_ref[rows, lanes]
        k = k_ref[rows, lanes]
        q = q * cs + _swap16(q) * sn
        k = k * cs + _swap16(k) * sn
        return q, k * (HEAD_DIM ** -0.5), v_ref[rows, lanes]

    def state_update(s, k, v, kdec, sdec):
        kd = (k * kdec).T.astype(BF16)
        u = jnp.dot(kd, v.astype(BF16), preferred_element_type=F32)
        return s * sdec + jnp.where(blockdiag, u, 0.0)

    h0, h1 = 2 * pp, 2 * pp + 1
    s_f = _block_diag(s0_ref[0, 0, 0, h0], s0_ref[0, 0, 0, h1])
    s_b = _block_diag(s0_ref[0, 0, 1, h0], s0_ref[0, 0, 1, h1])
    scr0 = pp * n_chunks
    for c in range(n_chunks):
        sf_scr[scr0 + c] = s_f
        _, k, v = load(c)
        s_f = state_update(s_f, k, v, kdec_f, sdec_f)
    for c in reversed(range(n_chunks)):
        sb_scr[scr0 + c] = s_b
        _, k, v = load(c)
        s_b = state_update(s_b, k, v, kdec_b, sdec_b)

    for c in range(n_chunks):
        q, k, v = load(c)
        qb, kb, vb = q.astype(BF16), k.astype(BF16), v.astype(BF16)
        outs = []
        for h in range(2):
            qh = jnp.where(lo_mask if h == 0 else ~lo_mask, qb, jnp.zeros_like(qb))
            a = lax.dot_general(qh, kb, NT_DIMS, preferred_element_type=F32) * dmat[h]
            outs.append(jnp.dot(a.astype(BF16), vb, preferred_element_type=F32))
        o = jnp.where(lo_mask, outs[0], outs[1])
        o = o + jnp.dot(qb, sf_scr[scr0 + c].astype(BF16), preferred_element_type=F32) * qdec_f
        o = o + jnp.dot(qb, sb_scr[scr0 + c].astype(BF16), preferred_element_type=F32) * qdec_b
        o2 = o * o
        ms0 = jnp.sum(jnp.where(lo_mask, o2, 0.0), axis=1, keepdims=True) * (1.0 / HEAD_DIM)
        ms1 = jnp.sum(jnp.where(lo_mask, 0.0, o2), axis=1, keepdims=True) * (1.0 / HEAD_DIM)
        inv = jnp.where(lo_mask, lax.rsqrt(ms0 + EPS), lax.rsqrt(ms1 + EPS))
        g = g_ref[pl.ds(c * CHUNK, CHUNK), lanes]
        o_ref[pl.ds(c * CHUNK, CHUNK), lanes] = (_silu(g) * (o * inv)).astype(BF16)


def _retention(z, log_g, rope, state_ret, layer):
    pps = RET_PAIRS_PER_STEP
    width = pps * LANES
    row_block0 = NP // DEC_SEQ
    scratch = pltpu.VMEM((pps * (DEC_SEQ // CHUNK), LANES, LANES), F32)

    def zspec(col):
        return pl.BlockSpec((DEC_SEQ, width), lambda b, p: (row_block0 + b, col // width + p))

    table = pl.BlockSpec((DEC_SEQ, LANES), lambda b, p: (0, 0))
    return pl.pallas_call(
        _retention_kernel,
        name="retention",
        grid=(DEC_BATCH, N_PAIRS // pps),
        in_specs=[pl.BlockSpec(memory_space=pltpu.SMEM), zspec(COL_RET_Q), zspec(COL_RET_K), zspec(COL_RET_V),
                  zspec(COL_RET_G), table, table,
                  pl.BlockSpec((1, 1, 2, 2 * pps, HEAD_DIM, HEAD_DIM), lambda b, p: (b, layer, 0, p, 0, 0))],
        out_specs=pl.BlockSpec((DEC_SEQ, width), lambda b, p: (b, p)),
        out_shape=jax.ShapeDtypeStruct((NS, N_PAIRS * LANES), BF16),
        scratch_shapes=[scratch, scratch],
        compiler_params=_params(("arbitrary", "arbitrary"), VMEM_LIMIT),
    )(log_g, z, z, z, z, *rope, state_ret)


def _retention_ctx_kernel(lg_ref, q_ref, k_ref, v_ref, g_ref, o_ref, st_ref, decay_ref):
    heads = 2 * N_PAIRS

    @pl.when(pl.program_id(0) == 0)
    def _():
        rel = (lax.broadcasted_iota(I32, (SEQ, SEQ), 0) - lax.broadcasted_iota(I32, (SEQ, SEQ), 1)).astype(F32)
        for h in range(heads):
            decay_ref[h] = (jnp.where(rel >= 0, jnp.exp(jnp.where(rel >= 0, rel, 0.0) * lg_ref[0, h]), 0.0)
                            + jnp.where(rel <= 0, jnp.exp(jnp.where(rel <= 0, -rel, 0.0) * lg_ref[1, h]), 0.0))

    lane = lax.broadcasted_iota(I32, (SEQ, LANES), 1)
    lo_mask = lane < HEAD_DIM
    pos = lax.broadcasted_iota(I32, (SEQ, LANES), 0).astype(F32)
    blockdiag = ((lax.broadcasted_iota(I32, (LANES, LANES), 0) < HEAD_DIM)
                 == (lax.broadcasted_iota(I32, (LANES, LANES), 1) < HEAD_DIM))
    for pp in range(N_PAIRS):
        lanes = slice(pp * LANES, (pp + 1) * LANES)
        h0, h1 = 2 * pp, 2 * pp + 1
        k = k_ref[:, lanes] * (HEAD_DIM ** -0.5)
        qb, kb, vb = q_ref[:, lanes].astype(BF16), k.astype(BF16), v_ref[:, lanes].astype(BF16)
        outs = []
        for h, mask in ((h0, lo_mask), (h1, ~lo_mask)):
            qh = jnp.where(mask, qb, jnp.zeros_like(qb))
            a = lax.dot_general(qh, kb, NT_DIMS, preferred_element_type=F32) * decay_ref[h]
            outs.append(jnp.dot(a.astype(BF16), vb, preferred_element_type=F32))
        o = jnp.where(lo_mask, outs[0], outs[1])
        o2 = o * o
        ms0 = jnp.sum(jnp.where(lo_mask, o2, 0.0), axis=1, keepdims=True) * (1.0 / HEAD_DIM)
        ms1 = jnp.sum(jnp.where(lo_mask, 0.0, o2), axis=1, keepdims=True) * (1.0 / HEAD_DIM)
        inv = jnp.where(lo_mask, lax.rsqrt(ms0 + EPS), lax.rsqrt(ms1 + EPS))
        o_ref[:, lanes] = (_silu(g_ref[:, lanes]) * (o * inv)).astype(BF16)

        for d, age in ((0, SEQ - 1.0 - pos), (1, pos)):
            lg = jnp.where(lo_mask, lg_ref[d, h0], lg_ref[d, h1])
            kd = (k * jnp.exp(age * lg)).T.astype(BF16)
            s = jnp.where(blockdiag, jnp.dot(kd, vb, preferred_element_type=F32), 0.0)
            st_ref[0, d, h0] = s[:HEAD_DIM, :HEAD_DIM]
            st_ref[0, d, h1] = s[HEAD_DIM:, HEAD_DIM:]


def _retention_ctx(z, log_g):
    width = N_PAIRS * LANES
    heads = 2 * N_PAIRS
    zspec = lambda col: pl.BlockSpec((SEQ, width), lambda b: (b, col // width))
    return pl.pallas_call(
        _retention_ctx_kernel,
        name="retention_ctx",
        grid=(BATCH,),
        in_specs=[pl.BlockSpec(memory_space=pltpu.SMEM), zspec(COL_RET_Q), zspec(COL_RET_K), zspec(COL_RET_V),
                  zspec(COL_RET_G)],
        out_specs=[pl.BlockSpec((SEQ, width), lambda b: (b, 0)),
                   pl.BlockSpec((1, 2, heads, HEAD_DIM, HEAD_DIM), lambda b: (b, 0, 0, 0, 0))],
        out_shape=[jax.ShapeDtypeStruct((NP, width), BF16),
                   jax.ShapeDtypeStruct((BATCH, 2, heads, HEAD_DIM, HEAD_DIM), F32)],
        scratch_shapes=[pltpu.VMEM((heads, SEQ, SEQ), F32)],
        compiler_params=_params(("arbitrary",)),
    )(log_g, z, z, z, z)


def _rope_tables():
    t = np.arange(DEC_SEQ)
    posn = [(t // GRID_W).astype(np.float32), (t % GRID_W).astype(np.float32)]
    nf = HEAD_DIM // 4
    freqs = (1.0 / (np.float32(ROPE_BASE) ** (np.arange(nf, dtype=np.float32) / np.float32(nf)))).astype(np.float32)
    cos = np.zeros((DEC_SEQ, HEAD_DIM), np.float32)
    sin = np.zeros((DEC_SEQ, HEAD_DIM), np.float32)
    for half in range(2):
        ang = (posn[half][:, None] * freqs[None, :]).astype(np.float32)
        for grp in range(2):
            lo = half * 32 + grp * nf
            cos[:, lo:lo + nf] = np.cos(ang)
            sin[:, lo:lo + nf] = np.sin(ang) * (-1.0 if grp == 0 else 1.0)
    return jnp.asarray(np.tile(cos, (1, 2))), jnp.asarray(np.tile(sin, (1, 2)))


def _pool_kernel(u_ref, w_ref, sc_ref, o_ref, *, seq):
    padded = seq + 2 * POOL_PAD
    t = lax.broadcasted_iota(I32, (seq, 1), 0)
    zpad = jnp.zeros((POOL_PAD, LANES), F32)
    for s in range(u_ref.shape[0] // seq):
        rows = slice(s * seq, (s + 1) * seq)
        for gi, w in enumerate(POOL_WINDOWS):
            cols = slice(gi * LANES, (gi + 1) * LANES)
            x = u_ref[rows, cols]
            run = jnp.concatenate([zpad, x, zpad], axis=0)
            span = 1
            while span < w:
                run = run + pltpu.roll(run, padded - span, 0)
                span *= 2
            win = pltpu.roll(run, padded - (POOL_PAD - w // 2), 0)[:seq]
            cnt = (jnp.minimum(t + w // 2, seq) - jnp.maximum(t - w // 2, 0)).astype(F32)
            pooled = win / cnt - x
            mixed = jnp.dot(pooled.astype(BF16), w_ref[gi].astype(BF16), preferred_element_type=F32)
            o_ref[rows, cols] = (mixed * sc_ref[:, cols]).astype(BF16)


def _pool(z, pool_w, pool_scale, *, nb, seq, per_step, row_block0):
    width = len(POOL_WINDOWS) * LANES
    rows = per_step * seq
    return pl.pallas_call(
        functools.partial(_pool_kernel, seq=seq),
        name="pool",
        grid=(nb // per_step,),
        in_specs=[pl.BlockSpec((rows, width), lambda b: (row_block0 + b, COL_POOL // width)),
                  pl.BlockSpec((len(POOL_WINDOWS), LANES, LANES), lambda b: (0, 0, 0)),
                  pl.BlockSpec((1, width), lambda b: (0, 0))],
        out_specs=pl.BlockSpec((rows, width), lambda b: (b, 0)),
        out_shape=jax.ShapeDtypeStruct((nb * seq, width), BF16),
        compiler_params=_params(("arbitrary",), VMEM_LIMIT),
    )(z, pool_w, pool_scale.reshape(1, width))


def _head_select(h, shape):
    lane = lax.broadcasted_iota(I32, shape, 1)
    return (lane < HEAD_DIM) if h == 0 else (lane >= HEAD_DIM)


def _ctx_attn_kernel(q_ref, k_ref, v_ref, o_ref):
    for pp in range(N_PAIRS):
        lanes = slice(pp * LANES, (pp + 1) * LANES)
        qb = (q_ref[:, lanes] * (HEAD_DIM ** -0.5)).astype(BF16)
        kb = k_ref[:, lanes].astype(BF16)
        vb = v_ref[:, lanes].astype(BF16)
        outs = []
        for h in range(2):
            qh = jnp.where(_head_select(h, qb.shape), qb, jnp.zeros_like(qb))
            s = lax.dot_general(qh, kb, NT_DIMS, preferred_element_type=F32)
            p = jnp.exp(s - jnp.max(s, axis=1, keepdims=True))
            denom = jnp.sum(p, axis=1, keepdims=True)
            outs.append(jnp.dot(p.astype(BF16), vb, preferred_element_type=F32) / denom)
        o_ref[:, lanes] = jnp.where(_head_select(0, outs[0].shape), outs[0], outs[1]).astype(BF16)


def _ctx_attention(z):
    width = N_PAIRS * LANES

    def zspec(col):
        return pl.BlockSpec((SEQ, width), lambda b: (b, col // width))

    return pl.pallas_call(
        _ctx_attn_kernel,
        name="ctx_attn",
        grid=(BATCH,),
        in_specs=[zspec(COL_NA_Q), zspec(COL_NA_K), zspec(COL_NA_V)],
        out_specs=pl.BlockSpec((SEQ, width), lambda b: (b, 0)),
        out_shape=jax.ShapeDtypeStruct((NP, width), BF16),
        compiler_params=_params(("arbitrary",)),
    )(z, z, z)


NA_QROWS = 4
NA_QBLK = NA_QROWS * GRID_W
NA_KROWS = 12
NA_NBLK = DEC_SEQ // NA_QBLK


def _na_key_block(i):
    return jnp.clip(i - 1, 0, NA_NBLK - 3)


NA_PATTERNS = ((0, 0), (NA_QROWS, 0), (DEC_SEQ // GRID_W - NA_QROWS, DEC_SEQ // GRID_W - NA_KROWS))
NA_DX_LANE = GRID_W - (NA_KW - 1)


def _na_bias_kernel(rpb_ref, o_ref):
    rows = DEC_SEQ // GRID_W
    q = lax.broadcasted_iota(I32, (GRID_W, LANES), 0)
    lane = lax.broadcasted_iota(I32, (GRID_W, LANES), 1)
    c = lane % GRID_W
    c_start = jnp.clip(q - NA_KW // 2, 0, GRID_W - NA_KW)
    col_ok = (c >= c_start) & (c < c_start + NA_KW)
    lower = lane < GRID_W
    for p, (r0, ks) in enumerate(NA_PATTERNS):
        for rr in range(NA_QROWS):
            r = r0 + rr
            start = min(max(r - NA_KH // 2, 0), rows - NA_KH)
            for kp in range(NA_KROWS // 2):
                halves = []
                for half in range(2):
                    kr = ks + 2 * kp + half
                    if start <= kr < start + NA_KH:
                        row = jnp.broadcast_to(rpb_ref[0, 0, pl.ds(kr - r + NA_KH - 1, 1), :], (GRID_W, LANES))
                        halves.append(pltpu.roll(row, GRID_W * (1 - half), 1, stride=1, stride_axis=0))
                    else:
                        halves.append(None)
                neg = jnp.full((GRID_W, LANES), -jnp.inf, F32)
                lo_half = neg if halves[0] is None else jnp.where(col_ok, halves[0], neg)
                hi_half = neg if halves[1] is None else jnp.where(col_ok, halves[1], neg)
                o_ref[0, p, 0, rr * GRID_W:(rr + 1) * GRID_W, kp * LANES:(kp + 1) * LANES] = (
                    jnp.where(lower, lo_half, hi_half))


def _na_bias(na_rpb):
    ny, nx = 2 * NA_KH - 1, 2 * NA_KW - 1
    padded = jnp.pad(na_rpb.astype(F32), ((0, 0), (0, 0), (0, 16 - ny), (NA_DX_LANE, LANES - NA_DX_LANE - nx)))
    heads = 2 * N_PAIRS
    return pl.pallas_call(
        _na_bias_kernel,
        name="nbr_bias",
        grid=(DEPTH, heads),
        in_specs=[pl.BlockSpec((1, 1, 16, LANES), lambda l, h: (l, h, 0, 0))],
        out_specs=pl.BlockSpec((1, len(NA_PATTERNS), 1, NA_QBLK, NA_KROWS * GRID_W), lambda l, h: (l, 0, h, 0, 0)),
        out_shape=jax.ShapeDtypeStruct((DEPTH, len(NA_PATTERNS), heads, NA_QBLK, NA_KROWS * GRID_W), F32),
        compiler_params=_params(("arbitrary", "arbitrary")),
    )(padded)


def _na_kernel(q_ref, k0_ref, k1_ref, k2_ref, v0_ref, v1_ref, v2_ref, ck_ref, cv_ref, bias_ref, o_ref):
    for pp in range(N_PAIRS):
        lanes = slice(pp * LANES, (pp + 1) * LANES)
        qb = (q_ref[:, lanes] * (HEAD_DIM ** -0.5)).astype(BF16)
        ks = [r[:, lanes].astype(BF16) for r in (k0_ref, k1_ref, k2_ref)] + [ck_ref[0, 0, :, lanes].astype(BF16)]
        vs = [r[:, lanes].astype(BF16) for r in (v0_ref, v1_ref, v2_ref)] + [cv_ref[0, 0, :, lanes].astype(BF16)]
        outs = []
        for h in range(2):
            qh = jnp.where(_head_select(h, qb.shape), qb, jnp.zeros_like(qb))
            ss = []
            for j in range(4):
                s = lax.dot_general(qh, ks[j], NT_DIMS, preferred_element_type=F32)
                if j < 3:
                    s = s + bias_ref[0, 2 * pp + h, :, j * NA_QBLK:(j + 1) * NA_QBLK]
                ss.append(s)
            m = functools.reduce(jnp.maximum, [jnp.max(s, axis=1, keepdims=True) for s in ss])
            ps = [jnp.exp(s - m) for s in ss]
            denom = functools.reduce(jnp.add, [jnp.sum(p, axis=1, keepdims=True) for p in ps])
            acc = functools.reduce(jnp.add, [jnp.dot(p.astype(BF16), v, preferred_element_type=F32)
                                             for p, v in zip(ps, vs)])
            outs.append(acc / denom)
        o_ref[:, lanes] = jnp.where(_head_select(0, outs[0].shape), outs[0], outs[1]).astype(BF16)


def _neighbourhood_attention(z, bias, cache_k, cache_v, layer):
    base = NP // NA_QBLK
    width = N_PAIRS * LANES
    heads = 2 * N_PAIRS

    def kvspec(col, j):
        return pl.BlockSpec((NA_QBLK, width),
                            lambda b, i: (base + b * NA_NBLK + _na_key_block(i) + j, col // width))

    cspec = pl.BlockSpec((1, 1, SEQ, width), lambda b, i: (b, layer, 0, 0))
    pattern = lambda i: jnp.where(i == 0, 0, jnp.where(i == NA_NBLK - 1, 2, 1))
    ck = cache_k.reshape(DEC_BATCH, DEPTH, SEQ, width)
    cv = cache_v.reshape(DEC_BATCH, DEPTH, SEQ, width)
    return pl.pallas_call(
        _na_kernel,
        name="nbr_attn",
        grid=(DEC_BATCH, NA_NBLK),
        in_specs=[pl.BlockSpec((NA_QBLK, width), lambda b, i: (base + b * NA_NBLK + i, COL_NA_Q // width))]
                 + [kvspec(COL_NA_K, j) for j in range(3)] + [kvspec(COL_NA_V, j) for j in range(3)]
                 + [cspec, cspec,
                    pl.BlockSpec((None, 1, heads, NA_QBLK, NA_KROWS * GRID_W),
                                 lambda b, i: (layer, pattern(i), 0, 0, 0))],
        out_specs=pl.BlockSpec((NA_QBLK, width), lambda b, i: (b * NA_NBLK + i, 0)),
        out_shape=jax.ShapeDtypeStruct((NS, width), BF16),
        compiler_params=_params(("arbitrary", "arbitrary"), VMEM_LIMIT),
    )(z, z, z, z, z, z, z, ck, cv, bias)


def _merge_kernel(rp_ref, rs_ref, pp_ref, ps_ref, ap_ref, as_ref, xp_ref, xs_ref, g0_ref, g1_ref, g2_ref,
                  wr_ref, wp_ref, wa_ref, wo_ref, m_ref, ln_ref, ln_ffn_ref, wrt_ref,
                  o_ref, h_ref, aff_ref, wrb, wpb, wab, wob):
    @pl.when(pl.program_id(0) == 0)
    def _():
        wrb[...] = wr_ref[...].astype(BF16)
        wpb[...] = wp_ref[...].astype(BF16)
        wab[...] = wa_ref[...].astype(BF16)
        wob[...] = wo_ref[...].astype(BF16)

    branch = lambda p_ref, s_ref, w: jnp.dot(_pick(p_ref, s_ref), w[...], preferred_element_type=F32)
    merged = (_sigmoid(g0_ref[...]) * branch(rp_ref, rs_ref, wrb)
              + _sigmoid(g1_ref[...]) * branch(pp_ref, ps_ref, wpb)
              + _sigmoid(g2_ref[...]) * branch(ap_ref, as_ref, wab))
    mix = jnp.dot(merged.astype(BF16), wob[...], preferred_element_type=F32)
    x = _pick(xp_ref, xs_ref) + m_ref[0, 2:3, :] * (_rms(mix) * ln_ref[...])
    o_ref[...] = x

    h = (_rms(x) * ln_ffn_ref[...]) * (1.0 + m_ref[0, 4:5, :]) + m_ref[0, 3:4, :]
    hb = h.astype(BF16)
    h_ref[...] = hb
    hl = (h - hb.astype(F32)).astype(BF16)
    w = wrt_ref[...]
    wb = w.astype(BF16)
    wl = (w - wb.astype(F32)).astype(BF16)
    logits = (lax.dot_general(wb, hb, NT_DIMS, preferred_element_type=F32)
              + lax.dot_general(wb, hl, NT_DIMS, preferred_element_type=F32)
              + lax.dot_general(wl, hb, NT_DIMS, preferred_element_type=F32))
    e = jnp.exp(logits - jnp.max(logits, axis=0, keepdims=True))
    aff = e / jnp.sum(e, axis=0, keepdims=True)
    for j in range(aff_ref.shape[0]):
        aff_ref[j] = aff[:, j * LANES:(j + 1) * LANES]


def _merge(ret_pair, pool_pair, na_pair, x_pair, z, w_ret_o, w_pool_o, w_na_o, w_o, mod_l, ln, ln_ffn, w_router_t,
           layer):
    tm = 512
    half = N_PAIRS * LANES
    row = lambda i: (i, 0)
    const = lambda i: (0, 0)
    slab = lambda i: (layer, 0, 0)
    return pl.pallas_call(
        _merge_kernel,
        name="merge",
        grid=(NT // tm,),
        in_specs=_pair_specs(tm, half) * 3 + _pair_specs(tm, D)
                 + [pl.BlockSpec((tm, D), lambda i, c=c: (i, COL_GATES // D + c)) for c in range(3)]
                 + [pl.BlockSpec((None, half, D), slab)] * 3
                 + [pl.BlockSpec((None, D, D), slab),
                    pl.BlockSpec((1, N_MOD, D), lambda i: (_mod_row(i * tm), 0, 0)),
                    pl.BlockSpec((1, D), const),
                    pl.BlockSpec((1, D), const),
                    pl.BlockSpec((N_EXPERTS, D), const)],
        out_specs=[pl.BlockSpec((tm, D), row),
                   pl.BlockSpec((tm, D), row),
                   pl.BlockSpec((tm // LANES, N_EXPERTS, LANES), lambda i: (i, 0, 0))],
        out_shape=[jax.ShapeDtypeStruct((NT, D), F32),
                   jax.ShapeDtypeStruct((NT, D), BF16),
                   jax.ShapeDtypeStruct((NT // LANES, N_EXPERTS, LANES), F32)],
        scratch_shapes=[pltpu.VMEM((half, D), BF16)] * 3 + [pltpu.VMEM((D, D), BF16)],
        compiler_params=_params(("arbitrary",), VMEM_LIMIT),
    )(*ret_pair, *pool_pair, *na_pair, *x_pair, z, z, z, w_ret_o, w_pool_o, w_na_o, w_o, mod_l, ln.reshape(1, D),
      ln_ffn.reshape(1, D), w_router_t)


def _route_kernel(aff_ref, slot_ref, offs_ref, *, cap, nblk):
    as_float = lambda bits: lax.bitcast_convert_type(bits, F32)

    def count(pred):
        return jnp.sum(jnp.sum(jnp.where(pred, 1.0, 0.0), axis=0), axis=1, keepdims=True)

    def search(_, lohi):
        lo, hi = lohi
        mid = lo + ((hi - lo + 1) >> 1)
        ok = count(aff_ref[...] >= as_float(mid)[None]) >= cap
        return jnp.where(ok, mid, lo), jnp.where(ok, hi, mid - 1)

    lo0 = jnp.zeros((N_EXPERTS, 1), I32)
    hi0 = jnp.full((N_EXPERTS, 1), 0x7F800000, I32)
    thr_bits, _ = lax.fori_loop(0, 31, search, (lo0, hi0))
    thr = as_float(thr_bits)
    need = cap - count(aff_ref[...] > thr[None])

    upper = (lax.broadcasted_iota(I32, (LANES, LANES), 0)
             < lax.broadcasted_iota(I32, (LANES, LANES), 1)).astype(BF16)

    def running_count(flags):
        inside = jnp.dot(flags.reshape(nblk * N_EXPERTS, LANES).astype(BF16), upper,
                         preferred_element_type=F32).reshape(nblk, N_EXPERTS, LANES)
        totals = jnp.sum(flags, axis=2, keepdims=True)
        before, run = [], jnp.zeros((N_EXPERTS, 1), F32)
        for b in range(nblk):
            before.append(run)
            run = run + totals[b]
        return inside + jnp.stack(before), before, run

    aff = aff_ref[...]
    tied = aff == thr[None]
    tied_rank, _, _ = running_count(jnp.where(tied, 1.0, 0.0))
    chosen = (aff > thr[None]) | (tied & (tied_rank < need[None]))
    rank, before, total = running_count(jnp.where(chosen, 1.0, 0.0))
    slot_ref[...] = jnp.where(chosen, rank, -1.0)

    lane = lax.broadcasted_iota(I32, (N_EXPERTS, LANES), 1)
    offs = jnp.where(lane >= nblk, total, 0.0)
    for b in range(nblk):
        offs = jnp.where(lane == b, before[b], offs)
    offs_ref[...] = offs.astype(I32)


def _route(aff_blocks, *, blk0, nblk, cap):
    return pl.pallas_call(
        functools.partial(_route_kernel, cap=cap, nblk=nblk),
        name="route",
        grid=(1,),
        in_specs=[pl.BlockSpec((nblk, N_EXPERTS, LANES), lambda i: (blk0 // nblk, 0, 0))],
        out_specs=[pl.BlockSpec((nblk, N_EXPERTS, LANES), lambda i: (0, 0, 0)),
                   pl.BlockSpec((N_EXPERTS, LANES), lambda i: (0, 0))],
        out_shape=[jax.ShapeDtypeStruct((nblk, N_EXPERTS, LANES), F32),
                   jax.ShapeDtypeStruct((N_EXPERTS, LANES), I32)],
        compiler_params=_params(("arbitrary",)),
    )(aff_blocks)


GATHER_TOKENS = 256
TILE_BLOCKS = GATHER_TOKENS // LANES
SLOT_ALIGN = 16
GATE_TERMS = 3
ROW_W = D + LANES


def _round_bounds(offs_ref, e, t, r, window):
    slots = window - SLOT_ALIGN
    off0 = offs_ref[e, TILE_BLOCKS * t]
    off1 = offs_ref[e, TILE_BLOCKS * t + TILE_BLOCKS]
    lo = jnp.minimum(off0 + slots * r, off1)
    hi = jnp.minimum(lo + slots, off1)
    return lo, hi, pl.multiple_of(lo & -SLOT_ALIGN, SLOT_ALIGN)


def _n_rounds(offs_ref, t, window):
    most = jnp.int32(0)
    for e in range(N_EXPERTS):
        most = jnp.maximum(most, offs_ref[e, TILE_BLOCKS * t + TILE_BLOCKS] - offs_ref[e, TILE_BLOCKS * t])
    rounds = jnp.int32(0)
    for filled in range(0, GATHER_TOKENS, window - SLOT_ALIGN):
        rounds = rounds + (most > filled).astype(I32)
    return rounds


def _gate_terms(aff_cols):
    hi = aff_cols.astype(BF16)
    rest = aff_cols - hi.astype(F32)
    mid = rest.astype(BF16)
    lo = (rest - mid.astype(F32)).astype(BF16)
    terms = jnp.stack([hi, mid, lo], axis=-1).reshape(aff_cols.shape[0], GATE_TERMS * N_EXPERTS)
    return jnp.pad(terms, ((0, 0), (0, LANES - GATE_TERMS * N_EXPERTS)))


def _gather_kernel(offs_ref, slot_ref, h_ref, g_ref, xe_hbm, stage_ref, carry_ref, sem, nround_ref, *, n_tiles):
    t = pl.program_id(0)
    window = stage_ref.shape[2]
    cap = xe_hbm.shape[1] - window

    def out_copy(buf, e, start):
        return pltpu.make_async_copy(stage_ref.at[buf, e], xe_hbm.at[e, pl.ds(start, window)], sem.at[buf])

    def wait_round(buf):
        for e in range(N_EXPERTS):
            out_copy(buf, e, 0).wait()

    @pl.when(t == 0)
    def _():
        carry_ref[...] = jnp.zeros_like(carry_ref)
        nround_ref[0] = 0
        stage_ref[0, 0] = jnp.zeros((window, ROW_W), BF16)
        for e in range(N_EXPERTS):
            pltpu.make_async_copy(stage_ref.at[0, 0], xe_hbm.at[e, pl.ds(cap, window)], sem.at[0]).start()
        wait_round(0)

    hb = jnp.concatenate([h_ref[...], g_ref[...]], axis=1)
    sub = lax.broadcasted_iota(I32, (window, GATHER_TOKENS), 0).astype(F32)

    def one_round(r, carry):
        done = nround_ref[0]
        buf = done & 1
        bounds = [_round_bounds(offs_ref, e, t, r, window) for e in range(N_EXPERTS)]
        onehots = []
        for e in range(N_EXPERTS):
            lo, hi, start = bounds[e]
            srow = jnp.concatenate([slot_ref[j, e:e + 1, :] for j in range(TILE_BLOCKS)], axis=1)
            hit = ((srow - start.astype(F32) == sub) & (srow >= lo.astype(F32)) & (srow < hi.astype(F32)))
            onehots.append(jnp.where(hit, 1.0, 0.0).astype(BF16))
        rows = jnp.dot(jnp.concatenate(onehots, axis=0), hb, preferred_element_type=F32)
        for e in range(N_EXPERTS):
            lo, hi, start = bounds[e]
            piece = rows[e * window:(e + 1) * window]
            head = piece[:SLOT_ALIGN] + carry_ref[e].astype(F32)
            stage_ref[buf, e, :SLOT_ALIGN, :] = head.astype(BF16)
            stage_ref[buf, e, SLOT_ALIGN:, :] = piece[SLOT_ALIGN:].astype(BF16)
            tail = pl.multiple_of((hi & -SLOT_ALIGN) - start, SLOT_ALIGN)
            carry_ref[e] = stage_ref[buf, e, pl.ds(tail, SLOT_ALIGN), :]

        @pl.when(done > 0)
        def _():
            wait_round(1 - buf)

        for e in range(N_EXPERTS):
            out_copy(buf, e, bounds[e][2]).start()
        nround_ref[0] = done + 1
        return carry

    lax.fori_loop(0, _n_rounds(offs_ref, t, window), one_round, 0)

    @pl.when((t == n_tiles - 1) & (nround_ref[0] > 0))
    def _():
        wait_round((nround_ref[0] - 1) & 1)


def _gather(offs, slot, h_all, gate_terms, *, row0, n, cap, window):
    n_tiles = n // GATHER_TOKENS
    tile0 = row0 // GATHER_TOKENS
    return pl.pallas_call(
        functools.partial(_gather_kernel, n_tiles=n_tiles),
        name="gather",
        grid_spec=pltpu.PrefetchScalarGridSpec(
            num_scalar_prefetch=1,
            grid=(n_tiles,),
            in_specs=[pl.BlockSpec((TILE_BLOCKS, N_EXPERTS, LANES), lambda t, o: (t, 0, 0)),
                      pl.BlockSpec((GATHER_TOKENS, D), lambda t, o: (tile0 + t, 0)),
                      pl.BlockSpec((GATHER_TOKENS, LANES), lambda t, o: (t, 0))],
            out_specs=pl.BlockSpec(memory_space=pl.ANY),
            scratch_shapes=[pltpu.VMEM((2, N_EXPERTS, window, ROW_W), BF16),
                            pltpu.VMEM((N_EXPERTS, SLOT_ALIGN, ROW_W), BF16),
                            pltpu.SemaphoreType.DMA((2,)),
                            pltpu.SMEM((1,), I32)]),
        out_shape=jax.ShapeDtypeStruct((N_EXPERTS, cap + window, ROW_W), BF16),
        compiler_params=_params(("arbitrary",), VMEM_LIMIT),
    )(offs, slot, h_all, gate_terms)


FF_CHUNK = 512


def _experts_kernel(xp_ref, xs_ref, wg_ref, wu_ref, wd_ref, yp_ref, ys_ref, accp_ref, accs_ref, *, n_f):
    f = pl.program_id(1)
    wg = wg_ref[0].astype(BF16)
    wu = wu_ref[0].astype(BF16)
    wd = wd_ref[0].astype(BF16)

    def ffn(x):
        a = jnp.dot(x, wg, preferred_element_type=F32)
        b = jnp.dot(x, wu, preferred_element_type=F32)
        return jnp.dot((_silu(a) * b).astype(BF16), wd, preferred_element_type=F32)

    @pl.when(f == 0)
    def _():
        accp_ref[...] = ffn(xp_ref[0, :, :D])
        accs_ref[...] = ffn(xs_ref[0, :, :D])

    @pl.when(f > 0)
    def _():
        accp_ref[...] += ffn(xp_ref[0, :, :D])
        accs_ref[...] += ffn(xs_ref[0, :, :D])

    @pl.when(f == n_f - 1)
    def _():
        first = GATE_TERMS * pl.program_id(0)
        for x_ref, y_ref, acc_ref in ((xp_ref, yp_ref, accp_ref), (xs_ref, ys_ref, accs_ref)):
            cap = acc_ref.shape[0]
            lane = lax.broadcasted_iota(I32, (cap, LANES), 1)
            mine = (lane >= first) & (lane < first + GATE_TERMS)
            gate = jnp.sum(jnp.where(mine, x_ref[0, :, D:].astype(F32), 0.0), axis=1, keepdims=True)
            y_ref[0, :cap, :] = (acc_ref[...] * gate).astype(BF16)
            y_ref[0, cap:, :] = jnp.zeros((y_ref.shape[1] - cap, D), BF16)


def _experts(xe_p, xe_s, caps, w_gate, w_up, w_down, layer):
    n_f = EXPERT_FF // FF_CHUNK
    cap_p, cap_s = caps
    spec = lambda rows, width: pl.BlockSpec((1, rows, width), lambda e, f: (e, 0, 0))
    out = lambda xe: jax.ShapeDtypeStruct((N_EXPERTS, xe.shape[1], D), BF16)
    return pl.pallas_call(
        functools.partial(_experts_kernel, n_f=n_f),
        name="experts",
        grid=(N_EXPERTS, n_f),
        in_specs=[spec(cap_p, ROW_W), spec(cap_s, ROW_W),
                  pl.BlockSpec((None, 1, D, FF_CHUNK), lambda e, f: (layer, e, 0, f)),
                  pl.BlockSpec((None, 1, D, FF_CHUNK), lambda e, f: (layer, e, 0, f)),
                  pl.BlockSpec((None, 1, FF_CHUNK, D), lambda e, f: (layer, e, f, 0))],
        out_specs=[spec(xe_p.shape[1], D), spec(xe_s.shape[1], D)],
        out_shape=[out(xe_p), out(xe_s)],
        scratch_shapes=[pltpu.VMEM((cap_p, D), F32), pltpu.VMEM((cap_s, D), F32)],
        compiler_params=_params(("arbitrary", "arbitrary"), VMEM_LIMIT),
    )(xe_p, xe_s, w_gate, w_up, w_down)


def _combine_kernel(offs_ref, slot_ref, x_ref, m_ref, ln_ref, ye_hbm, o_ref, stage_ref, sem, *, n_tiles):
    t = pl.program_id(0)
    buf = t & 1
    window = stage_ref.shape[1] // N_EXPERTS
    per_tile = LANES // window

    def in_copy(b, e, start):
        return pltpu.make_async_copy(ye_hbm.at[e, pl.ds(start, window)],
                                     stage_ref.at[b, pl.ds(e * window, window)], sem.at[b])

    def start_round(b, tile, r):
        for e in range(N_EXPERTS):
            in_copy(b, e, _round_bounds(offs_ref, e, tile, r, window)[2]).start()

    def wait_round(b):
        for e in range(N_EXPERTS):
            in_copy(b, e, 0).wait()

    @pl.when(t == 0)
    def _():
        start_round(0, 0, 0)

    wait_round(buf)

    @pl.when(t + 1 < n_tiles)
    def _():
        start_round(1 - buf, t + 1, 0)

    slot = slot_ref[...]
    lane = lax.broadcasted_iota(I32, (GATHER_TOKENS, LANES), 1)
    which = lane // window
    row_in_window = (lane % window).astype(F32)

    def token_rows(b, r):
        onehots = []
        for e0 in range(0, N_EXPERTS, per_tile):
            s = slot[:, e0:e0 + 1]
            lo, hi, start = _round_bounds(offs_ref, e0, t, r, window)
            for i in range(1, per_tile):
                lo_i, hi_i, start_i = _round_bounds(offs_ref, e0 + i, t, r, window)
                s = jnp.where(which == i, slot[:, e0 + i:e0 + i + 1], s)
                lo, hi, start = (jnp.where(which == i, a_i, a) for a_i, a in ((lo_i, lo), (hi_i, hi), (start_i, start)))
            hit = (s - start.astype(F32) == row_in_window) & (s >= lo.astype(F32)) & (s < hi.astype(F32))
            onehots.append(jnp.where(hit, 1.0, 0.0).astype(BF16))
        return jnp.dot(jnp.concatenate(onehots, axis=1), stage_ref[b], preferred_element_type=F32)

    o_ref[...] = token_rows(buf, 0)

    def extra_round(r, carry):
        start_round(buf, t, r)
        wait_round(buf)
        o_ref[...] += token_rows(buf, r)
        return carry

    lax.fori_loop(1, _n_rounds(offs_ref, t, window), extra_round, 0)
    o_ref[...] = x_ref[...] + m_ref[0, 5:6, :] * (_rms(o_ref[...]) * ln_ref[...])


def _combine(offs, ye, slot_cols, x_all, mod_l, ln, *, row0, n, window):
    n_tiles = n // GATHER_TOKENS
    tile0 = row0 // GATHER_TOKENS
    return pl.pallas_call(
        functools.partial(_combine_kernel, n_tiles=n_tiles),
        name="combine",
        grid_spec=pltpu.PrefetchScalarGridSpec(
            num_scalar_prefetch=1,
            grid=(n_tiles,),
            in_specs=[pl.BlockSpec((GATHER_TOKENS, N_EXPERTS), lambda t, o: (t, 0)),
                      pl.BlockSpec((GATHER_TOKENS, D), lambda t, o: (tile0 + t, 0)),
                      pl.BlockSpec((1, N_MOD, D), lambda t, o: (_mod_row(row0 + t * GATHER_TOKENS), 0, 0)),
                      pl.BlockSpec((1, D), lambda t, o: (0, 0)),
                      pl.BlockSpec(memory_space=pl.ANY)],
            out_specs=pl.BlockSpec((GATHER_TOKENS, D), lambda t, o: (t, 0)),
            scratch_shapes=[pltpu.VMEM((2, N_EXPERTS * window, D), BF16),
                            pltpu.SemaphoreType.DMA((2,))]),
        out_shape=jax.ShapeDtypeStruct((n, D), F32),
        compiler_params=_params(("arbitrary",), VMEM_LIMIT),
    )(offs, slot_cols, x_all, mod_l, ln.reshape(1, D), ye)


def _token_major(blocks):
    return blocks.transpose(0, 2, 1).reshape(-1, N_EXPERTS)


def kernel(x_prompt, x_sample, cache_k, cache_v, state_ret, c, c_ctx, w_mod, b_mod, ln_pre_mix, ln_post_mix,
           ln_pre_ffn, ln_post_ffn, w_in, ret_decay, pool_w, pool_scale, na_rpb, w_ret_o, w_pool_o, w_na_o, w_o,
           w_router, w_gate, w_up, w_down):
    cvecs = jnp.zeros((8, D), F32).at[0].set(c_ctx).at[1:1 + DEC_BATCH].set(c)
    mod = _modulation(cvecs, w_mod, b_mod)
    rope = _rope_tables()
    na_bias = _na_bias(na_rpb)
    x_pair = (x_prompt.reshape(NP, D), x_sample.reshape(NS, D))
    h_layers, new_s = [], []
    groups = ((0, NP, NP // N_EXPERTS * 2, LANES // 2), (NP, NS, NS // N_EXPERTS * 2, LANES))

    for l in range(DEPTH):
        mod_l = mod[l]
        h_all = _prenorm(x_pair, mod_l, ln_pre_mix[l])
        h_layers.append(h_all)
        z = _in_proj(h_all, w_in, l)
        log_g = jax.nn.log_sigmoid(ret_decay[l].astype(F32))

        ret_p, st = _retention_ctx(z, log_g)
        ret_s = _retention(z, log_g, rope, state_ret, l)
        pool_p = _pool(z, pool_w[l], pool_scale[l], nb=BATCH, seq=SEQ, per_step=4, row_block0=0)
        pool_s = _pool(z, pool_w[l], pool_scale[l], nb=DEC_BATCH, seq=DEC_SEQ, per_step=1,
                       row_block0=NP // DEC_SEQ)
        na_p = _ctx_attention(z)
        na_s = _neighbourhood_attention(z, na_bias, cache_k, cache_v, l)
        x_mid, h2, aff = _merge((ret_p, ret_s), (pool_p, pool_s), (na_p, na_s), x_pair, z, w_ret_o, w_pool_o, w_na_o,
                                w_o, mod_l, ln_post_mix[l], ln_pre_ffn[l], w_router[l].T, l)
        new_s.append(st)
        routed = []
        for row0, n, cap, window in groups:
            slot, offs = _route(aff, blk0=row0 // LANES, nblk=n // LANES, cap=cap)
            gate_terms = _gate_terms(_token_major(aff[row0 // LANES:(row0 + n) // LANES]))
            routed.append((slot, offs, _gather(offs, slot, h2, gate_terms, row0=row0, n=n, cap=cap, window=window)))
        ye = _experts(routed[0][2], routed[1][2], (groups[0][2], groups[1][2]), w_gate, w_up, w_down, l)
        outs = []
        for (row0, n, cap, window), (slot, offs, _), y in zip(groups, routed, ye):
            outs.append(_combine(offs, y, _token_major(slot), x_mid, mod_l, ln_post_ffn[l], row0=row0, n=n,
                                 window=window))
        x_pair = tuple(outs)

    y_prompt = x_pair[0].reshape(BATCH, SEQ, D)
    y_sample = x_pair[1].reshape(DEC_BATCH, DEC_SEQ, D)
    new_k, new_v = _kv_proj(h_layers, w_in)
    return (y_prompt, y_sample, new_k, new_v, jnp.stack(new_s, axis=1))
```
